```python
import math
import jax, jax.numpy as jnp
from jax import lax
import numpy as np

D_MODEL = 1024
BATCH = 4
SEQ = 4096
DEPTH = 1
DEC_BATCH = 32
DEC_SEQ = 8
PAST_LEN = 16384
PAGE_SIZE = 128

H_A = 8
DH_A = D_MODEL // 16
A_W = H_A * DH_A
MOBA_BLOCK = 256
MOBA_TOPK = 3
MOBA_Q_CHUNK = 64
ROT_DIMS = DH_A // 4
ROPE_THETA = 500000.0
H_M = 4
DH_M = D_MODEL // 8
M_W = H_M * DH_M
MLSTM_CHUNK = 128
FORGET_BIAS_LO = 3.0
FORGET_BIAS_HI = 6.0
D_FF = 4 * D_MODEL
RMS_EPS = 1e-6
SPLIT_SIZES = (A_W, A_W, A_W, M_W, M_W, M_W, M_W, H_M, H_M, D_MODEL, D_MODEL)
D_IN = sum(SPLIT_SIZES)

kernel_name = 'moba_mlstm_gated_hybrid_step'


def _rmsnorm(x, g):
    xf = x.astype(jnp.float32)
    y = xf * lax.rsqrt(jnp.mean(xf * xf, axis=-1, keepdims=True) + RMS_EPS)
    return (y * g.astype(jnp.float32)).astype(x.dtype)


def _rotary(x, pos):
    half = ROT_DIMS // 2
    inv = ROPE_THETA ** (-jnp.arange(half, dtype=jnp.float32) * 2.0 / ROT_DIMS)
    ang = pos[:, None] * inv[None, :]
    cos = jnp.cos(ang)[:, None, :]
    sin = jnp.sin(ang)[:, None, :]
    xf = x.astype(jnp.float32)
    x1 = xf[..., :half]
    x2 = xf[..., half:ROT_DIMS]
    out = jnp.concatenate([x1 * cos - x2 * sin, x2 * cos + x1 * sin, xf[..., ROT_DIMS:]], axis=-1)
    return out.astype(x.dtype)


def _project(x, g_pre_mix, w_in, b_if, pos):
    B, S, _ = x.shape
    h = _rmsnorm(x, g_pre_mix)
    z = h @ w_in
    splits = [int(s) for s in np.cumsum(SPLIT_SIZES)[:-1]]
    qa, ka, va, qm, km, vm, om, ip, fp, ga, gm = jnp.split(z, splits, axis=-1)
    qa = _rotary(qa.reshape(B, S, H_A, DH_A), pos)
    ka = _rotary(ka.reshape(B, S, H_A, DH_A), pos)
    va = va.reshape(B, S, H_A, DH_A)
    qm = qm.reshape(B, S, H_M, DH_M)
    km = km.reshape(B, S, H_M, DH_M) * (DH_M ** -0.5)
    vm = vm.reshape(B, S, H_M, DH_M)
    gates = jnp.concatenate([ip, fp], axis=-1).astype(jnp.float32) + b_if.astype(jnp.float32)
    i_pre = gates[..., :H_M]
    log_f = jax.nn.log_sigmoid(gates[..., H_M:])
    return qa, ka, va, qm, km, vm, om, i_pre, log_f, ga, gm


def _mlstm(q, k, v, o_pre, i_pre, log_f, C0, n0, m0):
    B, S, H, D = q.shape
    f32 = jnp.float32
    L = math.gcd(S, MLSTM_CHUNK)
    n_chunks = S // L

    def to_chunks(a):
        return jnp.moveaxis(a.astype(f32).reshape((B, n_chunks, L) + a.shape[2:]), 1, 0)

    xs = (to_chunks(q), to_chunks(k), to_chunks(v), to_chunks(i_pre), to_chunks(log_f))
    causal = jnp.tril(jnp.ones((L, L), dtype=bool))[None, :, :, None]

    def step(carry, inp):
        C, n, m = carry
        qc, kc, vc, ic, fc = inp
        b = jnp.cumsum(fc, axis=1)
        logD = b[:, :, None, :] - b[:, None, :, :] + ic[:, None, :, :]
        logD = jnp.where(causal, logD, -jnp.inf)
        m_t = jnp.maximum(m[:, None, :] + b, jnp.max(logD, axis=2))
        wD = jnp.exp(logD - m_t[:, :, None, :])
        w_prev = jnp.exp(m[:, None, :] + b - m_t)
        A = jnp.einsum('bthd,bshd->btsh', qc, kc) * wD
        num = jnp.einsum('btsh,bshd->bthd', A, vc) + w_prev[..., None] * jnp.einsum('bhde,bthe->bthd', C, qc)
        den = jnp.sum(A, axis=2) + w_prev * jnp.einsum('bhd,bthd->bth', n, qc)
        hc = num / jnp.maximum(jnp.abs(den), jnp.exp(-m_t))[..., None]
        m_new = m_t[:, -1]
        decay = jnp.exp(m + b[:, -1] - m_new)
        ws = jnp.exp(b[:, -1:, :] - b + ic - m_new[:, None, :])
        C_new = decay[..., None, None] * C + jnp.einsum('bsh,bshd,bshe->bhde', ws, vc, kc)
        n_new = decay[..., None] * n + jnp.einsum('bsh,bshd->bhd', ws, kc)
        return (C_new, n_new, m_new), hc

    init = (C0.astype(f32), n0.astype(f32), m0.astype(f32))
    (C, n, m), hs = lax.scan(step, init, xs)
    h = jnp.moveaxis(hs, 0, 1).reshape(B, S, H, D)
    h = jax.nn.sigmoid(o_pre.astype(f32)).reshape(B, S, H, D) * h
    return h.astype(q.dtype), C, n, m


def _moba_select(q, k_mean, j):
    NB = k_mean.shape[1]
    K = min(MOBA_TOPK, NB)
    s = jnp.einsum('bthd,bnhd->bhtn', q.astype(jnp.float32), k_mean)
    s = jnp.where(jnp.arange(NB) < j, s, -jnp.inf)
    _, idx = lax.top_k(s, K)
    valid = jnp.arange(K) < j
    return idx, valid


def _moba_core(q, k_sel, v_sel, sel_valid, k_own, v_own, own_mask):
    scale = DH_A ** -0.5
    s_own = jnp.einsum('bthd,bohd->bhto', q, k_own).astype(jnp.float32) * scale
    s_own = jnp.where(own_mask[None, None], s_own, -jnp.inf)
    if k_sel is None:
        p = jax.nn.softmax(s_own, axis=-1).astype(v_own.dtype)
        return jnp.einsum('bhto,bohd->bthd', p, v_own)
    B, H, T, K, L, D = k_sel.shape
    s_sel = jnp.einsum('bthd,bhtkld->bhtkl', q, k_sel).astype(jnp.float32) * scale
    s_sel = jnp.where(sel_valid[:, None], s_sel, -jnp.inf).reshape(B, H, T, K * L)
    p = jax.nn.softmax(jnp.concatenate([s_sel, s_own], axis=-1), axis=-1).astype(v_own.dtype)
    out = jnp.einsum('bhtn,bhtnd->bthd', p[..., :K * L], v_sel.reshape(B, H, T, K * L, D))
    return out + jnp.einsum('bhto,bohd->bthd', p[..., K * L:], v_own)


def _moba_prompt(q, k, v):
    B, S, H, D = q.shape
    nb_full = S // MOBA_BLOCK
    nb_tot = -(-S // MOBA_BLOCK)
    pad = nb_tot * MOBA_BLOCK - S
    k_pad = jnp.pad(k, ((0, 0), (0, pad), (0, 0), (0, 0)))
    v_pad = jnp.pad(v, ((0, 0), (0, pad), (0, 0), (0, 0)))
    qc_len = math.gcd(S, MOBA_Q_CHUNK)
    if nb_full > 0:
        kb = k[:, :nb_full * MOBA_BLOCK].reshape(B, nb_full, MOBA_BLOCK, H, D)
        vb = v[:, :nb_full * MOBA_BLOCK].reshape(B, nb_full, MOBA_BLOCK, H, D)
        k_mean = jnp.mean(kb, axis=2, dtype=jnp.float32)
    b_i = jnp.arange(B)[:, None, None, None]
    h_i = jnp.arange(H)[None, :, None, None]

    def chunk(c):
        s0 = c * qc_len
        qc = lax.dynamic_slice_in_dim(q, s0, qc_len, axis=1)
        j = s0 // MOBA_BLOCK
        start = j * MOBA_BLOCK
        k_own = lax.dynamic_slice_in_dim(k_pad, start, MOBA_BLOCK, axis=1)
        v_own = lax.dynamic_slice_in_dim(v_pad, start, MOBA_BLOCK, axis=1)
        own_mask = (start + jnp.arange(MOBA_BLOCK))[None, :] <= (s0 + jnp.arange(qc_len))[:, None]
        if nb_full > 0:
            idx, valid = _moba_select(qc, k_mean, j)
            k_sel = kb[b_i, idx, :, h_i]
            v_sel = vb[b_i, idx, :, h_i]
        else:
            k_sel, v_sel, valid = None, None, None
        return _moba_core(qc, k_sel, v_sel, valid, k_own, v_own, own_mask)

    out = lax.map(chunk, jnp.arange(S // qc_len))
    return jnp.moveaxis(out, 0, 1).reshape(B, S, H, D)


def _moba_sample(q, k, v, cache_k, cache_v, page_table):
    DB, T, H, D = q.shape
    n_pages = page_table.shape[1]
    ps = cache_k.shape[1]
    past = n_pages * ps
    ppb = MOBA_BLOCK // ps
    nb_full = past // MOBA_BLOCK
    r_own = past - nb_full * MOBA_BLOCK
    own_pages = page_table[:, nb_full * ppb:]
    k_own = jnp.concatenate([cache_k[own_pages].reshape(DB, r_own, H, D), k], axis=1)
    v_own = jnp.concatenate([cache_v[own_pages].reshape(DB, r_own, H, D), v], axis=1)
    own_mask = jnp.concatenate([jnp.ones((T, r_own), dtype=bool),
                                jnp.tril(jnp.ones((T, T), dtype=bool))], axis=1)
    if nb_full > 0:
        kb_past = cache_k[page_table[:, :nb_full * ppb]].reshape(DB, nb_full, MOBA_BLOCK, H, D)
        k_mean = jnp.mean(kb_past, axis=2, dtype=jnp.float32)
        idx, valid = _moba_select(q, k_mean, nb_full)
        logical = idx[..., None] * ppb + jnp.arange(ppb)
        phys = page_table[jnp.arange(DB)[:, None, None, None, None], logical]
        h_i = jnp.arange(H)[None, :, None, None, None]
        K = idx.shape[-1]
        k_sel = cache_k[phys, :, h_i].reshape(DB, H, T, K, MOBA_BLOCK, D)
        v_sel = cache_v[phys, :, h_i].reshape(DB, H, T, K, MOBA_BLOCK, D)
    else:
        k_sel, v_sel, valid = None, None, None
    return _moba_core(q, k_sel, v_sel, valid, k_own, v_own, own_mask)


def _merge_mlp(x, att, ml, ga, gm, w_attn_br, w_mlstm_br, w_out, g_post_mix, g_pre_mlp, w_up, w_down, g_post_mlp):
    B, S, _ = x.shape
    ya = att.reshape(B, S, A_W) @ w_attn_br
    ym = ml.reshape(B, S, M_W) @ w_mlstm_br
    u = jax.nn.sigmoid(ga) * ya + jax.nn.sigmoid(gm) * ym
    x = x + _rmsnorm(u @ w_out, g_post_mix)
    h = _rmsnorm(x, g_pre_mlp)
    f = jnp.square(jax.nn.relu(h @ w_up)) @ w_down
    return x + _rmsnorm(f, g_post_mlp)


def setup_inputs(seed: int = 0) -> dict:
    key = jax.random.key(seed)
    ks = jax.random.split(key, 24)
    n_pages = PAST_LEN // PAGE_SIZE
    n_pool = (DEC_BATCH * n_pages * 5) // 4
    nrm = jax.random.normal
    f32 = jnp.float32
    x_prompt = nrm(ks[0], (BATCH, SEQ, D_MODEL), f32)
    x_sample = nrm(ks[1], (DEC_BATCH, DEC_SEQ, D_MODEL), f32)
    cache_k = nrm(ks[2], (n_pool, PAGE_SIZE, H_A, DH_A), f32)
    cache_v = nrm(ks[3], (n_pool, PAGE_SIZE, H_A, DH_A), f32)
    state_C = 0.3 * nrm(ks[4], (DEC_BATCH, H_M, DH_M, DH_M), f32)
    state_n = 0.3 * nrm(ks[5], (DEC_BATCH, H_M, DH_M), f32)
    state_m = nrm(ks[6], (DEC_BATCH, H_M), f32)
    page_table = jax.random.permutation(ks[7], n_pool)[:DEC_BATCH * n_pages].reshape(DEC_BATCH, n_pages).astype(jnp.int32)
    g_pre_mix = 1.0 + 0.05 * nrm(ks[8], (D_MODEL,), f32)
    w_in = nrm(ks[9], (D_MODEL, D_IN), f32) * D_MODEL ** -0.5
    b_if = jnp.concatenate([0.1 * nrm(ks[10], (H_M,), f32),
                            jnp.linspace(FORGET_BIAS_LO, FORGET_BIAS_HI, H_M, dtype=f32) + 0.1 * nrm(ks[11], (H_M,), f32)])
    w_attn_br = nrm(ks[12], (A_W, D_MODEL), f32) * A_W ** -0.5
    w_mlstm_br = nrm(ks[13], (M_W, D_MODEL), f32) * M_W ** -0.5
    w_out = nrm(ks[14], (D_MODEL, D_MODEL), f32) * D_MODEL ** -0.5
    g_post_mix = 1.0 + 0.05 * nrm(ks[15], (D_MODEL,), f32)
    g_pre_mlp = 1.0 + 0.05 * nrm(ks[16], (D_MODEL,), f32)
    w_up = nrm(ks[17], (D_MODEL, D_FF), f32) * D_MODEL ** -0.5
    w_down = nrm(ks[18], (D_FF, D_MODEL), f32) * D_FF ** -0.5
    g_post_mlp = 1.0 + 0.05 * nrm(ks[19], (D_MODEL,), f32)
    return {'x_prompt': x_prompt, 'x_sample': x_sample, 'cache_k': cache_k, 'cache_v': cache_v,
            'state_C': state_C, 'state_n': state_n, 'state_m': state_m, 'page_table': page_table,
            'g_pre_mix': g_pre_mix, 'w_in': w_in, 'b_if': b_if, 'w_attn_br': w_attn_br,
            'w_mlstm_br': w_mlstm_br, 'w_out': w_out, 'g_post_mix': g_post_mix, 'g_pre_mlp': g_pre_mlp,
            'w_up': w_up, 'w_down': w_down, 'g_post_mlp': g_post_mlp}


def reference(x_prompt, x_sample, cache_k, cache_v, state_C, state_n, state_m, page_table,
              g_pre_mix, w_in, b_if, w_attn_br, w_mlstm_br, w_out, g_post_mix, g_pre_mlp,
              w_up, w_down, g_post_mlp):
    f32 = jnp.float32
    Bp, Sp, _ = x_prompt.shape
    pos_p = jnp.arange(Sp, dtype=f32)
    qa, ka_p, va_p, qm, km, vm, om, ip, lf, ga, gm = _project(x_prompt, g_pre_mix, w_in, b_if, pos_p)
    att_p = _moba_prompt(qa, ka_p, va_p)
    C0 = jnp.zeros((Bp, H_M, DH_M, DH_M), f32)
    n0 = jnp.zeros((Bp, H_M, DH_M), f32)
    m0 = jnp.zeros((Bp, H_M), f32)
    ml_p, C_p, n_p, m_p = _mlstm(qm, km, vm, om, ip, lf, C0, n0, m0)
    y_prompt = _merge_mlp(x_prompt, att_p, ml_p, ga, gm, w_attn_br, w_mlstm_br, w_out,
                          g_post_mix, g_pre_mlp, w_up, w_down, g_post_mlp)
    Ds = x_sample.shape[1]
    past = page_table.shape[1] * cache_k.shape[1]
    pos_s = past + jnp.arange(Ds, dtype=f32)
    qa, ka_s, va_s, qm, km, vm, om, ip, lf, ga, gm = _project(x_sample, g_pre_mix, w_in, b_if, pos_s)
    att_s = _moba_sample(qa, ka_s, va_s, cache_k, cache_v, page_table)
    ml_s, C_s, n_s, m_s = _mlstm(qm, km, vm, om, ip, lf, state_C, state_n, state_m)
    y_sample = _merge_mlp(x_sample, att_s, ml_s, ga, gm, w_attn_br, w_mlstm_br, w_out,
                          g_post_mix, g_pre_mlp, w_up, w_down, g_post_mlp)
    return (y_prompt, y_sample, ka_p, va_p, C_p, n_p, m_p, ka_s, va_s, C_s, n_s, m_s)
```

```python
import functools
import math

import jax
import jax.numpy as jnp
from jax import lax
from jax.experimental import pallas as pl
from jax.experimental.pallas import tpu as pltpu

F32 = jnp.float32
BF16 = jnp.bfloat16
NEG_INF = float("-inf")

D_MODEL = 1024
H_A = 8
DH_A = 64
A_W = H_A * DH_A
MOBA_BLOCK = 256
MOBA_TOPK = 3
ROT_DIMS = DH_A // 4
ROPE_THETA = 500000.0
H_M = 4
DH_M = 128
M_W = H_M * DH_M
MLSTM_CHUNK = 128
D_FF = 4 * D_MODEL
RMS_EPS = 1e-6
N_GATE = 2 * H_M

LANES = 128
HEADS_PER_LANE_GROUP = LANES // DH_A
VMEM_LIMIT_BYTES = 52 * 1024 * 1024

NT_DIMS = (((1,), (1,)), ((), ()))
TN_DIMS = (((0,), (0,)), ((), ()))


def _compiler_params(semantics):
    return pltpu.CompilerParams(dimension_semantics=semantics, vmem_limit_bytes=VMEM_LIMIT_BYTES)


def _rms(x, g):
    return x * lax.rsqrt(jnp.mean(x * x, axis=-1, keepdims=True) + RMS_EPS) * g


def _log_sigmoid(x):
    return jnp.minimum(x, 0.0) - jnp.log1p(jnp.exp(-jnp.abs(x)))


PROJ_ROWS = MOBA_BLOCK


def _project_kernel(x_ref, g_ref, w_ref, wg_ref, bif_ref, rc_ref, rs1_ref, rs2_ref, *out_refs, feature_major):
    if feature_major:
        (qt_ref, kt_ref, vt_ref, kb_ref, vtb_ref, kmean_ref,
         qm_ref, km_ref, vm_ref, om_ref, ga_ref, gm_ref, gates_ref) = out_refs
    else:
        qa_ref, ka_ref, va_ref, qm_ref, km_ref, vm_ref, om_ref, ga_ref, gm_ref, gates_ref = out_refs
    hb = _rms(x_ref[...], g_ref[...]).astype(BF16)

    def mm(c0, n):
        return jnp.dot(hb, w_ref[:, c0:c0 + n], preferred_element_type=F32)

    rc, rs1, rs2 = rc_ref[...], rs1_ref[...], rs2_ref[...]

    def rot(z):
        half = ROT_DIMS // 2
        outs = []
        for c in range(z.shape[1] // LANES):
            zc = z[:, c * LANES:(c + 1) * LANES]
            outs.append(zc * rc + pltpu.roll(zc, LANES - half, 1) * rs1 + pltpu.roll(zc, half, 1) * rs2)
        return jnp.concatenate(outs, axis=1)

    qa = rot(mm(0, A_W))
    ka = rot(mm(A_W, A_W))
    va = mm(2 * A_W, A_W)
    if feature_major:
        qt_ref[0] = qa.T
        kt_ref[0] = ka.T
        kb_ref[...] = ka.astype(BF16)
        kmean_ref[...] = jnp.mean(ka, axis=0, keepdims=True).reshape(1, 1, A_W)
        vt = va.T
        vt_ref[0] = vt
        vtb_ref[0, 0] = vt.astype(BF16)
    else:
        qa_ref[...] = qa
        ka_ref[...] = ka
        va_ref[...] = va
    c0 = 3 * A_W
    qm_ref[...] = mm(c0, M_W)
    km_ref[...] = mm(c0 + M_W, M_W) * (DH_M ** -0.5)
    vm_ref[...] = mm(c0 + 2 * M_W, M_W)
    om_ref[...] = mm(c0 + 3 * M_W, M_W)
    c1 = c0 + 4 * M_W
    ga_ref[...] = mm(c1, D_MODEL)
    gm_ref[...] = mm(c1 + D_MODEL, D_MODEL)
    zg = jnp.dot(hb, wg_ref[...], preferred_element_type=F32) + bif_ref[...]
    lane = lax.broadcasted_iota(jnp.int32, zg.shape, 1)
    is_forget = (lane >= H_M) & (lane < N_GATE)
    gates_ref[...] = jnp.where(is_forget, _log_sigmoid(zg), zg)


def _rotary_tables(pos):
    half = ROT_DIMS // 2
    inv = ROPE_THETA ** (-jnp.arange(half, dtype=F32) * 2.0 / ROT_DIMS)
    ang = pos[:, None] * inv[None, :]
    cos, sin = jnp.cos(ang), jnp.sin(ang)
    n = pos.shape[0]
    pad = jnp.zeros((n, DH_A - ROT_DIMS), F32)
    c_head = jnp.concatenate([cos, cos, pad + 1.0], axis=1)
    s1_head = jnp.concatenate([-sin, jnp.zeros_like(sin), pad], axis=1)
    s2_head = jnp.concatenate([jnp.zeros_like(sin), sin, pad], axis=1)
    rep = lambda t: jnp.tile(t, (1, HEADS_PER_LANE_GROUP))
    return rep(c_head), rep(s1_head), rep(s2_head)


def _project(x2d, pos, g_pre_mix, w_main, w_gate, b_if_row, feature_major):
    n = x2d.shape[0]
    tm = PROJ_ROWS
    n_tiles = n // tm
    rc, rs1, rs2 = _rotary_tables(pos)
    if pos.shape[0] < tm:
        reps = tm // pos.shape[0]
        rc, rs1, rs2 = (jnp.tile(t, (reps, 1)) for t in (rc, rs1, rs2))
    tab_tiles = rc.shape[0] // tm
    row = lambda w: pl.BlockSpec((tm, w), lambda i: (i, 0))
    full = lambda a: pl.BlockSpec(a.shape, lambda i: (0,) * a.ndim)
    tab = pl.BlockSpec((tm, LANES), lambda i: (i % tab_tiles, 0))
    f32 = lambda w: jax.ShapeDtypeStruct((n, w), F32)
    if feature_major:
        seq = pos.shape[0]
        batch = n // seq
        t_shape = jax.ShapeDtypeStruct((batch, A_W, seq), F32)
        t_spec = pl.BlockSpec((1, A_W, tm), lambda i: (i // tab_tiles, 0, i % tab_tiles))
        attn_shape = (t_shape, t_shape, t_shape, jax.ShapeDtypeStruct((n, A_W), BF16),
                      jax.ShapeDtypeStruct((batch, tab_tiles, A_W, tm), BF16),
                      jax.ShapeDtypeStruct((n_tiles, 1, A_W), F32))
        attn_specs = (t_spec, t_spec, t_spec, row(A_W),
                      pl.BlockSpec((1, 1, A_W, tm), lambda i: (i // tab_tiles, i % tab_tiles, 0, 0)),
                      pl.BlockSpec((1, 1, A_W), lambda i: (i, 0, 0)))
    else:
        attn_shape = (f32(A_W), f32(A_W), f32(A_W))
        attn_specs = (row(A_W), row(A_W), row(A_W))
    out_shape = attn_shape + (f32(M_W), f32(M_W), f32(M_W), f32(M_W), f32(D_MODEL), f32(D_MODEL), f32(LANES))
    out_specs = attn_specs + (row(M_W), row(M_W), row(M_W), row(M_W), row(D_MODEL), row(D_MODEL), row(LANES))
    g_row = g_pre_mix.reshape(1, D_MODEL)
    return pl.pallas_call(
        functools.partial(_project_kernel, feature_major=feature_major),
        grid=(n_tiles,),
        in_specs=[row(D_MODEL), full(g_row), full(w_main), full(w_gate), full(b_if_row), tab, tab, tab],
        out_specs=out_specs,
        out_shape=out_shape,
        compiler_params=_compiler_params(("parallel",)),
        name="project",
    )(x2d, g_row, w_main, w_gate, b_if_row, rc, rs1, rs2)


def _select_topk(scores, block_idx, n_valid, axis):
    width = scores.shape[axis]
    sc = jnp.where(block_idx < n_valid, scores, NEG_INF)
    sel = jnp.zeros(scores.shape, jnp.bool_)
    for _ in range(MOBA_TOPK):
        mx = jnp.max(sc, axis=axis, keepdims=True)
        idx = jnp.min(jnp.where(sc == mx, block_idx, width), axis=axis, keepdims=True)
        hit = block_idx == idx
        sel = sel | hit
        sc = jnp.where(hit, NEG_INF, sc)
    return jnp.where(sel & (block_idx < n_valid), 1.0, 0.0)


def _moba_prompt_kernel(qt_ref, k_ref, vt_ref, kmean_ref, o_ref, sel_ref, ot_ref, *, n_blocks):
    i = pl.program_id(1)
    tq = MOBA_BLOCK
    scale = DH_A ** -0.5
    key = lax.broadcasted_iota(jnp.int32, (tq, tq), 0)
    qry = lax.broadcasted_iota(jnp.int32, (tq, tq), 1)
    causal = key <= qry
    feat = lax.broadcasted_iota(jnp.int32, (LANES, tq), 0)
    blk = lax.broadcasted_iota(jnp.int32, (n_blocks, tq), 0)
    own_start = pl.multiple_of(i * tq, tq)
    for p in range(A_W // LANES):
        feats = slice(p * LANES, (p + 1) * LANES)
        qt_pair = qt_ref[0, feats, :]
        kmean_pair = kmean_ref[0, :, feats]
        k_own = k_ref[0, pl.ds(own_start, tq), feats]
        vt_own = vt_ref[0, i, feats, :]
        outs = []
        for hh in range(HEADS_PER_LANE_GROUP):
            qt_head = jnp.where((feat // DH_A) == hh, qt_pair, 0.0)
            scores = jnp.dot(kmean_pair, qt_head, precision=lax.Precision.HIGHEST, preferred_element_type=F32)
            sel = _select_topk(scores, blk, i, axis=0)
            for j in range(n_blocks):
                sel_ref[j] = sel[j:j + 1, :]
            qtb = (qt_head * scale).astype(BF16)
            s = jnp.where(causal, jnp.dot(k_own, qtb, preferred_element_type=F32), NEG_INF)
            m0 = jnp.max(s, axis=0, keepdims=True)
            pe = jnp.exp(s - m0)
            l0 = jnp.sum(pe, axis=0, keepdims=True)
            acc0 = jnp.dot(vt_own, pe.astype(BF16), preferred_element_type=F32)

            def body(j, carry, qtb=qtb, feats=feats):
                m, l, acc = carry
                start = pl.multiple_of(j * tq, tq)
                kj = k_ref[0, pl.ds(start, tq), feats]
                vtj = vt_ref[0, j, feats, :]
                sj = jnp.dot(kj, qtb, preferred_element_type=F32)
                sj = jnp.where(sel_ref[j] > 0.5, sj, NEG_INF)
                m_new = jnp.maximum(m, jnp.max(sj, axis=0, keepdims=True))
                alpha = jnp.exp(m - m_new)
                pj = jnp.exp(sj - m_new)
                l = alpha * l + jnp.sum(pj, axis=0, keepdims=True)
                acc = alpha * acc + jnp.dot(vtj, pj.astype(BF16), preferred_element_type=F32)
                return m_new, l, acc

            _, l, acc = lax.fori_loop(0, i, body, (m0, l0, acc0))
            outs.append(acc / l)
        ot_ref[feats, :] = jnp.where(feat < DH_A, outs[0], outs[1])
    o_ref[...] = ot_ref[...].T.astype(o_ref.dtype)


def _moba_prompt(qt, k_bf, vt_bf, kmean, batch, seq):
    n_blocks = seq // MOBA_BLOCK
    tq = MOBA_BLOCK
    k3 = k_bf.reshape(batch, seq, A_W)
    km3 = kmean.reshape(batch, n_blocks, A_W)
    return pl.pallas_call(
        functools.partial(_moba_prompt_kernel, n_blocks=n_blocks),
        grid=(batch, n_blocks),
        in_specs=[pl.BlockSpec((1, A_W, tq), lambda b, i: (b, 0, i)),
                  pl.BlockSpec((1, seq, A_W), lambda b, i: (b, 0, 0)),
                  pl.BlockSpec((1, n_blocks, A_W, tq), lambda b, i: (b, 0, 0, 0)),
                  pl.BlockSpec((1, n_blocks, A_W), lambda b, i: (b, 0, 0))],
        out_specs=pl.BlockSpec((tq, A_W), lambda b, i: (b * n_blocks + i, 0)),
        out_shape=jax.ShapeDtypeStruct((batch * seq, A_W), BF16),
        scratch_shapes=[pltpu.VMEM((n_blocks, 1, tq), F32), pltpu.VMEM((A_W, tq), F32)],
        compiler_params=_compiler_params(("parallel", "arbitrary")),
        name="moba_prompt",
    )(qt, k3, vt_bf, km3)


SAMPLE_PAGES_PER_STEP = 16


def _moba_sample_kernel(pt_ref, q_ref, kn_ref, vn_ref, *refs, pages_per_step, n_blocks, page_rows, dec_seq):
    pg = pages_per_step
    k_pages = refs[:pg]
    v_pages = refs[pg:2 * pg]
    o_ref = refs[2 * pg]
    qbd_ref, qbdt_ref, s_ref, score_ref, p_ref, pown_ref, l_ref, acc_ref = refs[2 * pg + 1:]
    ph = pl.program_id(1)
    g = pl.program_id(2)
    n_steps = pl.num_programs(2)
    ppb = MOBA_BLOCK // page_rows
    blocks_per_step = pg // ppb
    n_q = H_A * dec_seq
    scale = DH_A ** -0.5
    row = lax.broadcasted_iota(jnp.int32, (n_q, A_W), 0)
    lane_w = lax.broadcasted_iota(jnp.int32, (n_q, A_W), 1)
    head_diag = (row // dec_seq) == (lane_w // DH_A)
    lane = lax.broadcasted_iota(jnp.int32, (n_q, LANES), 1)
    tok = lax.broadcasted_iota(jnp.int32, (n_q, 1), 0) % dec_seq
    score_blk = lax.broadcasted_iota(jnp.int32, (LANES, LANES), 0)

    def page_cols(t):
        return slice(t * page_rows, (t + 1) * page_rows)

    @pl.when((ph == 0) & (g == 0))
    def _():
        q_rep = jnp.concatenate([q_ref[...]] * H_A, axis=0)
        qbd = jnp.where(head_diag, q_rep, 0.0)
        qbd_ref[...] = qbd
        qbdt_ref[...] = jnp.concatenate([qbd, jnp.zeros((LANES - n_q, A_W), F32)], axis=0).T
        score_ref[...] = jnp.full(score_ref.shape, NEG_INF, F32)

    @pl.when(ph == 0)
    def _():
        qb = (qbd_ref[...] * scale).astype(BF16)
        qbdt = qbdt_ref[...]
        for bb in range(blocks_per_step):
            jb = g * blocks_per_step + bb
            kts = [k_pages[bb * ppb + t][0] for t in range(ppb)]
            kmean = jnp.sum(functools.reduce(lambda a, b: a + b, kts), axis=1, keepdims=True) * (1.0 / MOBA_BLOCK)
            sc = jnp.sum(qbdt * kmean, axis=0, keepdims=True)
            score_ref[...] = jnp.where(score_blk == jb, sc, score_ref[...])
            for t in range(ppb):
                s_ref[jb, :, page_cols(t)] = jnp.dot(qb, kts[t].astype(BF16), preferred_element_type=F32)

    @pl.when((ph == 0) & (g == n_steps - 1))
    def _():
        sel = _select_topk(score_ref[...].T[:n_q], lane, n_blocks, axis=1)
        qs = qbd_ref[...] * scale
        kn = kn_ref[...]
        own = []
        for t in range(dec_seq):
            so = jnp.sum(qs * kn[t:t + 1, :], axis=1, keepdims=True)
            own.append(jnp.where(t <= tok, so, NEG_INF))
        m = functools.reduce(jnp.maximum, own)
        for jb in range(n_blocks):
            picked = sel[:, jb:jb + 1] > 0.5
            m = jnp.maximum(m, jnp.max(jnp.where(picked, s_ref[jb], NEG_INF), axis=1, keepdims=True))
        l = jnp.zeros((n_q, 1), F32)
        for jb in range(n_blocks):
            picked = sel[:, jb:jb + 1] > 0.5
            pj = jnp.exp(jnp.where(picked, s_ref[jb] - m, NEG_INF))
            l = l + jnp.sum(pj, axis=1, keepdims=True)
            p_ref[jb] = pj.astype(BF16)
        p_own = jnp.zeros((n_q, LANES), F32)
        for t in range(dec_seq):
            pt = jnp.exp(own[t] - m)
            l = l + pt
            p_own = jnp.where(lane == t, pt, p_own)
        pown_ref[...] = p_own
        l_ref[...] = l
        acc_ref[...] = jnp.zeros(acc_ref.shape, F32)

    @pl.when(ph == 1)
    def _():
        acc = acc_ref[...]
        for bb in range(blocks_per_step):
            jb = g * blocks_per_step + bb
            for t in range(ppb):
                vt = v_pages[bb * ppb + t][0].astype(BF16)
                acc = acc + lax.dot_general(p_ref[jb, :, page_cols(t)], vt, NT_DIMS, preferred_element_type=F32)
        acc_ref[...] = acc

    @pl.when((ph == 1) & (g == n_steps - 1))
    def _():
        acc = acc_ref[...]
        vn = vn_ref[...]
        p_own = pown_ref[...]
        for t in range(dec_seq):
            acc = acc + p_own[:, t:t + 1] * vn[t:t + 1, :]
        out = jnp.where(head_diag, acc / l_ref[...], 0.0)
        o_ref[...] = functools.reduce(
            lambda a, b: a + b, [out[h * dec_seq:(h + 1) * dec_seq, :] for h in range(H_A)])


def _moba_sample(qa, ka, va, cache_k, cache_v, page_table, dec_batch, dec_seq):
    n_pool, page_rows = cache_k.shape[0], cache_k.shape[1]
    n_pages = page_table.shape[1]
    past = n_pages * page_rows
    assert past % MOBA_BLOCK == 0 and MOBA_BLOCK % page_rows == 0, "cached rows must fill whole MoBA blocks"
    n_blocks = past // MOBA_BLOCK
    assert MOBA_TOPK <= n_blocks <= LANES
    pg = SAMPLE_PAGES_PER_STEP
    assert n_pages % pg == 0 and page_rows % LANES == 0
    n_steps = n_pages // pg
    n_q = H_A * dec_seq
    assert n_q <= LANES
    ck = cache_k.transpose(0, 2, 3, 1).reshape(n_pool, A_W, page_rows)
    cv = cache_v.transpose(0, 2, 3, 1).reshape(n_pool, A_W, page_rows)
    last_b = dec_batch - 1

    def k_map(t):
        def f(b, ph, g, pt):
            bb = jnp.where(ph == 0, b, jnp.minimum(b + 1, last_b))
            return (pt[bb, jnp.where(ph == 0, g * pg + t, t)], 0, 0)
        return f

    def v_map(t):
        def f(b, ph, g, pt):
            return (pt[b, jnp.where(ph == 0, t, g * pg + t)], 0, 0)
        return f

    tok_spec = pl.BlockSpec((dec_seq, A_W), lambda b, ph, g, pt: (b, 0))
    page = lambda f: pl.BlockSpec((1, A_W, page_rows), f)
    grid_spec = pltpu.PrefetchScalarGridSpec(
        num_scalar_prefetch=1,
        grid=(dec_batch, 2, n_steps),
        in_specs=[tok_spec, tok_spec, tok_spec]
                 + [page(k_map(t)) for t in range(pg)] + [page(v_map(t)) for t in range(pg)],
        out_specs=tok_spec,
        scratch_shapes=[pltpu.VMEM((n_q, A_W), F32),
                        pltpu.VMEM((A_W, LANES), F32),
                        pltpu.VMEM((n_blocks, n_q, MOBA_BLOCK), F32),
                        pltpu.VMEM((LANES, LANES), F32),
                        pltpu.VMEM((n_blocks, n_q, MOBA_BLOCK), BF16),
                        pltpu.VMEM((n_q, LANES), F32),
                        pltpu.VMEM((n_q, 1), F32),
                        pltpu.VMEM((n_q, A_W), F32)],
    )
    return pl.pallas_call(
        functools.partial(_moba_sample_kernel, pages_per_step=pg, n_blocks=n_blocks,
                          page_rows=page_rows, dec_seq=dec_seq),
        grid_spec=grid_spec,
        out_shape=jax.ShapeDtypeStruct((dec_batch * dec_seq, A_W), F32),
        compiler_params=_compiler_params(("arbitrary", "arbitrary", "arbitrary")),
        name="moba_sample",
    )(page_table, qa, ka, va, *([ck] * pg), *([cv] * pg))


def _mlstm_kernel(q_ref, k_ref, v_ref, o_ref, gate_ref, c0_ref, n0_ref, m0_ref,
                  h_ref, c_ref, n_ref, m_ref, *, chunk):
    step = pl.program_id(1)
    lp = MLSTM_CHUNK

    @pl.when(step == 0)
    def _():
        c_ref[...] = c0_ref[...]
        n_ref[...] = n0_ref[...]
        m_ref[...] = m0_ref[...]

    def pad_rows(a):
        if chunk == lp:
            return a
        return jnp.concatenate([a, jnp.zeros((lp - chunk, a.shape[1]), a.dtype)], axis=0)

    q_all, k_all, v_all = pad_rows(q_ref[...]), pad_rows(k_ref[...]), pad_rows(v_ref[...])
    o_all, gates = pad_rows(o_ref[...]), pad_rows(gate_ref[...])
    r = lax.broadcasted_iota(jnp.int32, (lp, lp), 0)
    c = lax.broadcasted_iota(jnp.int32, (lp, lp), 1)
    real = lax.broadcasted_iota(jnp.int32, (lp, 1), 0) < chunk
    for h in range(H_M):
        lanes = slice(h * DH_M, (h + 1) * DH_M)
        q, k, v = q_all[:, lanes], k_all[:, lanes], v_all[:, lanes]
        qb, kb = q.astype(BF16), k.astype(BF16)
        i_col = gates[:, h:h + 1]
        f_col = gates[:, H_M + h:H_M + h + 1]
        f_row = jnp.sum(jnp.where(r == c, f_col, 0.0), axis=0, keepdims=True)
        i_row = jnp.sum(jnp.where(r == c, i_col, 0.0), axis=0, keepdims=True)
        b_row = jnp.sum(jnp.where(r <= c, f_col, 0.0), axis=0, keepdims=True)
        b_col = jnp.sum(jnp.where(c <= r, f_row, 0.0), axis=1, keepdims=True)
        m_prev = m_ref[0, h]
        log_d = jnp.where(c <= r, b_col - b_row + i_row, NEG_INF)
        m_t = jnp.maximum(m_prev + b_col, jnp.max(log_d, axis=1, keepdims=True))
        w_d = jnp.exp(log_d - m_t)
        w_prev = jnp.exp(m_prev + b_col - m_t)
        a = lax.dot_general(qb, kb, NT_DIMS, preferred_element_type=F32) * w_d
        c_prev = c_ref[0, h]
        n_prev = n_ref[0, h]
        q_c = lax.dot_general(qb, c_prev.astype(BF16), NT_DIMS, preferred_element_type=F32)
        num = jnp.dot(a.astype(BF16), v.astype(BF16), preferred_element_type=F32) + w_prev * q_c
        den = jnp.sum(a, axis=1, keepdims=True) + w_prev * jnp.sum(q * n_prev, axis=1, keepdims=True)
        hc = num / jnp.maximum(jnp.abs(den), jnp.exp(-m_t))
        out = jax.nn.sigmoid(o_all[:, lanes]) * hc
        h_ref[:, lanes] = out[:chunk]
        m_new = m_t[chunk - 1:chunk, :]
        b_last = b_col[chunk - 1:chunk, :]
        decay = jnp.exp(m_prev + b_last - m_new)
        w_s = jnp.where(real, jnp.exp(b_last - b_col + i_col - m_new), 0.0)
        c_ref[0, h] = decay * c_prev + lax.dot_general((w_s * v).astype(BF16), kb, TN_DIMS,
                                                       preferred_element_type=F32)
        n_ref[0, h] = decay * n_prev + jnp.sum(w_s * k, axis=0, keepdims=True)
        m_ref[0, h] = m_new


def _mlstm(qm, km, vm, om, gates, c0, n0, m0, batch, seq):
    chunk = math.gcd(seq, MLSTM_CHUNK)
    n_chunks = seq // chunk
    n0_4 = n0.reshape(batch, H_M, 1, DH_M)
    m0_4 = m0.reshape(batch, H_M, 1, 1)
    row = lambda w: pl.BlockSpec((chunk, w), lambda b, s: (b * n_chunks + s, 0))
    state = lambda a: pl.BlockSpec((1,) + a.shape[1:], lambda b, s: (b, 0, 0, 0))
    h, c, n, m = pl.pallas_call(
        functools.partial(_mlstm_kernel, chunk=chunk),
        grid=(batch, n_chunks),
        in_specs=[row(M_W), row(M_W), row(M_W), row(M_W), row(LANES), state(c0), state(n0_4), state(m0_4)],
        out_specs=(row(M_W), state(c0), state(n0_4), state(m0_4)),
        out_shape=(jax.ShapeDtypeStruct((batch * seq, M_W), F32),
                   jax.ShapeDtypeStruct(c0.shape, F32),
                   jax.ShapeDtypeStruct(n0_4.shape, F32),
                   jax.ShapeDtypeStruct(m0_4.shape, F32)),
        compiler_params=_compiler_params(("parallel", "arbitrary")),
        name="mlstm",
    )(qm, km, vm, om, gates, c0, n0_4, m0_4)
    return h, c, n.reshape(batch, H_M, DH_M), m.reshape(batch, H_M)


def _merge_kernel(x_ref, att_ref, ml_ref, ga_ref, gm_ref, wa_ref, wm_ref, wo_ref, g_ref, o_ref):
    ya = jnp.dot(att_ref[...].astype(BF16), wa_ref[...], preferred_element_type=F32)
    ym = jnp.dot(ml_ref[...].astype(BF16), wm_ref[...], preferred_element_type=F32)
    u = jax.nn.sigmoid(ga_ref[...]) * ya + jax.nn.sigmoid(gm_ref[...]) * ym
    r = jnp.dot(u.astype(BF16), wo_ref[...], preferred_element_type=F32)
    o_ref[...] = x_ref[...] + _rms(r, g_ref[...])


def _merge(x2d, att, ml, ga, gm, wa, wm, wo, g_post_mix):
    n = x2d.shape[0]
    tm = min(512, n)
    row = lambda w: pl.BlockSpec((tm, w), lambda i: (i, 0))
    full = lambda a: pl.BlockSpec(a.shape, lambda i: (0,) * a.ndim)
    g_row = g_post_mix.reshape(1, D_MODEL)
    return pl.pallas_call(
        _merge_kernel,
        grid=(n // tm,),
        in_specs=[row(D_MODEL), row(A_W), row(M_W), row(D_MODEL), row(D_MODEL),
                  full(wa), full(wm), full(wo), full(g_row)],
        out_specs=row(D_MODEL),
        out_shape=jax.ShapeDtypeStruct((n, D_MODEL), F32),
        compiler_params=_compiler_params(("parallel",)),
        name="merge",
    )(x2d, att, ml, ga, gm, wa, wm, wo, g_row)


MLP_FF_TILE = 1024


def _mlp_kernel(x_ref, g1_ref, wu_ref, wd_ref, g2_ref, o_ref, h_ref, acc_ref):
    kk = pl.program_id(1)

    @pl.when(kk == 0)
    def _():
        h_ref[...] = _rms(x_ref[...], g1_ref[...]).astype(BF16)
        acc_ref[...] = jnp.zeros(acc_ref.shape, F32)

    up = jnp.dot(h_ref[...], wu_ref[...], preferred_element_type=F32)
    act = jnp.square(jnp.maximum(up, 0.0)).astype(BF16)
    acc_ref[...] += jnp.dot(act, wd_ref[...], preferred_element_type=F32)

    @pl.when(kk == pl.num_programs(1) - 1)
    def _():
        o_ref[...] = x_ref[...] + _rms(acc_ref[...], g2_ref[...])


def _mlp(x2d, g_pre_mlp, w_up, w_down, g_post_mlp):
    n = x2d.shape[0]
    tm = min(512, n)
    tf = MLP_FF_TILE
    g1 = g_pre_mlp.reshape(1, D_MODEL)
    g2 = g_post_mlp.reshape(1, D_MODEL)
    gspec = pl.BlockSpec((1, D_MODEL), lambda i, kk: (0, 0))
    return pl.pallas_call(
        _mlp_kernel,
        grid=(n // tm, D_FF // tf),
        in_specs=[pl.BlockSpec((tm, D_MODEL), lambda i, kk: (i, 0)), gspec,
                  pl.BlockSpec((D_MODEL, tf), lambda i, kk: (0, kk)),
                  pl.BlockSpec((tf, D_MODEL), lambda i, kk: (kk, 0)), gspec],
        out_specs=pl.BlockSpec((tm, D_MODEL), lambda i, kk: (i, 0)),
        out_shape=jax.ShapeDtypeStruct((n, D_MODEL), F32),
        scratch_shapes=[pltpu.VMEM((tm, D_MODEL), BF16), pltpu.VMEM((tm, D_MODEL), F32)],
        compiler_params=_compiler_params(("parallel", "arbitrary")),
        name="mlp",
    )(x2d, g1, w_up, w_down, g2)


def kernel(x_prompt, x_sample, cache_k, cache_v, state_C, state_n, state_m, page_table, g_pre_mix, w_in, b_if,
           w_attn_br, w_mlstm_br, w_out, g_post_mix, g_pre_mlp, w_up, w_down, g_post_mlp):
    bp, sp, _ = x_prompt.shape
    db, ds, _ = x_sample.shape
    past = page_table.shape[1] * cache_k.shape[1]

    gate0 = 3 * A_W + 4 * M_W
    w_main = jnp.concatenate([w_in[:, :gate0], w_in[:, gate0 + N_GATE:]], axis=1).astype(BF16)
    w_gate = jnp.pad(w_in[:, gate0:gate0 + N_GATE], ((0, 0), (0, LANES - N_GATE))).astype(BF16)
    b_if_row = jnp.pad(b_if.astype(F32), (0, LANES - N_GATE)).reshape(1, LANES)
    wa, wm, wo = w_attn_br.astype(BF16), w_mlstm_br.astype(BF16), w_out.astype(BF16)
    wu, wd = w_up.astype(BF16), w_down.astype(BF16)

    def tail(x2d, att, ml, ga, gm):
        x1 = _merge(x2d, att, ml, ga, gm, wa, wm, wo, g_post_mix)
        return _mlp(x1, g_pre_mlp, wu, wd, g_post_mlp)

    xp = x_prompt.reshape(bp * sp, D_MODEL)
    (qt, kt, vt, k_bf, vt_bf, kmean, qm, km, vm, om, ga, gm, gates) = _project(
        xp, jnp.arange(sp, dtype=F32), g_pre_mix, w_main, w_gate, b_if_row, feature_major=True)
    att_p = _moba_prompt(qt, k_bf, vt_bf, kmean, bp, sp)
    ml_p, c_p, n_p, m_p = _mlstm(qm, km, vm, om, gates,
                                 jnp.zeros((bp, H_M, DH_M, DH_M), F32), jnp.zeros((bp, H_M, DH_M), F32),
                                 jnp.zeros((bp, H_M), F32), bp, sp)
    y_prompt = tail(xp, att_p, ml_p, ga, gm).reshape(bp, sp, D_MODEL)
    k_prompt = kt.reshape(bp, H_A, DH_A, sp).transpose(0, 3, 1, 2)
    v_prompt = vt.reshape(bp, H_A, DH_A, sp).transpose(0, 3, 1, 2)

    xs = x_sample.reshape(db * ds, D_MODEL)
    (qa, ka, va, qm, km, vm, om, ga, gm, gates) = _project(
        xs, past + jnp.arange(ds, dtype=F32), g_pre_mix, w_main, w_gate, b_if_row, feature_major=False)
    att_s = _moba_sample(qa, ka, va, cache_k, cache_v, page_table, db, ds)
    ml_s, c_s, n_s, m_s = _mlstm(qm, km, vm, om, gates, state_C.astype(F32), state_n.astype(F32),
                                 state_m.astype(F32), db, ds)
    y_sample = tail(xs, att_s, ml_s, ga, gm).reshape(db, ds, D_MODEL)
    k_sample = ka.reshape(db, ds, H_A, DH_A)
    v_sample = va.reshape(db, ds, H_A, DH_A)

    return (y_prompt, y_sample, k_prompt, v_prompt, c_p, n_p, m_p, k_sample, v_sample, c_s, n_s, m_s)
```

```python
import functools
import math

import jax
import jax.numpy as jnp
from jax import lax
from jax.experimental import pallas as pl
from jax.experimental.pallas import tpu as pltpu

F32 = jnp.float32
BF16 = jnp.bfloat16
NEG_INF = float("-inf")
LOG2_E = math.log2(math.e)

D_MODEL = 1024
H_A = 8
DH_A = 64
A_W = H_A * DH_A
MOBA_BLOCK = 256
MOBA_TOPK = 3
ROT_DIMS = DH_A // 4
ROPE_THETA = 500000.0
H_M = 4
DH_M = 128
M_W = H_M * DH_M
MLSTM_CHUNK = 128
D_FF = 4 * D_MODEL
RMS_EPS = 1e-6
N_GATE = 2 * H_M

LANES = 128
HEADS_PER_LANE_GROUP = LANES // DH_A
VMEM_LIMIT_BYTES = 52 * 1024 * 1024

NT_DIMS = (((1,), (1,)), ((), ()))
TN_DIMS = (((0,), (0,)), ((), ()))


def _compiler_params(semantics):
    return pltpu.CompilerParams(dimension_semantics=semantics, vmem_limit_bytes=VMEM_LIMIT_BYTES)


def _rms(x, g):
    return x * lax.rsqrt(jnp.mean(x * x, axis=-1, keepdims=True) + RMS_EPS) * g


def _log_sigmoid(x):
    return jnp.minimum(x, 0.0) - jnp.log1p(jnp.exp(-jnp.abs(x)))


PROJ_ROWS = MOBA_BLOCK


def _project_kernel(x_ref, g_ref, w_ref, wg_ref, bif_ref, rc_ref, rs1_ref, rs2_ref, *out_refs, feature_major):
    if feature_major:
        (qt_ref, kt_ref, vt_ref, kb_ref, vtb_ref, kmean_ref,
         qm_ref, km_ref, vm_ref, om_ref, ga_ref, gm_ref, gates_ref) = out_refs
    else:
        qa_ref, ka_ref, va_ref, qm_ref, km_ref, vm_ref, om_ref, ga_ref, gm_ref, gates_ref = out_refs
    hb = _rms(x_ref[...], g_ref[...]).astype(BF16)

    def mm(c0, n):
        return jnp.dot(hb, w_ref[:, c0:c0 + n], preferred_element_type=F32)

    rc, rs1, rs2 = rc_ref[...], rs1_ref[...], rs2_ref[...]

    def rot(z):
        half = ROT_DIMS // 2
        outs = []
        for c in range(z.shape[1] // LANES):
            zc = z[:, c * LANES:(c + 1) * LANES]
            outs.append(zc * rc + pltpu.roll(zc, LANES - half, 1) * rs1 + pltpu.roll(zc, half, 1) * rs2)
        return jnp.concatenate(outs, axis=1)

    qa = rot(mm(0, A_W))
    ka = rot(mm(A_W, A_W))
    va = mm(2 * A_W, A_W)
    if feature_major:
        qt_ref[0] = qa.T
        kt_ref[0] = ka.T
        kb_ref[...] = ka.astype(BF16)
        kmean_ref[...] = jnp.mean(ka, axis=0, keepdims=True).reshape(1, 1, A_W)
        vt = va.T
        vt_ref[0] = vt
        vtb_ref[0, 0] = vt.astype(BF16)
    else:
        qa_ref[...] = qa
        ka_ref[...] = ka
        va_ref[...] = va
    c0 = 3 * A_W
    qm_ref[...] = mm(c0, M_W)
    km_ref[...] = mm(c0 + M_W, M_W) * (DH_M ** -0.5)
    vm_ref[...] = mm(c0 + 2 * M_W, M_W)
    om_ref[...] = mm(c0 + 3 * M_W, M_W)
    c1 = c0 + 4 * M_W
    ga_ref[...] = mm(c1, D_MODEL)
    gm_ref[...] = mm(c1 + D_MODEL, D_MODEL)
    zg = jnp.dot(hb, wg_ref[...], preferred_element_type=F32) + bif_ref[...]
    lane = lax.broadcasted_iota(jnp.int32, zg.shape, 1)
    is_forget = (lane >= H_M) & (lane < N_GATE)
    gates_ref[...] = jnp.where(is_forget, _log_sigmoid(zg), zg)


def _rotary_tables(pos):
    half = ROT_DIMS // 2
    inv = ROPE_THETA ** (-jnp.arange(half, dtype=F32) * 2.0 / ROT_DIMS)
    ang = pos[:, None] * inv[None, :]
    cos, sin = jnp.cos(ang), jnp.sin(ang)
    n = pos.shape[0]
    pad = jnp.zeros((n, DH_A - ROT_DIMS), F32)
    c_head = jnp.concatenate([cos, cos, pad + 1.0], axis=1)
    s1_head = jnp.concatenate([-sin, jnp.zeros_like(sin), pad], axis=1)
    s2_head = jnp.concatenate([jnp.zeros_like(sin), sin, pad], axis=1)
    rep = lambda t: jnp.tile(t, (1, HEADS_PER_LANE_GROUP))
    return rep(c_head), rep(s1_head), rep(s2_head)


def _project(x2d, pos, g_pre_mix, w_main, w_gate, b_if_row, feature_major):
    n = x2d.shape[0]
    tm = PROJ_ROWS
    n_tiles = n // tm
    rc, rs1, rs2 = _rotary_tables(pos)
    if pos.shape[0] < tm:
        reps = tm // pos.shape[0]
        rc, rs1, rs2 = (jnp.tile(t, (reps, 1)) for t in (rc, rs1, rs2))
    tab_tiles = rc.shape[0] // tm
    row = lambda w: pl.BlockSpec((tm, w), lambda i: (i, 0))
    full = lambda a: pl.BlockSpec(a.shape, lambda i: (0,) * a.ndim)
    tab = pl.BlockSpec((tm, LANES), lambda i: (i % tab_tiles, 0))
    f32 = lambda w: jax.ShapeDtypeStruct((n, w), F32)
    if feature_major:
        seq = pos.shape[0]
        batch = n // seq
        t_shape = jax.ShapeDtypeStruct((batch, A_W, seq), F32)
        t_spec = pl.BlockSpec((1, A_W, tm), lambda i: (i // tab_tiles, 0, i % tab_tiles))
        attn_shape = (t_shape, t_shape, t_shape, jax.ShapeDtypeStruct((n, A_W), BF16),
                      jax.ShapeDtypeStruct((batch, tab_tiles, A_W, tm), BF16),
                      jax.ShapeDtypeStruct((n_tiles, 1, A_W), F32))
        attn_specs = (t_spec, t_spec, t_spec, row(A_W),
                      pl.BlockSpec((1, 1, A_W, tm), lambda i: (i // tab_tiles, i % tab_tiles, 0, 0)),
                      pl.BlockSpec((1, 1, A_W), lambda i: (i, 0, 0)))
    else:
        attn_shape = (f32(A_W), f32(A_W), f32(A_W))
        attn_specs = (row(A_W), row(A_W), row(A_W))
    out_shape = attn_shape + (f32(M_W), f32(M_W), f32(M_W), f32(M_W), f32(D_MODEL), f32(D_MODEL), f32(LANES))
    out_specs = attn_specs + (row(M_W), row(M_W), row(M_W), row(M_W), row(D_MODEL), row(D_MODEL), row(LANES))
    g_row = g_pre_mix.reshape(1, D_MODEL)
    return pl.pallas_call(
        functools.partial(_project_kernel, feature_major=feature_major),
        grid=(n_tiles,),
        in_specs=[row(D_MODEL), full(g_row), full(w_main), full(w_gate), full(b_if_row), tab, tab, tab],
        out_specs=out_specs,
        out_shape=out_shape,
        compiler_params=_compiler_params(("parallel",)),
        name="project",
    )(x2d, g_row, w_main, w_gate, b_if_row, rc, rs1, rs2)


def _select_topk(scores, block_idx, n_valid, axis):
    width = scores.shape[axis]
    sc = jnp.where(block_idx < n_valid, scores, NEG_INF)
    sel = jnp.zeros(scores.shape, jnp.bool_)
    for _ in range(MOBA_TOPK):
        mx = jnp.max(sc, axis=axis, keepdims=True)
        idx = jnp.min(jnp.where(sc == mx, block_idx, width), axis=axis, keepdims=True)
        hit = block_idx == idx
        sel = sel | hit
        sc = jnp.where(hit, NEG_INF, sc)
    return jnp.where(sel & (block_idx < n_valid), 1.0, 0.0)


def _moba_prompt_kernel(qt_ref, k_ref, vt_ref, kmean_ref, o_ref,
                        qtb_ref, sel_ref, m_ref, l_ref, acc_ref, ot_ref, *, n_blocks):
    i = pl.program_id(1)
    tq = MOBA_BLOCK
    scale = DH_A ** -0.5
    key = lax.broadcasted_iota(jnp.int32, (tq, tq), 0)
    qry = lax.broadcasted_iota(jnp.int32, (tq, tq), 1)
    causal = key <= qry
    feat = lax.broadcasted_iota(jnp.int32, (LANES, tq), 0)
    blk = lax.broadcasted_iota(jnp.int32, (n_blocks, tq), 0)
    n_pairs = A_W // LANES
    pair_feats = [slice(p * LANES, (p + 1) * LANES) for p in range(n_pairs)]

    heads = [(p, p * HEADS_PER_LANE_GROUP + hh) for p in range(n_pairs) for hh in range(HEADS_PER_LANE_GROUP)]

    def attend(k_start, vt_block, first):
        k_blk = [k_ref[0, pl.ds(k_start, tq), pair_feats[p]] for p in range(n_pairs)]
        s_all = [jnp.dot(k_blk[p], qtb_ref[h], preferred_element_type=F32) for p, h in heads]
        for p, h in heads:
            s = s_all[h]
            if first:
                s = jnp.where(causal, s, NEG_INF)
                m_new = jnp.max(s, axis=0, keepdims=True)
                m_exp = m_new
            else:
                picked = sel_ref[h * n_blocks + vt_block] > 0.5
                m_old = m_ref[h]
                m_new = jnp.maximum(m_old, jnp.where(picked, jnp.max(s, axis=0, keepdims=True), NEG_INF))
                m_exp = jnp.where(picked, m_new, jnp.inf)
                alpha = jnp.exp2(m_old - m_new)
            pe = jnp.exp2(s - m_exp)
            l_new = jnp.sum(pe, axis=0, keepdims=True)
            acc_new = jnp.dot(vt_ref[0, vt_block, pair_feats[p], :], pe.astype(BF16), preferred_element_type=F32)
            if not first:
                l_new = alpha * l_ref[h] + l_new
                acc_new = alpha * acc_ref[h] + acc_new
            m_ref[h] = m_new
            l_ref[h] = l_new
            acc_ref[h] = acc_new

    for p in range(n_pairs):
        qt_pair = qt_ref[0, pair_feats[p], :]
        kmean_pair = kmean_ref[0, :, pair_feats[p]]
        for hh in range(HEADS_PER_LANE_GROUP):
            h = p * HEADS_PER_LANE_GROUP + hh
            qt_head = jnp.where((feat // DH_A) == hh, qt_pair, 0.0)
            scores = jnp.dot(kmean_pair, qt_head, precision=lax.Precision.HIGHEST, preferred_element_type=F32)
            sel = _select_topk(scores, blk, i, axis=0)
            for j in range(n_blocks):
                sel_ref[h * n_blocks + j] = sel[j:j + 1, :]
            qtb_ref[h] = (qt_head * (scale * LOG2_E)).astype(BF16)

    attend(pl.multiple_of(i * tq, tq), i, first=True)

    def body(j, carry):
        attend(pl.multiple_of(j * tq, tq), j, first=False)
        return carry

    lax.fori_loop(0, i, body, 0)

    for p in range(n_pairs):
        h0 = p * HEADS_PER_LANE_GROUP
        ot_ref[pair_feats[p], :] = jnp.where(feat < DH_A, acc_ref[h0] / l_ref[h0], acc_ref[h0 + 1] / l_ref[h0 + 1])
    o_ref[...] = ot_ref[...].T.astype(o_ref.dtype)


def _moba_prompt(qt, k_bf, vt_bf, kmean, batch, seq):
    n_blocks = seq // MOBA_BLOCK
    tq = MOBA_BLOCK
    k3 = k_bf.reshape(batch, seq, A_W)
    km3 = kmean.reshape(batch, n_blocks, A_W)
    return pl.pallas_call(
        functools.partial(_moba_prompt_kernel, n_blocks=n_blocks),
        grid=(batch, n_blocks),
        in_specs=[pl.BlockSpec((1, A_W, tq), lambda b, i: (b, 0, i)),
                  pl.BlockSpec((1, seq, A_W), lambda b, i: (b, 0, 0)),
                  pl.BlockSpec((1, n_blocks, A_W, tq), lambda b, i: (b, 0, 0, 0)),
                  pl.BlockSpec((1, n_blocks, A_W), lambda b, i: (b, 0, 0))],
        out_specs=pl.BlockSpec((tq, A_W), lambda b, i: (b * n_blocks + i, 0)),
        out_shape=jax.ShapeDtypeStruct((batch * seq, A_W), BF16),
        scratch_shapes=[pltpu.VMEM((H_A, LANES, tq), BF16),
                        pltpu.VMEM((H_A * n_blocks, 1, tq), F32),
                        pltpu.VMEM((H_A, 1, tq), F32),
                        pltpu.VMEM((H_A, 1, tq), F32),
                        pltpu.VMEM((H_A, LANES, tq), F32),
                        pltpu.VMEM((A_W, tq), F32)],
        compiler_params=_compiler_params(("parallel", "arbitrary")),
        name="moba_prompt",
    )(qt, k3, vt_bf, km3)


SAMPLE_PAGES_PER_STEP = 16


def _moba_sample_kernel(pt_ref, q_ref, kn_ref, vn_ref, *refs, pages_per_step, n_blocks, page_rows, dec_seq):
    pg = pages_per_step
    k_pages = refs[:pg]
    v_pages = refs[pg:2 * pg]
    o_ref = refs[2 * pg]
    (qbd_ref, qbdt_ref, s_ref, score_ref, bmax_ref, sel_ref, pown_ref, m_ref, lsum_ref,
     acc_ref) = refs[2 * pg + 1:]
    ph = pl.program_id(1)
    g = pl.program_id(2)
    n_steps = pl.num_programs(2)
    ppb = MOBA_BLOCK // page_rows
    blocks_per_step = pg // ppb
    n_q = H_A * dec_seq
    scale = DH_A ** -0.5
    row = lax.broadcasted_iota(jnp.int32, (n_q, A_W), 0)
    lane_w = lax.broadcasted_iota(jnp.int32, (n_q, A_W), 1)
    head_diag = (row // dec_seq) == (lane_w // DH_A)
    lane = lax.broadcasted_iota(jnp.int32, (n_q, LANES), 1)
    tok = lax.broadcasted_iota(jnp.int32, (n_q, 1), 0) % dec_seq
    score_blk = lax.broadcasted_iota(jnp.int32, (LANES, LANES), 0)

    def page_cols(t):
        return slice(t * page_rows, (t + 1) * page_rows)

    @pl.when((ph == 0) & (g == 0))
    def _():
        q_rep = jnp.concatenate([q_ref[...]] * H_A, axis=0)
        qbd = jnp.where(head_diag, q_rep, 0.0)
        qbd_ref[...] = qbd
        qbdt_ref[...] = jnp.concatenate([qbd, jnp.zeros((LANES - n_q, A_W), F32)], axis=0).T
        score_ref[...] = jnp.full(score_ref.shape, NEG_INF, F32)
        bmax_ref[...] = jnp.full(bmax_ref.shape, NEG_INF, F32)

    @pl.when(ph == 0)
    def _():
        qb = (qbd_ref[...] * scale).astype(BF16)
        qbdt = qbdt_ref[...]
        for bb in range(blocks_per_step):
            jb = g * blocks_per_step + bb
            kts = [k_pages[bb * ppb + t][0] for t in range(ppb)]
            kmean = jnp.sum(functools.reduce(lambda a, b: a + b, kts), axis=1, keepdims=True) * (1.0 / MOBA_BLOCK)
            sc = jnp.sum(qbdt * kmean, axis=0, keepdims=True)
            score_ref[...] = jnp.where(score_blk == jb, sc, score_ref[...])
            s_pages = [jnp.dot(qb, kts[t].astype(BF16), preferred_element_type=F32) for t in range(ppb)]
            for t in range(ppb):
                s_ref[jb, :, page_cols(t)] = s_pages[t]
            s_max = jnp.max(functools.reduce(jnp.maximum, s_pages), axis=1, keepdims=True)
            bmax_ref[...] = jnp.where(lane == jb, s_max, bmax_ref[...])

    @pl.when((ph == 0) & (g == n_steps - 1))
    def _():
        sel = _select_topk(score_ref[...].T[:n_q], lane, n_blocks, axis=1)
        sel_ref[...] = sel
        qs = qbd_ref[...] * scale
        kn = kn_ref[...]
        own = []
        for t in range(dec_seq):
            so = jnp.sum(qs * kn[t:t + 1, :], axis=1, keepdims=True)
            own.append(jnp.where(t <= tok, so, NEG_INF))
        m = jnp.max(jnp.where(sel > 0.5, bmax_ref[...], NEG_INF), axis=1, keepdims=True)
        m = functools.reduce(jnp.maximum, own, m)
        p_own = jnp.zeros((n_q, LANES), F32)
        for t in range(dec_seq):
            p_own = jnp.where(lane == t, jnp.exp(own[t] - m), p_own)
        pown_ref[...] = p_own
        m_ref[...] = m
        lsum_ref[...] = jnp.zeros(lsum_ref.shape, F32)
        acc_ref[...] = jnp.zeros(acc_ref.shape, F32)

    @pl.when(ph == 1)
    def _():
        acc = acc_ref[...]
        lsum = lsum_ref[...]
        m = m_ref[...]
        sel = sel_ref[...]
        for bb in range(blocks_per_step):
            jb = g * blocks_per_step + bb
            picked = jnp.sum(jnp.where(lane == jb, sel, 0.0), axis=1, keepdims=True) > 0.5
            pj = jnp.exp(jnp.where(picked, s_ref[jb] - m, NEG_INF))
            lsum = lsum + pj
            for t in range(ppb):
                vt = v_pages[bb * ppb + t][0].astype(BF16)
                acc = acc + lax.dot_general(pj[:, page_cols(t)].astype(BF16), vt, NT_DIMS,
                                            preferred_element_type=F32)
        acc_ref[...] = acc
        lsum_ref[...] = lsum

    @pl.when((ph == 1) & (g == n_steps - 1))
    def _():
        acc = acc_ref[...]
        vn = vn_ref[...]
        p_own = pown_ref[...]
        for t in range(dec_seq):
            acc = acc + p_own[:, t:t + 1] * vn[t:t + 1, :]
        l = jnp.sum(lsum_ref[...], axis=1, keepdims=True) + jnp.sum(p_own, axis=1, keepdims=True)
        out = jnp.where(head_diag, acc / l, 0.0)
        o_ref[...] = functools.reduce(
            lambda a, b: a + b, [out[h * dec_seq:(h + 1) * dec_seq, :] for h in range(H_A)])


def _moba_sample(qa, ka, va, cache_k, cache_v, page_table, dec_batch, dec_seq):
    n_pool, page_rows = cache_k.shape[0], cache_k.shape[1]
    n_pages = page_table.shape[1]
    past = n_pages * page_rows
    assert past % MOBA_BLOCK == 0 and MOBA_BLOCK % page_rows == 0, "cached rows must fill whole MoBA blocks"
    n_blocks = past // MOBA_BLOCK
    assert MOBA_TOPK <= n_blocks <= LANES
    pg = SAMPLE_PAGES_PER_STEP
    assert n_pages % pg == 0 and page_rows % LANES == 0
    n_steps = n_pages // pg
    n_q = H_A * dec_seq
    assert n_q <= LANES
    ck = cache_k.transpose(0, 2, 3, 1).reshape(n_pool, A_W, page_rows)
    cv = cache_v.transpose(0, 2, 3, 1).reshape(n_pool, A_W, page_rows)
    last_b = dec_batch - 1

    def k_map(t):
        def f(b, ph, g, pt):
            bb = jnp.where(ph == 0, b, jnp.minimum(b + 1, last_b))
            return (pt[bb, jnp.where(ph == 0, g * pg + t, t)], 0, 0)
        return f

    def v_map(t):
        def f(b, ph, g, pt):
            return (pt[b, jnp.where(ph == 0, t, g * pg + t)], 0, 0)
        return f

    tok_spec = pl.BlockSpec((dec_seq, A_W), lambda b, ph, g, pt: (b, 0))
    page = lambda f: pl.BlockSpec((1, A_W, page_rows), f)
    grid_spec = pltpu.PrefetchScalarGridSpec(
        num_scalar_prefetch=1,
        grid=(dec_batch, 2, n_steps),
        in_specs=[tok_spec, tok_spec, tok_spec]
                 + [page(k_map(t)) for t in range(pg)] + [page(v_map(t)) for t in range(pg)],
        out_specs=tok_spec,
        scratch_shapes=[pltpu.VMEM((n_q, A_W), F32),
                        pltpu.VMEM((A_W, LANES), F32),
                        pltpu.VMEM((n_blocks, n_q, MOBA_BLOCK), F32),
                        pltpu.VMEM((LANES, LANES), F32),
                        pltpu.VMEM((n_q, LANES), F32),
                        pltpu.VMEM((n_q, LANES), F32),
                        pltpu.VMEM((n_q, LANES), F32),
                        pltpu.VMEM((n_q, 1), F32),
                        pltpu.VMEM((n_q, MOBA_BLOCK), F32),
                        pltpu.VMEM((n_q, A_W), F32)],
    )
    return pl.pallas_call(
        functools.partial(_moba_sample_kernel, pages_per_step=pg, n_blocks=n_blocks,
                          page_rows=page_rows, dec_seq=dec_seq),
        grid_spec=grid_spec,
        out_shape=jax.ShapeDtypeStruct((dec_batch * dec_seq, A_W), F32),
        compiler_params=_compiler_params(("arbitrary", "arbitrary", "arbitrary")),
        name="moba_sample",
    )(page_table, qa, ka, va, *([ck] * pg), *([cv] * pg))


def _mlstm_kernel(q_ref, k_ref, v_ref, o_ref, gate_ref, c0_ref, n0_ref, m0_ref,
                  h_ref, c_ref, n_ref, m_ref, *, chunk):
    step = pl.program_id(1)
    lp = MLSTM_CHUNK

    @pl.when(step == 0)
    def _():
        c_ref[...] = c0_ref[...]
        n_ref[...] = n0_ref[...]
        m_ref[...] = m0_ref[...]

    def pad_rows(a):
        if chunk == lp:
            return a
        return jnp.concatenate([a, jnp.zeros((lp - chunk, a.shape[1]), a.dtype)], axis=0)

    q_all, k_all, v_all = pad_rows(q_ref[...]), pad_rows(k_ref[...]), pad_rows(v_ref[...])
    o_all, gates = pad_rows(o_ref[...]), pad_rows(gate_ref[...])
    r = lax.broadcasted_iota(jnp.int32, (lp, lp), 0)
    c = lax.broadcasted_iota(jnp.int32, (lp, lp), 1)
    real = lax.broadcasted_iota(jnp.int32, (lp, 1), 0) < chunk
    for h in range(H_M):
        lanes = slice(h * DH_M, (h + 1) * DH_M)
        q, k, v = q_all[:, lanes], k_all[:, lanes], v_all[:, lanes]
        qb, kb = q.astype(BF16), k.astype(BF16)
        i_col = gates[:, h:h + 1]
        f_col = gates[:, H_M + h:H_M + h + 1]
        f_row = jnp.sum(jnp.where(r == c, f_col, 0.0), axis=0, keepdims=True)
        i_row = jnp.sum(jnp.where(r == c, i_col, 0.0), axis=0, keepdims=True)
        b_row = jnp.sum(jnp.where(r <= c, f_col, 0.0), axis=0, keepdims=True)
        b_col = jnp.sum(jnp.where(c <= r, f_row, 0.0), axis=1, keepdims=True)
        m_prev = m_ref[0, h]
        log_d = jnp.where(c <= r, b_col - b_row + i_row, NEG_INF)
        m_t = jnp.maximum(m_prev + b_col, jnp.max(log_d, axis=1, keepdims=True))
        w_d = jnp.exp(log_d - m_t)
        w_prev = jnp.exp(m_prev + b_col - m_t)
        a = lax.dot_general(qb, kb, NT_DIMS, preferred_element_type=F32) * w_d
        c_prev = c_ref[0, h]
        n_prev = n_ref[0, h]
        q_c = lax.dot_general(qb, c_prev.astype(BF16), NT_DIMS, preferred_element_type=F32)
        num = jnp.dot(a.astype(BF16), v.astype(BF16), preferred_element_type=F32) + w_prev * q_c
        den = jnp.sum(a, axis=1, keepdims=True) + w_prev * jnp.sum(q * n_prev, axis=1, keepdims=True)
        hc = num / jnp.maximum(jnp.abs(den), jnp.exp(-m_t))
        out = jax.nn.sigmoid(o_all[:, lanes]) * hc
        h_ref[:, lanes] = out[:chunk]
        m_new = m_t[chunk - 1:chunk, :]
        b_last = b_col[chunk - 1:chunk, :]
        decay = jnp.exp(m_prev + b_last - m_new)
        w_s = jnp.where(real, jnp.exp(b_last - b_col + i_col - m_new), 0.0)
        c_ref[0, h] = decay * c_prev + lax.dot_general((w_s * v).astype(BF16), kb, TN_DIMS,
                                                       preferred_element_type=F32)
        n_ref[0, h] = decay * n_prev + jnp.sum(w_s * k, axis=0, keepdims=True)
        m_ref[0, h] = m_new


def _mlstm(qm, km, vm, om, gates, c0, n0, m0, batch, seq):
    chunk = math.gcd(seq, MLSTM_CHUNK)
    n_chunks = seq // chunk
    n0_4 = n0.reshape(batch, H_M, 1, DH_M)
    m0_4 = m0.reshape(batch, H_M, 1, 1)
    row = lambda w: pl.BlockSpec((chunk, w), lambda b, s: (b * n_chunks + s, 0))
    state = lambda a: pl.BlockSpec((1,) + a.shape[1:], lambda b, s: (b, 0, 0, 0))
    h, c, n, m = pl.pallas_call(
        functools.partial(_mlstm_kernel, chunk=chunk),
        grid=(batch, n_chunks),
        in_specs=[row(M_W), row(M_W), row(M_W), row(M_W), row(LANES), state(c0), state(n0_4), state(m0_4)],
        out_specs=(row(M_W), state(c0), state(n0_4), state(m0_4)),
        out_shape=(jax.ShapeDtypeStruct((batch * seq, M_W), F32),
                   jax.ShapeDtypeStruct(c0.shape, F32),
                   jax.ShapeDtypeStruct(n0_4.shape, F32),
                   jax.ShapeDtypeStruct(m0_4.shape, F32)),
        compiler_params=_compiler_params(("parallel", "arbitrary")),
        name="mlstm",
    )(qm, km, vm, om, gates, c0, n0_4, m0_4)
    return h, c, n.reshape(batch, H_M, DH_M), m.reshape(batch, H_M)


def _merge_kernel(x_ref, att_ref, ml_ref, ga_ref, gm_ref, wa_ref, wm_ref, wo_ref, g_ref, o_ref):
    ya = jnp.dot(att_ref[...].astype(BF16), wa_ref[...], preferred_element_type=F32)
    ym = jnp.dot(ml_ref[...].astype(BF16), wm_ref[...], preferred_element_type=F32)
    u = jax.nn.sigmoid(ga_ref[...]) * ya + jax.nn.sigmoid(gm_ref[...]) * ym
    r = jnp.dot(u.astype(BF16), wo_ref[...], preferred_element_type=F32)
    o_ref[...] = x_ref[...] + _rms(r, g_ref[...])


def _merge(x2d, att, ml, ga, gm, wa, wm, wo, g_post_mix):
    n = x2d.shape[0]
    tm = min(512, n)
    row = lambda w: pl.BlockSpec((tm, w), lambda i: (i, 0))
    full = lambda a: pl.BlockSpec(a.shape, lambda i: (0,) * a.ndim)
    g_row = g_post_mix.reshape(1, D_MODEL)
    return pl.pallas_call(
        _merge_kernel,
        grid=(n // tm,),
        in_specs=[row(D_MODEL), row(A_W), row(M_W), row(D_MODEL), row(D_MODEL),
                  full(wa), full(wm), full(wo), full(g_row)],
        out_specs=row(D_MODEL),
        out_shape=jax.ShapeDtypeStruct((n, D_MODEL), F32),
        compiler_params=_compiler_params(("parallel",)),
        name="merge",
    )(x2d, att, ml, ga, gm, wa, wm, wo, g_row)


MLP_FF_TILE = 1024


def _mlp_kernel(x_ref, g1_ref, wu_ref, wd_ref, g2_ref, o_ref, h_ref, acc_ref):
    kk = pl.program_id(1)

    @pl.when(kk == 0)
    def _():
        h_ref[...] = _rms(x_ref[...], g1_ref[...]).astype(BF16)
        acc_ref[...] = jnp.zeros(acc_ref.shape, F32)

    up = jnp.dot(h_ref[...], wu_ref[...], preferred_element_type=F32)
    act = jnp.square(jnp.maximum(up, 0.0)).astype(BF16)
    acc_ref[...] += jnp.dot(act, wd_ref[...], preferred_element_type=F32)

    @pl.when(kk == pl.num_programs(1) - 1)
    def _():
        o_ref[...] = x_ref[...] + _rms(acc_ref[...], g2_ref[...])


def _mlp(x2d, g_pre_mlp, w_up, w_down, g_post_mlp):
    n = x2d.shape[0]
    tm = min(512, n)
    tf = MLP_FF_TILE
    g1 = g_pre_mlp.reshape(1, D_MODEL)
    g2 = g_post_mlp.reshape(1, D_MODEL)
    gspec = pl.BlockSpec((1, D_MODEL), lambda i, kk: (0, 0))
    return pl.pallas_call(
        _mlp_kernel,
        grid=(n // tm, D_FF // tf),
        in_specs=[pl.BlockSpec((tm, D_MODEL), lambda i, kk: (i, 0)), gspec,
                  pl.BlockSpec((D_MODEL, tf), lambda i, kk: (0, kk)),
                  pl.BlockSpec((tf, D_MODEL), lambda i, kk: (kk, 0)), gspec],
        out_specs=pl.BlockSpec((tm, D_MODEL), lambda i, kk: (i, 0)),
        out_shape=jax.ShapeDtypeStruct((n, D_MODEL), F32),
        scratch_shapes=[pltpu.VMEM((tm, D_MODEL), BF16), pltpu.VMEM((tm, D_MODEL), F32)],
        compiler_params=_compiler_params(("parallel", "arbitrary")),
        name="mlp",
    )(x2d, g1, w_up, w_down, g2)


def kernel(x_prompt, x_sample, cache_k, cache_v, state_C, state_n, state_m, page_table, g_pre_mix, w_in, b_if,
           w_attn_br, w_mlstm_br, w_out, g_post_mix, g_pre_mlp, w_up, w_down, g_post_mlp):
    bp, sp, _ = x_prompt.shape
    db, ds, _ = x_sample.shape
    past = page_table.shape[1] * cache_k.shape[1]

    gate0 = 3 * A_W + 4 * M_W
    w_main = jnp.concatenate([w_in[:, :gate0], w_in[:, gate0 + N_GATE:]], axis=1).astype(BF16)
    w_gate = jnp.pad(w_in[:, gate0:gate0 + N_GATE], ((0, 0), (0, LANES - N_GATE))).astype(BF16)
    b_if_row = jnp.pad(b_if.astype(F32), (0, LANES - N_GATE)).reshape(1, LANES)
    wa, wm, wo = w_attn_br.astype(BF16), w_mlstm_br.astype(BF16), w_out.astype(BF16)
    wu, wd = w_up.astype(BF16), w_down.astype(BF16)

    def tail(x2d, att, ml, ga, gm):
        x1 = _merge(x2d, att, ml, ga, gm, wa, wm, wo, g_post_mix)
        return _mlp(x1, g_pre_mlp, wu, wd, g_post_mlp)

    xp = x_prompt.reshape(bp * sp, D_MODEL)
    (qt, kt, vt, k_bf, vt_bf, kmean, qm, km, vm, om, ga, gm, gates) = _project(
        xp, jnp.arange(sp, dtype=F32), g_pre_mix, w_main, w_gate, b_if_row, feature_major=True)
    att_p = _moba_prompt(qt, k_bf, vt_bf, kmean, bp, sp)
    ml_p, c_p, n_p, m_p = _mlstm(qm, km, vm, om, gates,
                                 jnp.zeros((bp, H_M, DH_M, DH_M), F32), jnp.zeros((bp, H_M, DH_M), F32),
                                 jnp.zeros((bp, H_M), F32), bp, sp)
    y_prompt = tail(xp, att_p, ml_p, ga, gm).reshape(bp, sp, D_MODEL)
    k_prompt = kt.reshape(bp, H_A, DH_A, sp).transpose(0, 3, 1, 2)
    v_prompt = vt.reshape(bp, H_A, DH_A, sp).transpose(0, 3, 1, 2)

    xs = x_sample.reshape(db * ds, D_MODEL)
    (qa, ka, va, qm, km, vm, om, ga, gm, gates) = _project(
        xs, past + jnp.arange(ds, dtype=F32), g_pre_mix, w_main, w_gate, b_if_row, feature_major=False)
    att_s = _moba_sample(qa, ka, va, cache_k, cache_v, page_table, db, ds)
    ml_s, c_s, n_s, m_s = _mlstm(qm, km, vm, om, gates, state_C.astype(F32), state_n.astype(F32),
                                 state_m.astype(F32), db, ds)
    y_sample = tail(xs, att_s, ml_s, ga, gm).reshape(db, ds, D_MODEL)
    k_sample = ka.reshape(db, ds, H_A, DH_A)
    v_sample = va.reshape(db, ds, H_A, DH_A)

    return (y_prompt, y_sample, k_prompt, v_prompt, c_p, n_p, m_p, k_sample, v_sample, c_s, n_s, m_s)
```

```python
import functools
import math

import jax
import jax.numpy as jnp
from jax import lax
from jax.experimental import pallas as pl
from jax.experimental.pallas import tpu as pltpu

F32 = jnp.float32
BF16 = jnp.bfloat16
NEG_INF = float("-inf")
LOG2_E = math.log2(math.e)

D_MODEL = 1024
H_A = 8
DH_A = 64
A_W = H_A * DH_A
MOBA_BLOCK = 256
MOBA_TOPK = 3
ROT_DIMS = DH_A // 4
ROPE_THETA = 500000.0
H_M = 4
DH_M = 128
M_W = H_M * DH_M
MLSTM_CHUNK = 128
D_FF = 4 * D_MODEL
RMS_EPS = 1e-6
N_GATE = 2 * H_M

LANES = 128
HEADS_PER_LANE_GROUP = LANES // DH_A
VMEM_LIMIT_BYTES = 52 * 1024 * 1024

NT_DIMS = (((1,), (1,)), ((), ()))
TN_DIMS = (((0,), (0,)), ((), ()))


def _compiler_params(semantics):
    return pltpu.CompilerParams(dimension_semantics=semantics, vmem_limit_bytes=VMEM_LIMIT_BYTES)


def _rms(x, g):
    return x * lax.rsqrt(jnp.mean(x * x, axis=-1, keepdims=True) + RMS_EPS) * g


def _log_sigmoid(x):
    return jnp.minimum(x, 0.0) - jnp.log1p(jnp.exp(-jnp.abs(x)))


PROJ_ROWS = MOBA_BLOCK


def _project_kernel(x_ref, g_ref, w_ref, wg_ref, bif_ref, rc_ref, rs1_ref, rs2_ref, *out_refs, feature_major):
    if feature_major:
        (qt_ref, kt_ref, vt_ref, kb_ref, vtb_ref, kmean_ref,
         qm_ref, km_ref, vm_ref, om_ref, ga_ref, gm_ref, gates_ref) = out_refs
    else:
        qa_ref, ka_ref, va_ref, qm_ref, km_ref, vm_ref, om_ref, ga_ref, gm_ref, gates_ref = out_refs
    hb = _rms(x_ref[...], g_ref[...]).astype(BF16)

    def mm(c0, n):
        return jnp.dot(hb, w_ref[:, c0:c0 + n], preferred_element_type=F32)

    rc, rs1, rs2 = rc_ref[...], rs1_ref[...], rs2_ref[...]

    def rot(z):
        half = ROT_DIMS // 2
        outs = []
        for c in range(z.shape[1] // LANES):
            zc = z[:, c * LANES:(c + 1) * LANES]
            outs.append(zc * rc + pltpu.roll(zc, LANES - half, 1) * rs1 + pltpu.roll(zc, half, 1) * rs2)
        return jnp.concatenate(outs, axis=1)

    qa = rot(mm(0, A_W))
    ka = rot(mm(A_W, A_W))
    va = mm(2 * A_W, A_W)
    if feature_major:
        qt_ref[0] = qa.T
        kt_ref[0] = ka.T
        kb_ref[...] = ka.astype(BF16)
        kmean_ref[...] = jnp.mean(ka, axis=0, keepdims=True).reshape(1, 1, A_W)
        vt = va.T
        vt_ref[0] = vt
        vtb_ref[0, 0] = vt.astype(BF16)
    else:
        qa_ref[...] = qa
        ka_ref[...] = ka
        va_ref[...] = va
    c0 = 3 * A_W
    qm_ref[...] = mm(c0, M_W)
    km_ref[...] = mm(c0 + M_W, M_W) * (DH_M ** -0.5)
    vm_ref[...] = mm(c0 + 2 * M_W, M_W)
    om_ref[...] = mm(c0 + 3 * M_W, M_W)
    c1 = c0 + 4 * M_W
    ga_ref[...] = mm(c1, D_MODEL)
    gm_ref[...] = mm(c1 + D_MODEL, D_MODEL)
    zg = jnp.dot(hb, wg_ref[...], preferred_element_type=F32) + bif_ref[...]
    lane = lax.broadcasted_iota(jnp.int32, zg.shape, 1)
    is_forget = (lane >= H_M) & (lane < N_GATE)
    gates_ref[...] = jnp.where(is_forget, _log_sigmoid(zg), zg)


def _rotary_tables(pos):
    half = ROT_DIMS // 2
    inv = ROPE_THETA ** (-jnp.arange(half, dtype=F32) * 2.0 / ROT_DIMS)
    ang = pos[:, None] * inv[None, :]
    cos, sin = jnp.cos(ang), jnp.sin(ang)
    n = pos.shape[0]
    pad = jnp.zeros((n, DH_A - ROT_DIMS), F32)
    c_head = jnp.concatenate([cos, cos, pad + 1.0], axis=1)
    s1_head = jnp.concatenate([-sin, jnp.zeros_like(sin), pad], axis=1)
    s2_head = jnp.concatenate([jnp.zeros_like(sin), sin, pad], axis=1)
    rep = lambda t: jnp.tile(t, (1, HEADS_PER_LANE_GROUP))
    return rep(c_head), rep(s1_head), rep(s2_head)


def _project(x2d, pos, g_pre_mix, w_main, w_gate, b_if_row, feature_major):
    n = x2d.shape[0]
    tm = PROJ_ROWS
    n_tiles = n // tm
    rc, rs1, rs2 = _rotary_tables(pos)
    if pos.shape[0] < tm:
        reps = tm // pos.shape[0]
        rc, rs1, rs2 = (jnp.tile(t, (reps, 1)) for t in (rc, rs1, rs2))
    tab_tiles = rc.shape[0] // tm
    row = lambda w: pl.BlockSpec((tm, w), lambda i: (i, 0))
    full = lambda a: pl.BlockSpec(a.shape, lambda i: (0,) * a.ndim)
    tab = pl.BlockSpec((tm, LANES), lambda i: (i % tab_tiles, 0))
    f32 = lambda w: jax.ShapeDtypeStruct((n, w), F32)
    if feature_major:
        seq = pos.shape[0]
        batch = n // seq
        t_shape = jax.ShapeDtypeStruct((batch, A_W, seq), F32)
        t_spec = pl.BlockSpec((1, A_W, tm), lambda i: (i // tab_tiles, 0, i % tab_tiles))
        attn_shape = (t_shape, t_shape, t_shape, jax.ShapeDtypeStruct((n, A_W), BF16),
                      jax.ShapeDtypeStruct((batch, tab_tiles, A_W, tm), BF16),
                      jax.ShapeDtypeStruct((n_tiles, 1, A_W), F32))
        attn_specs = (t_spec, t_spec, t_spec, row(A_W),
                      pl.BlockSpec((1, 1, A_W, tm), lambda i: (i // tab_tiles, i % tab_tiles, 0, 0)),
                      pl.BlockSpec((1, 1, A_W), lambda i: (i, 0, 0)))
    else:
        attn_shape = (f32(A_W), f32(A_W), f32(A_W))
        attn_specs = (row(A_W), row(A_W), row(A_W))
    out_shape = attn_shape + (f32(M_W), f32(M_W), f32(M_W), f32(M_W), f32(D_MODEL), f32(D_MODEL), f32(LANES))
    out_specs = attn_specs + (row(M_W), row(M_W), row(M_W), row(M_W), row(D_MODEL), row(D_MODEL), row(LANES))
    g_row = g_pre_mix.reshape(1, D_MODEL)
    return pl.pallas_call(
        functools.partial(_project_kernel, feature_major=feature_major),
        grid=(n_tiles,),
        in_specs=[row(D_MODEL), full(g_row), full(w_main), full(w_gate), full(b_if_row), tab, tab, tab],
        out_specs=out_specs,
        out_shape=out_shape,
        compiler_params=_compiler_params(("parallel",)),
        name="project",
    )(x2d, g_row, w_main, w_gate, b_if_row, rc, rs1, rs2)


def _select_topk(scores, block_idx, n_valid, axis):
    width = scores.shape[axis]
    sc = jnp.where(block_idx < n_valid, scores, NEG_INF)
    sel = jnp.zeros(scores.shape, jnp.bool_)
    for _ in range(MOBA_TOPK):
        mx = jnp.max(sc, axis=axis, keepdims=True)
        idx = jnp.min(jnp.where(sc == mx, block_idx, width), axis=axis, keepdims=True)
        hit = block_idx == idx
        sel = sel | hit
        sc = jnp.where(hit, NEG_INF, sc)
    return jnp.where(sel & (block_idx < n_valid), 1.0, 0.0)


def _moba_prompt_kernel(qt_ref, k_ref, vt_ref, kmean_ref, o_ref,
                        qtb_ref, sel_ref, m_ref, l_ref, acc_ref, ot_ref, *, n_blocks):
    i = pl.program_id(1)
    tq = MOBA_BLOCK
    scale = DH_A ** -0.5
    key = lax.broadcasted_iota(jnp.int32, (tq, tq), 0)
    qry = lax.broadcasted_iota(jnp.int32, (tq, tq), 1)
    causal = key <= qry
    feat = lax.broadcasted_iota(jnp.int32, (LANES, tq), 0)
    blk = lax.broadcasted_iota(jnp.int32, (n_blocks, tq), 0)
    n_pairs = A_W // LANES
    pair_feats = [slice(p * LANES, (p + 1) * LANES) for p in range(n_pairs)]

    heads = [(p, p * HEADS_PER_LANE_GROUP + hh) for p in range(n_pairs) for hh in range(HEADS_PER_LANE_GROUP)]

    def attend(k_start, vt_block, first):
        k_blk = [k_ref[0, pl.ds(k_start, tq), pair_feats[p]] for p in range(n_pairs)]
        s_all = [jnp.dot(k_blk[p], qtb_ref[h], preferred_element_type=F32) for p, h in heads]
        for p, h in heads:
            s = s_all[h]
            if first:
                s = jnp.where(causal, s, NEG_INF)
                m_new = jnp.max(s, axis=0, keepdims=True)
                m_exp = m_new
            else:
                picked = sel_ref[h * n_blocks + vt_block] > 0.5
                m_old = m_ref[h]
                m_new = jnp.maximum(m_old, jnp.where(picked, jnp.max(s, axis=0, keepdims=True), NEG_INF))
                m_exp = jnp.where(picked, m_new, jnp.inf)
                alpha = jnp.exp2(m_old - m_new)
            pe = jnp.exp2(s - m_exp)
            l_new = jnp.sum(pe, axis=0, keepdims=True)
            acc_new = jnp.dot(vt_ref[0, vt_block, pair_feats[p], :], pe.astype(BF16), preferred_element_type=F32)
            if not first:
                l_new = alpha * l_ref[h] + l_new
                acc_new = alpha * acc_ref[h] + acc_new
            m_ref[h] = m_new
            l_ref[h] = l_new
            acc_ref[h] = acc_new

    for p in range(n_pairs):
        qt_pair = qt_ref[0, pair_feats[p], :]
        kmean_pair = kmean_ref[0, :, pair_feats[p]]
        for hh in range(HEADS_PER_LANE_GROUP):
            h = p * HEADS_PER_LANE_GROUP + hh
            qt_head = jnp.where((feat // DH_A) == hh, qt_pair, 0.0)
            scores = jnp.dot(kmean_pair, qt_head, precision=lax.Precision.HIGHEST, preferred_element_type=F32)
            sel = _select_topk(scores, blk, i, axis=0)
            for j in range(n_blocks):
                sel_ref[h * n_blocks + j] = sel[j:j + 1, :]
            qtb_ref[h] = (qt_head * (scale * LOG2_E)).astype(BF16)

    attend(pl.multiple_of(i * tq, tq), i, first=True)

    def body(j, carry):
        attend(pl.multiple_of(j * tq, tq), j, first=False)
        return carry

    lax.fori_loop(0, i, body, 0)

    for p in range(n_pairs):
        h0 = p * HEADS_PER_LANE_GROUP
        ot_ref[pair_feats[p], :] = jnp.where(feat < DH_A, acc_ref[h0] / l_ref[h0], acc_ref[h0 + 1] / l_ref[h0 + 1])
    o_ref[...] = ot_ref[...].T.astype(o_ref.dtype)


def _moba_prompt(qt, k_bf, vt_bf, kmean, batch, seq):
    n_blocks = seq // MOBA_BLOCK
    tq = MOBA_BLOCK
    k3 = k_bf.reshape(batch, seq, A_W)
    km3 = kmean.reshape(batch, n_blocks, A_W)
    return pl.pallas_call(
        functools.partial(_moba_prompt_kernel, n_blocks=n_blocks),
        grid=(batch, n_blocks),
        in_specs=[pl.BlockSpec((1, A_W, tq), lambda b, i: (b, 0, i)),
                  pl.BlockSpec((1, seq, A_W), lambda b, i: (b, 0, 0)),
                  pl.BlockSpec((1, n_blocks, A_W, tq), lambda b, i: (b, 0, 0, 0)),
                  pl.BlockSpec((1, n_blocks, A_W), lambda b, i: (b, 0, 0))],
        out_specs=pl.BlockSpec((tq, A_W), lambda b, i: (b * n_blocks + i, 0)),
        out_shape=jax.ShapeDtypeStruct((batch * seq, A_W), BF16),
        scratch_shapes=[pltpu.VMEM((H_A, LANES, tq), BF16),
                        pltpu.VMEM((H_A * n_blocks, 1, tq), F32),
                        pltpu.VMEM((H_A, 1, tq), F32),
                        pltpu.VMEM((H_A, 1, tq), F32),
                        pltpu.VMEM((H_A, LANES, tq), F32),
                        pltpu.VMEM((A_W, tq), F32)],
        compiler_params=_compiler_params(("parallel", "arbitrary")),
        name="moba_prompt",
    )(qt, k3, vt_bf, km3)


SAMPLE_RING_PAGES = 32
SAMPLE_BLOCKS_PER_ITER = 4


def _moba_sample_kernel(pt_ref, q_ref, kn_ref, vn_ref, ck_ref, cv_ref, o_ref,
                        ring_ref, ring_sem, qbd_ref, qbdt_ref, s_ref, score_ref, bmax_ref, acc_ref,
                        *, n_pages, n_blocks, page_rows, dec_seq):
    b = pl.program_id(0)
    n_seq = pl.num_programs(0)
    n_ring = SAMPLE_RING_PAGES
    stream_len = 2 * n_pages
    ppb = MOBA_BLOCK // page_rows
    n_q = H_A * dec_seq

    def ring_copy(pool_ref, page, slot):
        return pltpu.make_async_copy(pool_ref.at[page], ring_ref.at[slot], ring_sem.at[slot])

    def start_fetch(seq, pos, keys_only=False):
        slot = pos % n_ring
        if keys_only:
            ring_copy(ck_ref, pt_ref[seq, pos], slot).start()
            return

        @pl.when(pos < n_pages)
        def _():
            ring_copy(ck_ref, pt_ref[seq, pos], slot).start()

        @pl.when(pos >= n_pages)
        def _():
            ring_copy(cv_ref, pt_ref[seq, pos - n_pages], slot).start()

    def take(pos):
        slot = pos % n_ring
        ring_copy(ck_ref, 0, slot).wait()
        return slot

    def refill(pos):
        nxt = pos + n_ring

        @pl.when(nxt < stream_len)
        def _():
            start_fetch(b, nxt)

        @pl.when((nxt >= stream_len) & (b + 1 < n_seq))
        def _():
            start_fetch(b + 1, nxt - stream_len, keys_only=True)

    @pl.when(b == 0)
    def _():
        for pos in range(n_ring):
            start_fetch(0, pos, keys_only=True)

    scale = DH_A ** -0.5
    row = lax.broadcasted_iota(jnp.int32, (n_q, A_W), 0)
    lane_w = lax.broadcasted_iota(jnp.int32, (n_q, A_W), 1)
    head_diag = (row // dec_seq) == (lane_w // DH_A)
    lane = lax.broadcasted_iota(jnp.int32, (n_q, LANES), 1)
    tok = lax.broadcasted_iota(jnp.int32, (n_q, 1), 0) % dec_seq
    score_blk = lax.broadcasted_iota(jnp.int32, (LANES, LANES), 0)

    def page_cols(t):
        return slice(t * page_rows, (t + 1) * page_rows)

    q_rep = jnp.concatenate([q_ref[...]] * H_A, axis=0)
    qbd = jnp.where(head_diag, q_rep, 0.0)
    qbd_ref[...] = qbd
    qbdt_ref[...] = jnp.concatenate([qbd, jnp.zeros((LANES - n_q, A_W), F32)], axis=0).T
    score_ref[...] = jnp.full(score_ref.shape, NEG_INF, F32)
    bmax_ref[...] = jnp.full(bmax_ref.shape, NEG_INF, F32)

    unroll = SAMPLE_BLOCKS_PER_ITER
    pages_per_iter = unroll * ppb

    def k_blocks(it, carry):
        qb = (qbd_ref[...] * scale).astype(BF16)
        qbdt = qbdt_ref[...]
        pos0 = it * pages_per_iter
        slots = [take(pos0 + t) for t in range(pages_per_iter)]
        score, bmax = score_ref[...], bmax_ref[...]
        for u in range(unroll):
            jb = it * unroll + u
            kts = [ring_ref[slots[u * ppb + t]] for t in range(ppb)]
            kmean = jnp.sum(functools.reduce(lambda x, y: x + y, kts), axis=1, keepdims=True) * (1.0 / MOBA_BLOCK)
            sc = jnp.sum(qbdt * kmean, axis=0, keepdims=True)
            score = jnp.where(score_blk == jb, sc, score)
            s_pages = [jnp.dot(qb, kt.astype(BF16), preferred_element_type=F32) for kt in kts]
            for t in range(ppb):
                s_ref[jb, :, page_cols(t)] = s_pages[t]
            s_max = jnp.max(functools.reduce(jnp.maximum, s_pages), axis=1, keepdims=True)
            bmax = jnp.where(lane == jb, s_max, bmax)
        score_ref[...], bmax_ref[...] = score, bmax
        for t in range(pages_per_iter):
            refill(pos0 + t)
        return carry

    lax.fori_loop(0, n_blocks // unroll, k_blocks, 0)

    sel = _select_topk(score_ref[...].T[:n_q], lane, n_blocks, axis=1)
    qs = qbd * scale
    kn = kn_ref[...]
    own = []
    for t in range(dec_seq):
        so = jnp.sum(qs * kn[t:t + 1, :], axis=1, keepdims=True)
        own.append(jnp.where(t <= tok, so, NEG_INF))
    m = jnp.max(jnp.where(sel > 0.5, bmax_ref[...], NEG_INF), axis=1, keepdims=True)
    m = functools.reduce(jnp.maximum, own, m)
    p_own = jnp.zeros((n_q, LANES), F32)
    for t in range(dec_seq):
        p_own = jnp.where(lane == t, jnp.exp(own[t] - m), p_own)
    acc_ref[...] = jnp.zeros(acc_ref.shape, F32)

    def v_blocks(it, lsum):
        pos0 = n_pages + it * pages_per_iter
        slots = [take(pos0 + t) for t in range(pages_per_iter)]
        weights = []
        for u in range(unroll):
            jb = it * unroll + u
            picked = jnp.sum(jnp.where(lane == jb, sel, 0.0), axis=1, keepdims=True) > 0.5
            pj = jnp.exp(jnp.where(picked, s_ref[jb] - m, NEG_INF))
            lsum = lsum + pj
            weights.append(pj.astype(BF16))
        acc = acc_ref[...]
        for u in range(unroll):
            for t in range(ppb):
                vt = ring_ref[slots[u * ppb + t]].astype(BF16)
                acc = acc + lax.dot_general(weights[u][:, page_cols(t)], vt, NT_DIMS,
                                            preferred_element_type=F32)
        acc_ref[...] = acc
        for t in range(pages_per_iter):
            refill(pos0 + t)
        return lsum

    lsum = lax.fori_loop(0, n_blocks // unroll, v_blocks, jnp.zeros((n_q, MOBA_BLOCK), F32))

    acc = acc_ref[...]
    vn = vn_ref[...]
    for t in range(dec_seq):
        acc = acc + p_own[:, t:t + 1] * vn[t:t + 1, :]
    l = jnp.sum(lsum, axis=1, keepdims=True) + jnp.sum(p_own, axis=1, keepdims=True)
    out = jnp.where(head_diag, acc / l, 0.0)
    o_ref[...] = functools.reduce(
        lambda x, y: x + y, [out[h * dec_seq:(h + 1) * dec_seq, :] for h in range(H_A)])


def _moba_sample(qa, ka, va, cache_k, cache_v, page_table, dec_batch, dec_seq):
    n_pool, page_rows = cache_k.shape[0], cache_k.shape[1]
    n_pages = page_table.shape[1]
    past = n_pages * page_rows
    assert past % MOBA_BLOCK == 0 and MOBA_BLOCK % page_rows == 0, "cached rows must fill whole MoBA blocks"
    n_blocks = past // MOBA_BLOCK
    assert MOBA_TOPK <= n_blocks <= LANES
    n_ring = SAMPLE_RING_PAGES
    assert page_rows % LANES == 0 and n_ring <= n_pages and (2 * n_pages) % n_ring == 0
    assert n_blocks % SAMPLE_BLOCKS_PER_ITER == 0
    n_q = H_A * dec_seq
    assert n_q <= LANES
    ck = cache_k.transpose(0, 2, 3, 1).reshape(n_pool, A_W, page_rows)
    cv = cache_v.transpose(0, 2, 3, 1).reshape(n_pool, A_W, page_rows)

    tok_spec = pl.BlockSpec((dec_seq, A_W), lambda b, pt: (b, 0))
    pool_spec = pl.BlockSpec(memory_space=pl.ANY)
    grid_spec = pltpu.PrefetchScalarGridSpec(
        num_scalar_prefetch=1,
        grid=(dec_batch,),
        in_specs=[tok_spec, tok_spec, tok_spec, pool_spec, pool_spec],
        out_specs=tok_spec,
        scratch_shapes=[pltpu.VMEM((n_ring, A_W, page_rows), F32),
                        pltpu.SemaphoreType.DMA((n_ring,)),
                        pltpu.VMEM((n_q, A_W), F32),
                        pltpu.VMEM((A_W, LANES), F32),
                        pltpu.VMEM((n_blocks, n_q, MOBA_BLOCK), F32),
                        pltpu.VMEM((LANES, LANES), F32),
                        pltpu.VMEM((n_q, LANES), F32),
                        pltpu.VMEM((n_q, A_W), F32)],
    )
    return pl.pallas_call(
        functools.partial(_moba_sample_kernel, n_pages=n_pages, n_blocks=n_blocks,
                          page_rows=page_rows, dec_seq=dec_seq),
        grid_spec=grid_spec,
        out_shape=jax.ShapeDtypeStruct((dec_batch * dec_seq, A_W), F32),
        compiler_params=_compiler_params(("arbitrary",)),
        name="moba_sample",
    )(page_table, qa, ka, va, ck, cv)


def _mlstm_kernel(q_ref, k_ref, v_ref, o_ref, gate_ref, c0_ref, n0_ref, m0_ref,
                  h_ref, c_ref, n_ref, m_ref, *, chunk):
    step = pl.program_id(1)
    lp = MLSTM_CHUNK

    @pl.when(step == 0)
    def _():
        c_ref[...] = c0_ref[...]
        n_ref[...] = n0_ref[...]
        m_ref[...] = m0_ref[...]

    def pad_rows(a):
        if chunk == lp:
            return a
        return jnp.concatenate([a, jnp.zeros((lp - chunk, a.shape[1]), a.dtype)], axis=0)

    q_all, k_all, v_all = pad_rows(q_ref[...]), pad_rows(k_ref[...]), pad_rows(v_ref[...])
    o_all, gates = pad_rows(o_ref[...]), pad_rows(gate_ref[...])
    r = lax.broadcasted_iota(jnp.int32, (lp, lp), 0)
    c = lax.broadcasted_iota(jnp.int32, (lp, lp), 1)
    real = lax.broadcasted_iota(jnp.int32, (lp, 1), 0) < chunk
    for h in range(H_M):
        lanes = slice(h * DH_M, (h + 1) * DH_M)
        q, k, v = q_all[:, lanes], k_all[:, lanes], v_all[:, lanes]
        qb, kb = q.astype(BF16), k.astype(BF16)
        i_col = gates[:, h:h + 1]
        f_col = gates[:, H_M + h:H_M + h + 1]
        f_row = jnp.sum(jnp.where(r == c, f_col, 0.0), axis=0, keepdims=True)
        i_row = jnp.sum(jnp.where(r == c, i_col, 0.0), axis=0, keepdims=True)
        b_row = jnp.sum(jnp.where(r <= c, f_col, 0.0), axis=0, keepdims=True)
        b_col = jnp.sum(jnp.where(c <= r, f_row, 0.0), axis=1, keepdims=True)
        m_prev = m_ref[0, h]
        log_d = jnp.where(c <= r, b_col - b_row + i_row, NEG_INF)
        m_t = jnp.maximum(m_prev + b_col, jnp.max(log_d, axis=1, keepdims=True))
        w_d = jnp.exp(log_d - m_t)
        w_prev = jnp.exp(m_prev + b_col - m_t)
        a = lax.dot_general(qb, kb, NT_DIMS, preferred_element_type=F32) * w_d
        c_prev = c_ref[0, h]
        n_prev = n_ref[0, h]
        q_c = lax.dot_general(qb, c_prev.astype(BF16), NT_DIMS, preferred_element_type=F32)
        num = jnp.dot(a.astype(BF16), v.astype(BF16), preferred_element_type=F32) + w_prev * q_c
        den = jnp.sum(a, axis=1, keepdims=True) + w_prev * jnp.sum(q * n_prev, axis=1, keepdims=True)
        hc = num / jnp.maximum(jnp.abs(den), jnp.exp(-m_t))
        out = jax.nn.sigmoid(o_all[:, lanes]) * hc
        h_ref[:, lanes] = out[:chunk]
        m_new = m_t[chunk - 1:chunk, :]
        b_last = b_col[chunk - 1:chunk, :]
        decay = jnp.exp(m_prev + b_last - m_new)
        w_s = jnp.where(real, jnp.exp(b_last - b_col + i_col - m_new), 0.0)
        c_ref[0, h] = decay * c_prev + lax.dot_general((w_s * v).astype(BF16), kb, TN_DIMS,
                                                       preferred_element_type=F32)
        n_ref[0, h] = decay * n_prev + jnp.sum(w_s * k, axis=0, keepdims=True)
        m_ref[0, h] = m_new


def _mlstm(qm, km, vm, om, gates, c0, n0, m0, batch, seq):
    chunk = math.gcd(seq, MLSTM_CHUNK)
    n_chunks = seq // chunk
    n0_4 = n0.reshape(batch, H_M, 1, DH_M)
    m0_4 = m0.reshape(batch, H_M, 1, 1)
    row = lambda w: pl.BlockSpec((chunk, w), lambda b, s: (b * n_chunks + s, 0))
    state = lambda a: pl.BlockSpec((1,) + a.shape[1:], lambda b, s: (b, 0, 0, 0))
    h, c, n, m = pl.pallas_call(
        functools.partial(_mlstm_kernel, chunk=chunk),
        grid=(batch, n_chunks),
        in_specs=[row(M_W), row(M_W), row(M_W), row(M_W), row(LANES), state(c0), state(n0_4), state(m0_4)],
        out_specs=(row(M_W), state(c0), state(n0_4), state(m0_4)),
        out_shape=(jax.ShapeDtypeStruct((batch * seq, M_W), F32),
                   jax.ShapeDtypeStruct(c0.shape, F32),
                   jax.ShapeDtypeStruct(n0_4.shape, F32),
                   jax.ShapeDtypeStruct(m0_4.shape, F32)),
        compiler_params=_compiler_params(("parallel", "arbitrary")),
        name="mlstm",
    )(qm, km, vm, om, gates, c0, n0_4, m0_4)
    return h, c, n.reshape(batch, H_M, DH_M), m.reshape(batch, H_M)


def _merge_kernel(x_ref, att_ref, ml_ref, ga_ref, gm_ref, wa_ref, wm_ref, wo_ref, g_ref, o_ref):
    ya = jnp.dot(att_ref[...].astype(BF16), wa_ref[...], preferred_element_type=F32)
    ym = jnp.dot(ml_ref[...].astype(BF16), wm_ref[...], preferred_element_type=F32)
    u = jax.nn.sigmoid(ga_ref[...]) * ya + jax.nn.sigmoid(gm_ref[...]) * ym
    r = jnp.dot(u.astype(BF16), wo_ref[...], preferred_element_type=F32)
    o_ref[...] = x_ref[...] + _rms(r, g_ref[...])


def _merge(x2d, att, ml, ga, gm, wa, wm, wo, g_post_mix):
    n = x2d.shape[0]
    tm = min(512, n)
    row = lambda w: pl.BlockSpec((tm, w), lambda i: (i, 0))
    full = lambda a: pl.BlockSpec(a.shape, lambda i: (0,) * a.ndim)
    g_row = g_post_mix.reshape(1, D_MODEL)
    return pl.pallas_call(
        _merge_kernel,
        grid=(n // tm,),
        in_specs=[row(D_MODEL), row(A_W), row(M_W), row(D_MODEL), row(D_MODEL),
                  full(wa), full(wm), full(wo), full(g_row)],
        out_specs=row(D_MODEL),
        out_shape=jax.ShapeDtypeStruct((n, D_MODEL), F32),
        compiler_params=_compiler_params(("parallel",)),
        name="merge",
    )(x2d, att, ml, ga, gm, wa, wm, wo, g_row)


MLP_FF_TILE = 1024


def _mlp_kernel(x_ref, g1_ref, wu_ref, wd_ref, g2_ref, o_ref, h_ref, acc_ref):
    kk = pl.program_id(1)

    @pl.when(kk == 0)
    def _():
        h_ref[...] = _rms(x_ref[...], g1_ref[...]).astype(BF16)
        acc_ref[...] = jnp.zeros(acc_ref.shape, F32)

    up = jnp.dot(h_ref[...], wu_ref[...], preferred_element_type=F32)
    act = jnp.square(jnp.maximum(up, 0.0)).astype(BF16)
    acc_ref[...] += jnp.dot(act, wd_ref[...], preferred_element_type=F32)

    @pl.when(kk == pl.num_programs(1) - 1)
    def _():
        o_ref[...] = x_ref[...] + _rms(acc_ref[...], g2_ref[...])


def _mlp(x2d, g_pre_mlp, w_up, w_down, g_post_mlp):
    n = x2d.shape[0]
    tm = min(512, n)
    tf = MLP_FF_TILE
    g1 = g_pre_mlp.reshape(1, D_MODEL)
    g2 = g_post_mlp.reshape(1, D_MODEL)
    gspec = pl.BlockSpec((1, D_MODEL), lambda i, kk: (0, 0))
    return pl.pallas_call(
        _mlp_kernel,
        grid=(n // tm, D_FF // tf),
        in_specs=[pl.BlockSpec((tm, D_MODEL), lambda i, kk: (i, 0)), gspec,
                  pl.BlockSpec((D_MODEL, tf), lambda i, kk: (0, kk)),
                  pl.BlockSpec((tf, D_MODEL), lambda i, kk: (kk, 0)), gspec],
        out_specs=pl.BlockSpec((tm, D_MODEL), lambda i, kk: (i, 0)),
        out_shape=jax.ShapeDtypeStruct((n, D_MODEL), F32),
        scratch_shapes=[pltpu.VMEM((tm, D_MODEL), BF16), pltpu.VMEM((tm, D_MODEL), F32)],
        compiler_params=_compiler_params(("parallel", "arbitrary")),
        name="mlp",
    )(x2d, g1, w_up, w_down, g2)


def kernel(x_prompt, x_sample, cache_k, cache_v, state_C, state_n, state_m, page_table, g_pre_mix, w_in, b_if,
           w_attn_br, w_mlstm_br, w_out, g_post_mix, g_pre_mlp, w_up, w_down, g_post_mlp):
    bp, sp, _ = x_prompt.shape
    db, ds, _ = x_sample.shape
    past = page_table.shape[1] * cache_k.shape[1]

    gate0 = 3 * A_W + 4 * M_W
    w_main = jnp.concatenate([w_in[:, :gate0], w_in[:, gate0 + N_GATE:]], axis=1).astype(BF16)
    w_gate = jnp.pad(w_in[:, gate0:gate0 + N_GATE], ((0, 0), (0, LANES - N_GATE))).astype(BF16)
    b_if_row = jnp.pad(b_if.astype(F32), (0, LANES - N_GATE)).reshape(1, LANES)
    wa, wm, wo = w_attn_br.astype(BF16), w_mlstm_br.astype(BF16), w_out.astype(BF16)
    wu, wd = w_up.astype(BF16), w_down.astype(BF16)

    def tail(x2d, att, ml, ga, gm):
        x1 = _merge(x2d, att, ml, ga, gm, wa, wm, wo, g_post_mix)
        return _mlp(x1, g_pre_mlp, wu, wd, g_post_mlp)

    xp = x_prompt.reshape(bp * sp, D_MODEL)
    (qt, kt, vt, k_bf, vt_bf, kmean, qm, km, vm, om, ga, gm, gates) = _project(
        xp, jnp.arange(sp, dtype=F32), g_pre_mix, w_main, w_gate, b_if_row, feature_major=True)
    att_p = _moba_prompt(qt, k_bf, vt_bf, kmean, bp, sp)
    ml_p, c_p, n_p, m_p = _mlstm(qm, km, vm, om, gates,
                                 jnp.zeros((bp, H_M, DH_M, DH_M), F32), jnp.zeros((bp, H_M, DH_M), F32),
                                 jnp.zeros((bp, H_M), F32), bp, sp)
    y_prompt = tail(xp, att_p, ml_p, ga, gm).reshape(bp, sp, D_MODEL)
    k_prompt = kt.reshape(bp, H_A, DH_A, sp).transpose(0, 3, 1, 2)
    v_prompt = vt.reshape(bp, H_A, DH_A, sp).transpose(0, 3, 1, 2)

    xs = x_sample.reshape(db * ds, D_MODEL)
    (qa, ka, va, qm, km, vm, om, ga, gm, gates) = _project(
        xs, past + jnp.arange(ds, dtype=F32), g_pre_mix, w_main, w_gate, b_if_row, feature_major=False)
    att_s = _moba_sample(qa, ka, va, cache_k, cache_v, page_table, db, ds)
    ml_s, c_s, n_s, m_s = _mlstm(qm, km, vm, om, gates, state_C.astype(F32), state_n.astype(F32),
                                 state_m.astype(F32), db, ds)
    y_sample = tail(xs, att_s, ml_s, ga, gm).reshape(db, ds, D_MODEL)
    k_sample = ka.reshape(db, ds, H_A, DH_A)
    v_sample = va.reshape(db, ds, H_A, DH_A)

    return (y_prompt, y_sample, k_prompt, v_prompt, c_p, n_p, m_p, k_sample, v_sample, c_s, n_s, m_s)
```

```python
import functools
import math

import jax
import jax.numpy as jnp
from jax import lax
from jax.experimental import pallas as pl
from jax.experimental.pallas import tpu as pltpu

F32 = jnp.float32
BF16 = jnp.bfloat16
NEG_INF = float("-inf")
LOG2_E = math.log2(math.e)

D_MODEL = 1024
H_A = 8
DH_A = 64
A_W = H_A * DH_A
MOBA_BLOCK = 256
MOBA_TOPK = 3
ROT_DIMS = DH_A // 4
ROPE_THETA = 500000.0
H_M = 4
DH_M = 128
M_W = H_M * DH_M
MLSTM_CHUNK = 128
D_FF = 4 * D_MODEL
RMS_EPS = 1e-6
N_GATE = 2 * H_M

LANES = 128
HEADS_PER_LANE_GROUP = LANES // DH_A
VMEM_LIMIT_BYTES = 52 * 1024 * 1024

NT_DIMS = (((1,), (1,)), ((), ()))
TN_DIMS = (((0,), (0,)), ((), ()))


def _compiler_params(semantics):
    return pltpu.CompilerParams(dimension_semantics=semantics, vmem_limit_bytes=VMEM_LIMIT_BYTES)


def _rms(x, g):
    return x * lax.rsqrt(jnp.mean(x * x, axis=-1, keepdims=True) + RMS_EPS) * g


def _log_sigmoid(x):
    return jnp.minimum(x, 0.0) - jnp.log1p(jnp.exp(-jnp.abs(x)))


PROJ_ROWS = MOBA_BLOCK


def _project_kernel(x_ref, g_ref, w_ref, wg_ref, bif_ref, rc_ref, rs1_ref, rs2_ref, *out_refs, feature_major):
    if feature_major:
        (qt_ref, kt_ref, vt_ref, kb_ref, vtb_ref, kmean_ref,
         qm_ref, km_ref, vm_ref, om_ref, ga_ref, gm_ref, gates_ref) = out_refs
    else:
        qa_ref, ka_ref, va_ref, qm_ref, km_ref, vm_ref, om_ref, ga_ref, gm_ref, gates_ref = out_refs
    hb = _rms(x_ref[...], g_ref[...]).astype(BF16)

    def mm(c0, n):
        return jnp.dot(hb, w_ref[:, c0:c0 + n], preferred_element_type=F32)

    rc, rs1, rs2 = rc_ref[...], rs1_ref[...], rs2_ref[...]

    def rot(z):
        half = ROT_DIMS // 2
        outs = []
        for c in range(z.shape[1] // LANES):
            zc = z[:, c * LANES:(c + 1) * LANES]
            outs.append(zc * rc + pltpu.roll(zc, LANES - half, 1) * rs1 + pltpu.roll(zc, half, 1) * rs2)
        return jnp.concatenate(outs, axis=1)

    qa = rot(mm(0, A_W))
    ka = rot(mm(A_W, A_W))
    va = mm(2 * A_W, A_W)
    if feature_major:
        qt_ref[0] = qa.T
        kt_ref[0] = ka.T
        kb_ref[...] = ka.astype(BF16)
        kmean_ref[...] = jnp.mean(ka, axis=0, keepdims=True).reshape(1, 1, A_W)
        vt = va.T
        vt_ref[0] = vt
        vtb_ref[0, 0] = vt.astype(BF16)
    else:
        qa_ref[...] = qa
        ka_ref[...] = ka
        va_ref[...] = va
    c0 = 3 * A_W
    qm_ref[...] = mm(c0, M_W)
    km_ref[...] = mm(c0 + M_W, M_W) * (DH_M ** -0.5)
    vm_ref[...] = mm(c0 + 2 * M_W, M_W)
    om_ref[...] = mm(c0 + 3 * M_W, M_W)
    c1 = c0 + 4 * M_W
    ga_ref[...] = mm(c1, D_MODEL)
    gm_ref[...] = mm(c1 + D_MODEL, D_MODEL)
    zg = jnp.dot(hb, wg_ref[...], preferred_element_type=F32) + bif_ref[...]
    lane = lax.broadcasted_iota(jnp.int32, zg.shape, 1)
    is_forget = (lane >= H_M) & (lane < N_GATE)
    gates_ref[...] = jnp.where(is_forget, _log_sigmoid(zg), zg)


def _rotary_tables(pos):
    half = ROT_DIMS // 2
    inv = ROPE_THETA ** (-jnp.arange(half, dtype=F32) * 2.0 / ROT_DIMS)
    ang = pos[:, None] * inv[None, :]
    cos, sin = jnp.cos(ang), jnp.sin(ang)
    n = pos.shape[0]
    pad = jnp.zeros((n, DH_A - ROT_DIMS), F32)
    c_head = jnp.concatenate([cos, cos, pad + 1.0], axis=1)
    s1_head = jnp.concatenate([-sin, jnp.zeros_like(sin), pad], axis=1)
    s2_head = jnp.concatenate([jnp.zeros_like(sin), sin, pad], axis=1)
    rep = lambda t: jnp.tile(t, (1, HEADS_PER_LANE_GROUP))
    return rep(c_head), rep(s1_head), rep(s2_head)


def _project(x2d, pos, g_pre_mix, w_main, w_gate, b_if_row, feature_major):
    n = x2d.shape[0]
    tm = PROJ_ROWS
    n_tiles = n // tm
    rc, rs1, rs2 = _rotary_tables(pos)
    if pos.shape[0] < tm:
        reps = tm // pos.shape[0]
        rc, rs1, rs2 = (jnp.tile(t, (reps, 1)) for t in (rc, rs1, rs2))
    tab_tiles = rc.shape[0] // tm
    row = lambda w: pl.BlockSpec((tm, w), lambda i: (i, 0))
    full = lambda a: pl.BlockSpec(a.shape, lambda i: (0,) * a.ndim)
    tab = pl.BlockSpec((tm, LANES), lambda i: (i % tab_tiles, 0))
    f32 = lambda w: jax.ShapeDtypeStruct((n, w), F32)
    if feature_major:
        seq = pos.shape[0]
        batch = n // seq
        t_shape = jax.ShapeDtypeStruct((batch, A_W, seq), F32)
        t_spec = pl.BlockSpec((1, A_W, tm), lambda i: (i // tab_tiles, 0, i % tab_tiles))
        attn_shape = (t_shape, t_shape, t_shape, jax.ShapeDtypeStruct((n, A_W), BF16),
                      jax.ShapeDtypeStruct((batch, tab_tiles, A_W, tm), BF16),
                      jax.ShapeDtypeStruct((n_tiles, 1, A_W), F32))
        attn_specs = (t_spec, t_spec, t_spec, row(A_W),
                      pl.BlockSpec((1, 1, A_W, tm), lambda i: (i // tab_tiles, i % tab_tiles, 0, 0)),
                      pl.BlockSpec((1, 1, A_W), lambda i: (i, 0, 0)))
    else:
        attn_shape = (f32(A_W), f32(A_W), f32(A_W))
        attn_specs = (row(A_W), row(A_W), row(A_W))
    out_shape = attn_shape + (f32(M_W), f32(M_W), f32(M_W), f32(M_W), f32(D_MODEL), f32(D_MODEL), f32(LANES))
    out_specs = attn_specs + (row(M_W), row(M_W), row(M_W), row(M_W), row(D_MODEL), row(D_MODEL), row(LANES))
    g_row = g_pre_mix.reshape(1, D_MODEL)
    return pl.pallas_call(
        functools.partial(_project_kernel, feature_major=feature_major),
        grid=(n_tiles,),
        in_specs=[row(D_MODEL), full(g_row), full(w_main), full(w_gate), full(b_if_row), tab, tab, tab],
        out_specs=out_specs,
        out_shape=out_shape,
        compiler_params=_compiler_params(("parallel",)),
        name="project",
    )(x2d, g_row, w_main, w_gate, b_if_row, rc, rs1, rs2)


def _select_topk(scores, block_idx, n_valid, axis):
    width = scores.shape[axis]
    sc = jnp.where(block_idx < n_valid, scores, NEG_INF)
    sel = jnp.zeros(scores.shape, jnp.bool_)
    for _ in range(MOBA_TOPK):
        mx = jnp.max(sc, axis=axis, keepdims=True)
        idx = jnp.min(jnp.where(sc == mx, block_idx, width), axis=axis, keepdims=True)
        hit = block_idx == idx
        sel = sel | hit
        sc = jnp.where(hit, NEG_INF, sc)
    return jnp.where(sel & (block_idx < n_valid), 1.0, 0.0)


def _moba_prompt_kernel(qt_ref, k_ref, vt_ref, kmean_ref, o_ref,
                        qtb_ref, sel_ref, s0_ref, s1_ref, m_ref, l_ref, acc_ref, ot_ref, *, n_blocks):
    i = pl.program_id(1)
    tq = MOBA_BLOCK
    scale = DH_A ** -0.5
    key = lax.broadcasted_iota(jnp.int32, (tq, tq), 0)
    qry = lax.broadcasted_iota(jnp.int32, (tq, tq), 1)
    causal = key <= qry
    feat = lax.broadcasted_iota(jnp.int32, (LANES, tq), 0)
    blk = lax.broadcasted_iota(jnp.int32, (n_blocks, tq), 0)
    n_pairs = A_W // LANES
    pair_feats = [slice(p * LANES, (p + 1) * LANES) for p in range(n_pairs)]

    heads = [(p, p * HEADS_PER_LANE_GROUP + hh) for p in range(n_pairs) for hh in range(HEADS_PER_LANE_GROUP)]

    def score(h, p, block):
        start = block * tq if isinstance(block, int) else pl.multiple_of(block * tq, tq)
        k_blk = k_ref[0, pl.ds(start, tq), pair_feats[p]]
        return jnp.dot(k_blk, qtb_ref[h], preferred_element_type=F32)

    def fold(h, p, block, s, own):
        m_old = m_ref[h]
        if own:
            s = jnp.where(causal, s, NEG_INF)
            m_new = jnp.maximum(m_old, jnp.max(s, axis=0, keepdims=True))
            m_shift = m_exp = m_new
        else:
            picked = sel_ref[h * n_blocks + block] > 0.5
            m_new = jnp.maximum(m_old, jnp.where(picked, jnp.max(s, axis=0, keepdims=True), NEG_INF))
            m_shift = jnp.where(m_new == NEG_INF, 0.0, m_new)
            m_exp = jnp.where(picked, m_shift, jnp.inf)
        alpha = jnp.exp2(m_old - m_shift)
        pe = jnp.exp2(s - m_exp)
        pv = jnp.dot(vt_ref[0, block, pair_feats[p], :], pe.astype(BF16), preferred_element_type=F32)
        m_ref[h] = m_new
        l_ref[h] = alpha * l_ref[h] + jnp.sum(pe, axis=0, keepdims=True)
        acc_ref[h] = alpha * acc_ref[h] + pv

    for p in range(n_pairs):
        qt_pair = qt_ref[0, pair_feats[p], :]
        kmean_pair = kmean_ref[0, :, pair_feats[p]]
        for hh in range(HEADS_PER_LANE_GROUP):
            h = p * HEADS_PER_LANE_GROUP + hh
            qt_head = jnp.where((feat // DH_A) == hh, qt_pair, 0.0)
            scores = jnp.dot(kmean_pair, qt_head, precision=lax.Precision.HIGHEST, preferred_element_type=F32)
            sel = _select_topk(scores, blk, i, axis=0)
            for j in range(n_blocks):
                sel_ref[h * n_blocks + j] = sel[j:j + 1, :]
            qtb_ref[h] = (qt_head * (scale * LOG2_E)).astype(BF16)
            m_ref[h] = jnp.full((1, tq), NEG_INF, F32)
            l_ref[h] = jnp.zeros((1, tq), F32)
            acc_ref[h] = jnp.zeros((LANES, tq), F32)

    bufs = (s0_ref, s1_ref)

    def stage(block, cur, own, score_next=True):
        for p, h in heads:
            if score_next:
                bufs[1 - cur][h] = score(h, p, block + 1)
            fold(h, p, block, bufs[cur][h], own)

    for p, h in heads:
        s0_ref[h] = score(h, p, 0)

    def body(jj, carry):
        stage(2 * jj, 0, own=False)
        stage(2 * jj + 1, 1, own=False)
        return carry

    lax.fori_loop(0, i // 2, body, 0)

    @pl.when(i % 2 == 0)
    def _():
        stage(i, 0, own=True, score_next=False)

    @pl.when(i % 2 == 1)
    def _():
        stage(i - 1, 0, own=False)
        stage(i, 1, own=True, score_next=False)

    for p in range(n_pairs):
        h0 = p * HEADS_PER_LANE_GROUP
        ot_ref[pair_feats[p], :] = jnp.where(feat < DH_A, acc_ref[h0] / l_ref[h0], acc_ref[h0 + 1] / l_ref[h0 + 1])
    o_ref[...] = ot_ref[...].T.astype(o_ref.dtype)


def _moba_prompt(qt, k_bf, vt_bf, kmean, batch, seq):
    n_blocks = seq // MOBA_BLOCK
    tq = MOBA_BLOCK
    k3 = k_bf.reshape(batch, seq, A_W)
    km3 = kmean.reshape(batch, n_blocks, A_W)
    return pl.pallas_call(
        functools.partial(_moba_prompt_kernel, n_blocks=n_blocks),
        grid=(batch, n_blocks),
        in_specs=[pl.BlockSpec((1, A_W, tq), lambda b, i: (b, 0, i)),
                  pl.BlockSpec((1, seq, A_W), lambda b, i: (b, 0, 0)),
                  pl.BlockSpec((1, n_blocks, A_W, tq), lambda b, i: (b, 0, 0, 0)),
                  pl.BlockSpec((1, n_blocks, A_W), lambda b, i: (b, 0, 0))],
        out_specs=pl.BlockSpec((tq, A_W), lambda b, i: (b * n_blocks + i, 0)),
        out_shape=jax.ShapeDtypeStruct((batch * seq, A_W), BF16),
        scratch_shapes=[pltpu.VMEM((H_A, LANES, tq), BF16),
                        pltpu.VMEM((H_A * n_blocks, 1, tq), F32),
                        pltpu.VMEM((H_A, tq, tq), F32),
                        pltpu.VMEM((H_A, tq, tq), F32),
                        pltpu.VMEM((H_A, 1, tq), F32),
                        pltpu.VMEM((H_A, 1, tq), F32),
                        pltpu.VMEM((H_A, LANES, tq), F32),
                        pltpu.VMEM((A_W, tq), F32)],
        compiler_params=_compiler_params(("parallel", "arbitrary")),
        name="moba_prompt",
    )(qt, k3, vt_bf, km3)


SAMPLE_RING_PAGES = 32
SAMPLE_BLOCKS_PER_ITER = 4


def _moba_sample_kernel(pt_ref, q_ref, kn_ref, vn_ref, ck_ref, cv_ref, o_ref,
                        ring_ref, ring_sem, qbd_ref, qbdt_ref, s_ref, score_ref, bmax_ref, acc_ref,
                        *, n_pages, n_blocks, page_rows, dec_seq):
    b = pl.program_id(0)
    n_seq = pl.num_programs(0)
    n_ring = SAMPLE_RING_PAGES
    stream_len = 2 * n_pages
    ppb = MOBA_BLOCK // page_rows
    n_q = H_A * dec_seq

    def ring_copy(pool_ref, page, slot):
        return pltpu.make_async_copy(pool_ref.at[page], ring_ref.at[slot], ring_sem.at[slot])

    def start_fetch(seq, pos, keys_only=False):
        slot = pos % n_ring
        if keys_only:
            ring_copy(ck_ref, pt_ref[seq, pos], slot).start()
            return

        @pl.when(pos < n_pages)
        def _():
            ring_copy(ck_ref, pt_ref[seq, pos], slot).start()

        @pl.when(pos >= n_pages)
        def _():
            ring_copy(cv_ref, pt_ref[seq, pos - n_pages], slot).start()

    def take(pos):
        slot = pos % n_ring
        ring_copy(ck_ref, 0, slot).wait()
        return slot

    def refill(pos):
        nxt = pos + n_ring

        @pl.when(nxt < stream_len)
        def _():
            start_fetch(b, nxt)

        @pl.when((nxt >= stream_len) & (b + 1 < n_seq))
        def _():
            start_fetch(b + 1, nxt - stream_len, keys_only=True)

    @pl.when(b == 0)
    def _():
        for pos in range(n_ring):
            start_fetch(0, pos, keys_only=True)

    scale = DH_A ** -0.5
    row = lax.broadcasted_iota(jnp.int32, (n_q, A_W), 0)
    lane_w = lax.broadcasted_iota(jnp.int32, (n_q, A_W), 1)
    head_diag = (row // dec_seq) == (lane_w // DH_A)
    lane = lax.broadcasted_iota(jnp.int32, (n_q, LANES), 1)
    tok = lax.broadcasted_iota(jnp.int32, (n_q, 1), 0) % dec_seq
    score_blk = lax.broadcasted_iota(jnp.int32, (LANES, LANES), 0)

    def page_cols(t):
        return slice(t * page_rows, (t + 1) * page_rows)

    q_rep = jnp.concatenate([q_ref[...]] * H_A, axis=0)
    qbd = jnp.where(head_diag, q_rep, 0.0)
    qbd_ref[...] = qbd
    qbdt_ref[...] = jnp.concatenate([qbd, jnp.zeros((LANES - n_q, A_W), F32)], axis=0).T
    score_ref[...] = jnp.full(score_ref.shape, NEG_INF, F32)
    bmax_ref[...] = jnp.full(bmax_ref.shape, NEG_INF, F32)

    unroll = SAMPLE_BLOCKS_PER_ITER
    pages_per_iter = unroll * ppb

    def k_blocks(it, carry):
        qb = (qbd_ref[...] * scale).astype(BF16)
        qbdt = qbdt_ref[...]
        pos0 = it * pages_per_iter
        slots = [take(pos0 + t) for t in range(pages_per_iter)]
        score, bmax = score_ref[...], bmax_ref[...]
        for u in range(unroll):
            jb = it * unroll + u
            kts = [ring_ref[slots[u * ppb + t]] for t in range(ppb)]
            kmean = jnp.sum(functools.reduce(lambda x, y: x + y, kts), axis=1, keepdims=True) * (1.0 / MOBA_BLOCK)
            sc = jnp.sum(qbdt * kmean, axis=0, keepdims=True)
            score = jnp.where(score_blk == jb, sc, score)
            s_pages = [jnp.dot(qb, kt.astype(BF16), preferred_element_type=F32) for kt in kts]
            for t in range(ppb):
                s_ref[jb, :, page_cols(t)] = s_pages[t]
            s_max = jnp.max(functools.reduce(jnp.maximum, s_pages), axis=1, keepdims=True)
            bmax = jnp.where(lane == jb, s_max, bmax)
        score_ref[...], bmax_ref[...] = score, bmax
        for t in range(pages_per_iter):
            refill(pos0 + t)
        return carry

    lax.fori_loop(0, n_blocks // unroll, k_blocks, 0)

    sel = _select_topk(score_ref[...].T[:n_q], lane, n_blocks, axis=1)
    qs = qbd * scale
    kn = kn_ref[...]
    own = []
    for t in range(dec_seq):
        so = jnp.sum(qs * kn[t:t + 1, :], axis=1, keepdims=True)
        own.append(jnp.where(t <= tok, so, NEG_INF))
    m = jnp.max(jnp.where(sel > 0.5, bmax_ref[...], NEG_INF), axis=1, keepdims=True)
    m = functools.reduce(jnp.maximum, own, m)
    p_own = jnp.zeros((n_q, LANES), F32)
    for t in range(dec_seq):
        p_own = jnp.where(lane == t, jnp.exp(own[t] - m), p_own)
    acc_ref[...] = jnp.zeros(acc_ref.shape, F32)

    def v_blocks(it, lsum):
        pos0 = n_pages + it * pages_per_iter
        slots = [take(pos0 + t) for t in range(pages_per_iter)]
        weights = []
        for u in range(unroll):
            jb = it * unroll + u
            picked = jnp.sum(jnp.where(lane == jb, sel, 0.0), axis=1, keepdims=True) > 0.5
            pj = jnp.exp(jnp.where(picked, s_ref[jb] - m, NEG_INF))
            lsum = lsum + pj
            weights.append(pj.astype(BF16))
        acc = acc_ref[...]
        for u in range(unroll):
            for t in range(ppb):
                vt = ring_ref[slots[u * ppb + t]].astype(BF16)
                acc = acc + lax.dot_general(weights[u][:, page_cols(t)], vt, NT_DIMS,
                                            preferred_element_type=F32)
        acc_ref[...] = acc
        for t in range(pages_per_iter):
            refill(pos0 + t)
        return lsum

    lsum = lax.fori_loop(0, n_blocks // unroll, v_blocks, jnp.zeros((n_q, MOBA_BLOCK), F32))

    acc = acc_ref[...]
    vn = vn_ref[...]
    for t in range(dec_seq):
        acc = acc + p_own[:, t:t + 1] * vn[t:t + 1, :]
    l = jnp.sum(lsum, axis=1, keepdims=True) + jnp.sum(p_own, axis=1, keepdims=True)
    out = jnp.where(head_diag, acc / l, 0.0)
    o_ref[...] = functools.reduce(
        lambda x, y: x + y, [out[h * dec_seq:(h + 1) * dec_seq, :] for h in range(H_A)])


def _moba_sample(qa, ka, va, cache_k, cache_v, page_table, dec_batch, dec_seq):
    n_pool, page_rows = cache_k.shape[0], cache_k.shape[1]
    n_pages = page_table.shape[1]
    past = n_pages * page_rows
    assert past % MOBA_BLOCK == 0 and MOBA_BLOCK % page_rows == 0, "cached rows must fill whole MoBA blocks"
    n_blocks = past // MOBA_BLOCK
    assert MOBA_TOPK <= n_blocks <= LANES
    n_ring = SAMPLE_RING_PAGES
    assert page_rows % LANES == 0 and n_ring <= n_pages and (2 * n_pages) % n_ring == 0
    assert n_blocks % SAMPLE_BLOCKS_PER_ITER == 0
    n_q = H_A * dec_seq
    assert n_q <= LANES
    ck = cache_k.transpose(0, 2, 3, 1).reshape(n_pool, A_W, page_rows)
    cv = cache_v.transpose(0, 2, 3, 1).reshape(n_pool, A_W, page_rows)

    tok_spec = pl.BlockSpec((dec_seq, A_W), lambda b, pt: (b, 0))
    pool_spec = pl.BlockSpec(memory_space=pl.ANY)
    grid_spec = pltpu.PrefetchScalarGridSpec(
        num_scalar_prefetch=1,
        grid=(dec_batch,),
        in_specs=[tok_spec, tok_spec, tok_spec, pool_spec, pool_spec],
        out_specs=tok_spec,
        scratch_shapes=[pltpu.VMEM((n_ring, A_W, page_rows), F32),
                        pltpu.SemaphoreType.DMA((n_ring,)),
                        pltpu.VMEM((n_q, A_W), F32),
                        pltpu.VMEM((A_W, LANES), F32),
                        pltpu.VMEM((n_blocks, n_q, MOBA_BLOCK), F32),
                        pltpu.VMEM((LANES, LANES), F32),
                        pltpu.VMEM((n_q, LANES), F32),
                        pltpu.VMEM((n_q, A_W), F32)],
    )
    return pl.pallas_call(
        functools.partial(_moba_sample_kernel, n_pages=n_pages, n_blocks=n_blocks,
                          page_rows=page_rows, dec_seq=dec_seq),
        grid_spec=grid_spec,
        out_shape=jax.ShapeDtypeStruct((dec_batch * dec_seq, A_W), F32),
        compiler_params=_compiler_params(("arbitrary",)),
        name="moba_sample",
    )(page_table, qa, ka, va, ck, cv)


def _mlstm_kernel(q_ref, k_ref, v_ref, o_ref, gate_ref, c0_ref, n0_ref, m0_ref,
                  h_ref, c_ref, n_ref, m_ref, *, chunk):
    step = pl.program_id(1)
    lp = MLSTM_CHUNK

    @pl.when(step == 0)
    def _():
        c_ref[...] = c0_ref[...]
        n_ref[...] = n0_ref[...]
        m_ref[...] = m0_ref[...]

    def pad_rows(a):
        if chunk == lp:
            return a
        return jnp.concatenate([a, jnp.zeros((lp - chunk, a.shape[1]), a.dtype)], axis=0)

    q_all, k_all, v_all = pad_rows(q_ref[...]), pad_rows(k_ref[...]), pad_rows(v_ref[...])
    o_all, gates = pad_rows(o_ref[...]), pad_rows(gate_ref[...])
    r = lax.broadcasted_iota(jnp.int32, (lp, lp), 0)
    c = lax.broadcasted_iota(jnp.int32, (lp, lp), 1)
    real = lax.broadcasted_iota(jnp.int32, (lp, 1), 0) < chunk
    for h in range(H_M):
        lanes = slice(h * DH_M, (h + 1) * DH_M)
        q, k, v = q_all[:, lanes], k_all[:, lanes], v_all[:, lanes]
        qb, kb = q.astype(BF16), k.astype(BF16)
        i_col = gates[:, h:h + 1]
        f_col = gates[:, H_M + h:H_M + h + 1]
        f_row = jnp.sum(jnp.where(r == c, f_col, 0.0), axis=0, keepdims=True)
        i_row = jnp.sum(jnp.where(r == c, i_col, 0.0), axis=0, keepdims=True)
        b_row = jnp.sum(jnp.where(r <= c, f_col, 0.0), axis=0, keepdims=True)
        b_col = jnp.sum(jnp.where(c <= r, f_row, 0.0), axis=1, keepdims=True)
        m_prev = m_ref[0, h]
        log_d = jnp.where(c <= r, b_col - b_row + i_row, NEG_INF)
        m_t = jnp.maximum(m_prev + b_col, jnp.max(log_d, axis=1, keepdims=True))
        w_d = jnp.exp(log_d - m_t)
        w_prev = jnp.exp(m_prev + b_col - m_t)
        a = lax.dot_general(qb, kb, NT_DIMS, preferred_element_type=F32) * w_d
        c_prev = c_ref[0, h]
        n_prev = n_ref[0, h]
        q_c = lax.dot_general(qb, c_prev.astype(BF16), NT_DIMS, preferred_element_type=F32)
        num = jnp.dot(a.astype(BF16), v.astype(BF16), preferred_element_type=F32) + w_prev * q_c
        den = jnp.sum(a, axis=1, keepdims=True) + w_prev * jnp.sum(q * n_prev, axis=1, keepdims=True)
        hc = num / jnp.maximum(jnp.abs(den), jnp.exp(-m_t))
        out = jax.nn.sigmoid(o_all[:, lanes]) * hc
        h_ref[:, lanes] = out[:chunk]
        m_new = m_t[chunk - 1:chunk, :]
        b_last = b_col[chunk - 1:chunk, :]
        decay = jnp.exp(m_prev + b_last - m_new)
        w_s = jnp.where(real, jnp.exp(b_last - b_col + i_col - m_new), 0.0)
        c_ref[0, h] = decay * c_prev + lax.dot_general((w_s * v).astype(BF16), kb, TN_DIMS,
                                                       preferred_element_type=F32)
        n_ref[0, h] = decay * n_prev + jnp.sum(w_s * k, axis=0, keepdims=True)
        m_ref[0, h] = m_new


def _mlstm(qm, km, vm, om, gates, c0, n0, m0, batch, seq):
    chunk = math.gcd(seq, MLSTM_CHUNK)
    n_chunks = seq // chunk
    n0_4 = n0.reshape(batch, H_M, 1, DH_M)
    m0_4 = m0.reshape(batch, H_M, 1, 1)
    row = lambda w: pl.BlockSpec((chunk, w), lambda b, s: (b * n_chunks + s, 0))
    state = lambda a: pl.BlockSpec((1,) + a.shape[1:], lambda b, s: (b, 0, 0, 0))
    h, c, n, m = pl.pallas_call(
        functools.partial(_mlstm_kernel, chunk=chunk),
        grid=(batch, n_chunks),
        in_specs=[row(M_W), row(M_W), row(M_W), row(M_W), row(LANES), state(c0), state(n0_4), state(m0_4)],
        out_specs=(row(M_W), state(c0), state(n0_4), state(m0_4)),
        out_shape=(jax.ShapeDtypeStruct((batch * seq, M_W), F32),
                   jax.ShapeDtypeStruct(c0.shape, F32),
                   jax.ShapeDtypeStruct(n0_4.shape, F32),
                   jax.ShapeDtypeStruct(m0_4.shape, F32)),
        compiler_params=_compiler_params(("parallel", "arbitrary")),
        name="mlstm",
    )(qm, km, vm, om, gates, c0, n0_4, m0_4)
    return h, c, n.reshape(batch, H_M, DH_M), m.reshape(batch, H_M)


def _merge_kernel(x_ref, att_ref, ml_ref, ga_ref, gm_ref, wa_ref, wm_ref, wo_ref, g_ref, o_ref):
    ya = jnp.dot(att_ref[...].astype(BF16), wa_ref[...], preferred_element_type=F32)
    ym = jnp.dot(ml_ref[...].astype(BF16), wm_ref[...], preferred_element_type=F32)
    u = jax.nn.sigmoid(ga_ref[...]) * ya + jax.nn.sigmoid(gm_ref[...]) * ym
    r = jnp.dot(u.astype(BF16), wo_ref[...], preferred_element_type=F32)
    o_ref[...] = x_ref[...] + _rms(r, g_ref[...])


def _merge(x2d, att, ml, ga, gm, wa, wm, wo, g_post_mix):
    n = x2d.shape[0]
    tm = min(512, n)
    row = lambda w: pl.BlockSpec((tm, w), lambda i: (i, 0))
    full = lambda a: pl.BlockSpec(a.shape, lambda i: (0,) * a.ndim)
    g_row = g_post_mix.reshape(1, D_MODEL)
    return pl.pallas_call(
        _merge_kernel,
        grid=(n // tm,),
        in_specs=[row(D_MODEL), row(A_W), row(M_W), row(D_MODEL), row(D_MODEL),
                  full(wa), full(wm), full(wo), full(g_row)],
        out_specs=row(D_MODEL),
        out_shape=jax.ShapeDtypeStruct((n, D_MODEL), F32),
        compiler_params=_compiler_params(("parallel",)),
        name="merge",
    )(x2d, att, ml, ga, gm, wa, wm, wo, g_row)


MLP_FF_TILE = 1024


def _mlp_kernel(x_ref, g1_ref, wu_ref, wd_ref, g2_ref, o_ref):
    x = x_ref[...]
    h = _rms(x, g1_ref[...]).astype(BF16)
    n_chunks = D_FF // MLP_FF_TILE
    cols = lambda c: slice(c * MLP_FF_TILE, (c + 1) * MLP_FF_TILE)
    up = lambda c: jnp.dot(h, wu_ref[:, cols(c)], preferred_element_type=F32)
    pending = up(0)
    acc = None
    for c in range(n_chunks):
        act = jnp.square(jnp.maximum(pending, 0.0)).astype(BF16)
        if c + 1 < n_chunks:
            pending = up(c + 1)
        part = jnp.dot(act, wd_ref[cols(c), :], preferred_element_type=F32)
        acc = part if acc is None else acc + part
    o_ref[...] = x + _rms(acc, g2_ref[...])


def _mlp(x2d, g_pre_mlp, w_up, w_down, g_post_mlp):
    n = x2d.shape[0]
    tm = min(512, n)
    g1 = g_pre_mlp.reshape(1, D_MODEL)
    g2 = g_post_mlp.reshape(1, D_MODEL)
    gspec = pl.BlockSpec((1, D_MODEL), lambda i: (0, 0))
    resident = lambda a: pl.BlockSpec(a.shape, lambda i: (0, 0), pipeline_mode=pl.Buffered(1))
    return pl.pallas_call(
        _mlp_kernel,
        grid=(n // tm,),
        in_specs=[pl.BlockSpec((tm, D_MODEL), lambda i: (i, 0)), gspec, resident(w_up), resident(w_down), gspec],
        out_specs=pl.BlockSpec((tm, D_MODEL), lambda i: (i, 0)),
        out_shape=jax.ShapeDtypeStruct((n, D_MODEL), F32),
        compiler_params=_compiler_params(("parallel",)),
        name="mlp",
    )(x2d, g1, w_up, w_down, g2)


def kernel(x_prompt, x_sample, cache_k, cache_v, state_C, state_n, state_m, page_table, g_pre_mix, w_in, b_if,
           w_attn_br, w_mlstm_br, w_out, g_post_mix, g_pre_mlp, w_up, w_down, g_post_mlp):
    bp, sp, _ = x_prompt.shape
    db, ds, _ = x_sample.shape
    past = page_table.shape[1] * cache_k.shape[1]

    gate0 = 3 * A_W + 4 * M_W
    w_main = jnp.concatenate([w_in[:, :gate0], w_in[:, gate0 + N_GATE:]], axis=1).astype(BF16)
    w_gate = jnp.pad(w_in[:, gate0:gate0 + N_GATE], ((0, 0), (0, LANES - N_GATE))).astype(BF16)
    b_if_row = jnp.pad(b_if.astype(F32), (0, LANES - N_GATE)).reshape(1, LANES)
    wa, wm, wo = w_attn_br.astype(BF16), w_mlstm_br.astype(BF16), w_out.astype(BF16)
    wu, wd = w_up.astype(BF16), w_down.astype(BF16)

    def tail(x2d, att, ml, ga, gm):
        x1 = _merge(x2d, att, ml, ga, gm, wa, wm, wo, g_post_mix)
        return _mlp(x1, g_pre_mlp, wu, wd, g_post_mlp)

    xp = x_prompt.reshape(bp * sp, D_MODEL)
    (qt, kt, vt, k_bf, vt_bf, kmean, qm, km, vm, om, ga, gm, gates) = _project(
        xp, jnp.arange(sp, dtype=F32), g_pre_mix, w_main, w_gate, b_if_row, feature_major=True)
    att_p = _moba_prompt(qt, k_bf, vt_bf, kmean, bp, sp)
    ml_p, c_p, n_p, m_p = _mlstm(qm, km, vm, om, gates,
                                 jnp.zeros((bp, H_M, DH_M, DH_M), F32), jnp.zeros((bp, H_M, DH_M), F32),
                                 jnp.zeros((bp, H_M), F32), bp, sp)
    y_prompt = tail(xp, att_p, ml_p, ga, gm).reshape(bp, sp, D_MODEL)
    k_prompt = kt.reshape(bp, H_A, DH_A, sp).transpose(0, 3, 1, 2)
    v_prompt = vt.reshape(bp, H_A, DH_A, sp).transpose(0, 3, 1, 2)

    xs = x_sample.reshape(db * ds, D_MODEL)
    (qa, ka, va, qm, km, vm, om, ga, gm, gates) = _project(
        xs, past + jnp.arange(ds, dtype=F32), g_pre_mix, w_main, w_gate, b_if_row, feature_major=False)
    att_s = _moba_sample(qa, ka, va, cache_k, cache_v, page_table, db, ds)
    ml_s, c_s, n_s, m_s = _mlstm(qm, km, vm, om, gates, state_C.astype(F32), state_n.astype(F32),
                                 state_m.astype(F32), db, ds)
    y_sample = tail(xs, att_s, ml_s, ga, gm).reshape(db, ds, D_MODEL)
    k_sample = ka.reshape(db, ds, H_A, DH_A)
    v_sample = va.reshape(db, ds, H_A, DH_A)

    return (y_prompt, y_sample, k_prompt, v_prompt, c_p, n_p, m_p, k_sample, v_sample, c_s, n_s, m_s)
```

```python
import functools
import math

import jax
import jax.numpy as jnp
from jax import lax
from jax.experimental import pallas as pl
from jax.experimental.pallas import tpu as pltpu

F32 = jnp.float32
BF16 = jnp.bfloat16
NEG_INF = float("-inf")
LOG2_E = math.log2(math.e)

D_MODEL = 1024
H_A = 8
DH_A = 64
A_W = H_A * DH_A
MOBA_BLOCK = 256
MOBA_TOPK = 3
ROT_DIMS = DH_A // 4
ROPE_THETA = 500000.0
H_M = 4
DH_M = 128
M_W = H_M * DH_M
MLSTM_CHUNK = 128
D_FF = 4 * D_MODEL
RMS_EPS = 1e-6
N_GATE = 2 * H_M

LANES = 128
HEADS_PER_LANE_GROUP = LANES // DH_A
VMEM_LIMIT_BYTES = 52 * 1024 * 1024

NT_DIMS = (((1,), (1,)), ((), ()))
TN_DIMS = (((0,), (0,)), ((), ()))


def _compiler_params(semantics):
    return pltpu.CompilerParams(dimension_semantics=semantics, vmem_limit_bytes=VMEM_LIMIT_BYTES)


def _rms(x, g):
    return x * lax.rsqrt(jnp.mean(x * x, axis=-1, keepdims=True) + RMS_EPS) * g


def _log_sigmoid(x):
    return jnp.minimum(x, 0.0) - jnp.log1p(jnp.exp(-jnp.abs(x)))


PROJ_ROWS = MOBA_BLOCK


def _project_kernel(x_ref, g_ref, w_ref, wg_ref, bif_ref, rc_ref, rs1_ref, rs2_ref, *out_refs, feature_major):
    if feature_major:
        (qt_ref, kt_ref, vt_ref, kb_ref, vtb_ref, kmean_ref,
         qm_ref, km_ref, vm_ref, om_ref, ga_ref, gm_ref, gates_ref) = out_refs
    else:
        qa_ref, ka_ref, va_ref, qm_ref, km_ref, vm_ref, om_ref, ga_ref, gm_ref, gates_ref = out_refs
    hb = _rms(x_ref[...], g_ref[...]).astype(BF16)

    def mm(c0, n):
        return jnp.dot(hb, w_ref[:, c0:c0 + n], preferred_element_type=F32)

    rc, rs1, rs2 = rc_ref[...], rs1_ref[...], rs2_ref[...]

    def rot(z):
        half = ROT_DIMS // 2
        outs = []
        for c in range(z.shape[1] // LANES):
            zc = z[:, c * LANES:(c + 1) * LANES]
            outs.append(zc * rc + pltpu.roll(zc, LANES - half, 1) * rs1 + pltpu.roll(zc, half, 1) * rs2)
        return jnp.concatenate(outs, axis=1)

    qa = rot(mm(0, A_W))
    ka = rot(mm(A_W, A_W))
    va = mm(2 * A_W, A_W)
    if feature_major:
        qt_ref[0] = qa.T
        kt_ref[0] = ka.T
        kb_ref[...] = ka.astype(BF16)
        kmean_ref[...] = jnp.mean(ka, axis=0, keepdims=True).reshape(1, 1, A_W)
        vt = va.T
        vt_ref[0] = vt
        vtb_ref[0, 0] = vt.astype(BF16)
    else:
        qa_ref[...] = qa
        ka_ref[...] = ka
        va_ref[...] = va
    c0 = 3 * A_W
    qm_ref[...] = mm(c0, M_W)
    km_ref[...] = mm(c0 + M_W, M_W) * (DH_M ** -0.5)
    vm_ref[...] = mm(c0 + 2 * M_W, M_W)
    om_ref[...] = mm(c0 + 3 * M_W, M_W)
    c1 = c0 + 4 * M_W
    ga_ref[...] = mm(c1, D_MODEL)
    gm_ref[...] = mm(c1 + D_MODEL, D_MODEL)
    zg = jnp.dot(hb, wg_ref[...], preferred_element_type=F32) + bif_ref[...]
    lane = lax.broadcasted_iota(jnp.int32, zg.shape, 1)
    is_forget = (lane >= H_M) & (lane < N_GATE)
    gates_ref[...] = jnp.where(is_forget, _log_sigmoid(zg), zg)


def _rotary_tables(pos):
    half = ROT_DIMS // 2
    inv = ROPE_THETA ** (-jnp.arange(half, dtype=F32) * 2.0 / ROT_DIMS)
    ang = pos[:, None] * inv[None, :]
    cos, sin = jnp.cos(ang), jnp.sin(ang)
    n = pos.shape[0]
    pad = jnp.zeros((n, DH_A - ROT_DIMS), F32)
    c_head = jnp.concatenate([cos, cos, pad + 1.0], axis=1)
    s1_head = jnp.concatenate([-sin, jnp.zeros_like(sin), pad], axis=1)
    s2_head = jnp.concatenate([jnp.zeros_like(sin), sin, pad], axis=1)
    rep = lambda t: jnp.tile(t, (1, HEADS_PER_LANE_GROUP))
    return rep(c_head), rep(s1_head), rep(s2_head)


def _project(x2d, pos, g_pre_mix, w_main, w_gate, b_if_row, feature_major):
    n = x2d.shape[0]
    tm = PROJ_ROWS
    n_tiles = n // tm
    rc, rs1, rs2 = _rotary_tables(pos)
    if pos.shape[0] < tm:
        reps = tm // pos.shape[0]
        rc, rs1, rs2 = (jnp.tile(t, (reps, 1)) for t in (rc, rs1, rs2))
    tab_tiles = rc.shape[0] // tm
    row = lambda w: pl.BlockSpec((tm, w), lambda i: (i, 0))
    full = lambda a: pl.BlockSpec(a.shape, lambda i: (0,) * a.ndim)
    tab = pl.BlockSpec((tm, LANES), lambda i: (i % tab_tiles, 0))
    f32 = lambda w: jax.ShapeDtypeStruct((n, w), F32)
    if feature_major:
        seq = pos.shape[0]
        batch = n // seq
        t_shape = jax.ShapeDtypeStruct((batch, A_W, seq), F32)
        t_spec = pl.BlockSpec((1, A_W, tm), lambda i: (i // tab_tiles, 0, i % tab_tiles))
        attn_shape = (t_shape, t_shape, t_shape, jax.ShapeDtypeStruct((n, A_W), BF16),
                      jax.ShapeDtypeStruct((batch, tab_tiles, A_W, tm), BF16),
                      jax.ShapeDtypeStruct((n_tiles, 1, A_W), F32))
        attn_specs = (t_spec, t_spec, t_spec, row(A_W),
                      pl.BlockSpec((1, 1, A_W, tm), lambda i: (i // tab_tiles, i % tab_tiles, 0, 0)),
                      pl.BlockSpec((1, 1, A_W), lambda i: (i, 0, 0)))
    else:
        attn_shape = (f32(A_W), f32(A_W), f32(A_W))
        attn_specs = (row(A_W), row(A_W), row(A_W))
    out_shape = attn_shape + (f32(M_W), f32(M_W), f32(M_W), f32(M_W), f32(D_MODEL), f32(D_MODEL), f32(LANES))
    out_specs = attn_specs + (row(M_W), row(M_W), row(M_W), row(M_W), row(D_MODEL), row(D_MODEL), row(LANES))
    g_row = g_pre_mix.reshape(1, D_MODEL)
    return pl.pallas_call(
        functools.partial(_project_kernel, feature_major=feature_major),
        grid=(n_tiles,),
        in_specs=[row(D_MODEL), full(g_row), full(w_main), full(w_gate), full(b_if_row), tab, tab, tab],
        out_specs=out_specs,
        out_shape=out_shape,
        compiler_params=_compiler_params(("parallel",)),
        name="project",
    )(x2d, g_row, w_main, w_gate, b_if_row, rc, rs1, rs2)


def _select_topk(scores, block_idx, n_valid, axis):
    width = scores.shape[axis]
    sc = jnp.where(block_idx < n_valid, scores, NEG_INF)
    sel = jnp.zeros(scores.shape, jnp.bool_)
    for _ in range(MOBA_TOPK):
        mx = jnp.max(sc, axis=axis, keepdims=True)
        idx = jnp.min(jnp.where(sc == mx, block_idx, width), axis=axis, keepdims=True)
        hit = block_idx == idx
        sel = sel | hit
        sc = jnp.where(hit, NEG_INF, sc)
    return jnp.where(sel & (block_idx < n_valid), 1.0, 0.0)


def _moba_prompt_kernel(qt_ref, k_ref, vt_ref, kmean_ref, o_ref,
                        qtb_ref, sel_ref, s0_ref, s1_ref, m_ref, l_ref, acc_ref, ot_ref, *, n_blocks):
    i = pl.program_id(1)
    tq = MOBA_BLOCK
    scale = DH_A ** -0.5
    key = lax.broadcasted_iota(jnp.int32, (tq, tq), 0)
    qry = lax.broadcasted_iota(jnp.int32, (tq, tq), 1)
    causal = key <= qry
    feat = lax.broadcasted_iota(jnp.int32, (LANES, tq), 0)
    blk = lax.broadcasted_iota(jnp.int32, (n_blocks, tq), 0)
    n_pairs = A_W // LANES
    pair_feats = [slice(p * LANES, (p + 1) * LANES) for p in range(n_pairs)]

    heads = [(p, p * HEADS_PER_LANE_GROUP + hh) for p in range(n_pairs) for hh in range(HEADS_PER_LANE_GROUP)]

    def score(h, p, block):
        start = block * tq if isinstance(block, int) else pl.multiple_of(block * tq, tq)
        k_blk = k_ref[0, pl.ds(start, tq), pair_feats[p]]
        return jnp.dot(k_blk, qtb_ref[h], preferred_element_type=F32)

    def fold(h, p, block, s, own):
        m_old = m_ref[h]
        if own:
            s = jnp.where(causal, s, NEG_INF)
            m_new = jnp.maximum(m_old, jnp.max(s, axis=0, keepdims=True))
            m_shift = m_exp = m_new
        else:
            picked = sel_ref[h * n_blocks + block] > 0.5
            m_new = jnp.maximum(m_old, jnp.where(picked, jnp.max(s, axis=0, keepdims=True), NEG_INF))
            m_shift = jnp.where(m_new == NEG_INF, 0.0, m_new)
            m_exp = jnp.where(picked, m_shift, jnp.inf)
        alpha = jnp.exp2(m_old - m_shift)
        pe = jnp.exp2(s - m_exp)
        pv = jnp.dot(vt_ref[0, block, pair_feats[p], :], pe.astype(BF16), preferred_element_type=F32)
        m_ref[h] = m_new
        l_ref[h] = alpha * l_ref[h] + jnp.sum(pe, axis=0, keepdims=True)
        acc_ref[h] = alpha * acc_ref[h] + pv

    for p in range(n_pairs):
        qt_pair = qt_ref[0, pair_feats[p], :]
        kmean_pair = kmean_ref[0, :, pair_feats[p]]
        for hh in range(HEADS_PER_LANE_GROUP):
            h = p * HEADS_PER_LANE_GROUP + hh
            qt_head = jnp.where((feat // DH_A) == hh, qt_pair, 0.0)
            scores = jnp.dot(kmean_pair, qt_head, precision=lax.Precision.HIGHEST, preferred_element_type=F32)
            sel = _select_topk(scores, blk, i, axis=0)
            for j in range(n_blocks):
                sel_ref[h * n_blocks + j] = sel[j:j + 1, :]
            qtb_ref[h] = (qt_head * (scale * LOG2_E)).astype(BF16)
            m_ref[h] = jnp.full((1, tq), NEG_INF, F32)
            l_ref[h] = jnp.zeros((1, tq), F32)
            acc_ref[h] = jnp.zeros((LANES, tq), F32)

    bufs = (s0_ref, s1_ref)

    def stage(block, cur, own, score_next=True):
        for p, h in heads:
            if score_next:
                bufs[1 - cur][h] = score(h, p, block + 1)
            fold(h, p, block, bufs[cur][h], own)

    for p, h in heads:
        s0_ref[h] = score(h, p, 0)

    def body(jj, carry):
        stage(2 * jj, 0, own=False)
        stage(2 * jj + 1, 1, own=False)
        return carry

    lax.fori_loop(0, i // 2, body, 0)

    @pl.when(i % 2 == 0)
    def _():
        stage(i, 0, own=True, score_next=False)

    @pl.when(i % 2 == 1)
    def _():
        stage(i - 1, 0, own=False)
        stage(i, 1, own=True, score_next=False)

    for p in range(n_pairs):
        h0 = p * HEADS_PER_LANE_GROUP
        ot_ref[pair_feats[p], :] = jnp.where(feat < DH_A, acc_ref[h0] / l_ref[h0], acc_ref[h0 + 1] / l_ref[h0 + 1])
    o_ref[...] = ot_ref[...].T.astype(o_ref.dtype)


def _moba_prompt(qt, k_bf, vt_bf, kmean, batch, seq):
    n_blocks = seq // MOBA_BLOCK
    tq = MOBA_BLOCK
    k3 = k_bf.reshape(batch, seq, A_W)
    km3 = kmean.reshape(batch, n_blocks, A_W)
    return pl.pallas_call(
        functools.partial(_moba_prompt_kernel, n_blocks=n_blocks),
        grid=(batch, n_blocks),
        in_specs=[pl.BlockSpec((1, A_W, tq), lambda b, i: (b, 0, i)),
                  pl.BlockSpec((1, seq, A_W), lambda b, i: (b, 0, 0)),
                  pl.BlockSpec((1, n_blocks, A_W, tq), lambda b, i: (b, 0, 0, 0)),
                  pl.BlockSpec((1, n_blocks, A_W), lambda b, i: (b, 0, 0))],
        out_specs=pl.BlockSpec((tq, A_W), lambda b, i: (b * n_blocks + i, 0)),
        out_shape=jax.ShapeDtypeStruct((batch * seq, A_W), BF16),
        scratch_shapes=[pltpu.VMEM((H_A, LANES, tq), BF16),
                        pltpu.VMEM((H_A * n_blocks, 1, tq), F32),
                        pltpu.VMEM((H_A, tq, tq), F32),
                        pltpu.VMEM((H_A, tq, tq), F32),
                        pltpu.VMEM((H_A, 1, tq), F32),
                        pltpu.VMEM((H_A, 1, tq), F32),
                        pltpu.VMEM((H_A, LANES, tq), F32),
                        pltpu.VMEM((A_W, tq), F32)],
        compiler_params=_compiler_params(("parallel", "arbitrary")),
        name="moba_prompt",
    )(qt, k3, vt_bf, km3)


SAMPLE_RING_PAGES = 32
SAMPLE_BLOCKS_PER_ITER = 4


def _moba_sample_kernel(pt_ref, q_ref, kn_ref, vn_ref, ck_ref, cv_ref, o_ref,
                        ring_ref, ring_sem, qbd_ref, qbdt_ref, s_ref, score_ref, bmax_ref, acc_ref,
                        *, n_pages, n_blocks, page_rows, dec_seq):
    b = pl.program_id(0)
    n_seq = pl.num_programs(0)
    n_ring = SAMPLE_RING_PAGES
    stream_len = 2 * n_pages
    ppb = MOBA_BLOCK // page_rows
    n_q = H_A * dec_seq

    def ring_copy(pool_ref, page, slot):
        return pltpu.make_async_copy(pool_ref.at[page], ring_ref.at[slot], ring_sem.at[slot])

    def start_fetch(seq, pos, keys_only=False):
        slot = pos % n_ring
        if keys_only:
            ring_copy(ck_ref, pt_ref[seq, pos], slot).start()
            return

        @pl.when(pos < n_pages)
        def _():
            ring_copy(ck_ref, pt_ref[seq, pos], slot).start()

        @pl.when(pos >= n_pages)
        def _():
            ring_copy(cv_ref, pt_ref[seq, pos - n_pages], slot).start()

    def take(pos):
        slot = pos % n_ring
        ring_copy(ck_ref, 0, slot).wait()
        return slot

    def refill(pos):
        nxt = pos + n_ring

        @pl.when(nxt < stream_len)
        def _():
            start_fetch(b, nxt)

        @pl.when((nxt >= stream_len) & (b + 1 < n_seq))
        def _():
            start_fetch(b + 1, nxt - stream_len, keys_only=True)

    @pl.when(b == 0)
    def _():
        for pos in range(n_ring):
            start_fetch(0, pos, keys_only=True)

    scale = DH_A ** -0.5
    row = lax.broadcasted_iota(jnp.int32, (n_q, A_W), 0)
    lane_w = lax.broadcasted_iota(jnp.int32, (n_q, A_W), 1)
    head_diag = (row // dec_seq) == (lane_w // DH_A)
    lane = lax.broadcasted_iota(jnp.int32, (n_q, LANES), 1)
    tok = lax.broadcasted_iota(jnp.int32, (n_q, 1), 0) % dec_seq
    score_blk = lax.broadcasted_iota(jnp.int32, (LANES, LANES), 0)

    def page_cols(t):
        return slice(t * page_rows, (t + 1) * page_rows)

    q_rep = jnp.concatenate([q_ref[...]] * H_A, axis=0)
    qbd = jnp.where(head_diag, q_rep, 0.0)
    qbd_ref[...] = qbd
    qbdt_ref[...] = jnp.concatenate([qbd, jnp.zeros((LANES - n_q, A_W), F32)], axis=0).T
    score_ref[...] = jnp.full(score_ref.shape, NEG_INF, F32)
    bmax_ref[...] = jnp.full(bmax_ref.shape, NEG_INF, F32)

    unroll = SAMPLE_BLOCKS_PER_ITER
    pages_per_iter = unroll * ppb

    def k_blocks(it, carry):
        qb = (qbd_ref[...] * scale).astype(BF16)
        qbdt = qbdt_ref[...]
        pos0 = it * pages_per_iter
        slots = [take(pos0 + t) for t in range(pages_per_iter)]
        score, bmax = score_ref[...], bmax_ref[...]
        for u in range(unroll):
            jb = it * unroll + u
            kts = [ring_ref[slots[u * ppb + t]] for t in range(ppb)]
            kmean = jnp.sum(functools.reduce(lambda x, y: x + y, kts), axis=1, keepdims=True) * (1.0 / MOBA_BLOCK)
            sc = jnp.sum(qbdt * kmean, axis=0, keepdims=True)
            score = jnp.where(score_blk == jb, sc, score)
            s_pages = [jnp.dot(qb, kt.astype(BF16), preferred_element_type=F32) for kt in kts]
            for t in range(ppb):
                s_ref[jb, :, page_cols(t)] = s_pages[t]
            s_max = jnp.max(functools.reduce(jnp.maximum, s_pages), axis=1, keepdims=True)
            bmax = jnp.where(lane == jb, s_max, bmax)
        score_ref[...], bmax_ref[...] = score, bmax
        for t in range(pages_per_iter):
            refill(pos0 + t)
        return carry

    lax.fori_loop(0, n_blocks // unroll, k_blocks, 0)

    sel = _select_topk(score_ref[...].T[:n_q], lane, n_blocks, axis=1)
    qs = qbd * scale
    kn = kn_ref[...]
    own = []
    for t in range(dec_seq):
        so = jnp.sum(qs * kn[t:t + 1, :], axis=1, keepdims=True)
        own.append(jnp.where(t <= tok, so, NEG_INF))
    m = jnp.max(jnp.where(sel > 0.5, bmax_ref[...], NEG_INF), axis=1, keepdims=True)
    m = functools.reduce(jnp.maximum, own, m)
    p_own = jnp.zeros((n_q, LANES), F32)
    for t in range(dec_seq):
        p_own = jnp.where(lane == t, jnp.exp(own[t] - m), p_own)
    acc_ref[...] = jnp.zeros(acc_ref.shape, F32)

    def v_blocks(it, lsum):
        pos0 = n_pages + it * pages_per_iter
        slots = [take(pos0 + t) for t in range(pages_per_iter)]
        weights = []
        for u in range(unroll):
            jb = it * unroll + u
            picked = jnp.sum(jnp.where(lane == jb, sel, 0.0), axis=1, keepdims=True) > 0.5
            pj = jnp.exp(jnp.where(picked, s_ref[jb] - m, NEG_INF))
            lsum = lsum + pj
            weights.append(pj.astype(BF16))
        acc = acc_ref[...]
        for u in range(unroll):
            for t in range(ppb):
                vt = ring_ref[slots[u * ppb + t]].astype(BF16)
                acc = acc + lax.dot_general(weights[u][:, page_cols(t)], vt, NT_DIMS,
                                            preferred_element_type=F32)
        acc_ref[...] = acc
        for t in range(pages_per_iter):
            refill(pos0 + t)
        return lsum

    lsum = lax.fori_loop(0, n_blocks // unroll, v_blocks, jnp.zeros((n_q, MOBA_BLOCK), F32))

    acc = acc_ref[...]
    vn = vn_ref[...]
    for t in range(dec_seq):
        acc = acc + p_own[:, t:t + 1] * vn[t:t + 1, :]
    l = jnp.sum(lsum, axis=1, keepdims=True) + jnp.sum(p_own, axis=1, keepdims=True)
    out = jnp.where(head_diag, acc / l, 0.0)
    o_ref[...] = functools.reduce(
        lambda x, y: x + y, [out[h * dec_seq:(h + 1) * dec_seq, :] for h in range(H_A)])


def _moba_sample(qa, ka, va, cache_k, cache_v, page_table, dec_batch, dec_seq):
    n_pool, page_rows = cache_k.shape[0], cache_k.shape[1]
    n_pages = page_table.shape[1]
    past = n_pages * page_rows
    assert past % MOBA_BLOCK == 0 and MOBA_BLOCK % page_rows == 0, "cached rows must fill whole MoBA blocks"
    n_blocks = past // MOBA_BLOCK
    assert MOBA_TOPK <= n_blocks <= LANES
    n_ring = SAMPLE_RING_PAGES
    assert page_rows % LANES == 0 and n_ring <= n_pages and (2 * n_pages) % n_ring == 0
    assert n_blocks % SAMPLE_BLOCKS_PER_ITER == 0
    n_q = H_A * dec_seq
    assert n_q <= LANES
    ck = cache_k.transpose(0, 2, 3, 1).reshape(n_pool, A_W, page_rows)
    cv = cache_v.transpose(0, 2, 3, 1).reshape(n_pool, A_W, page_rows)

    tok_spec = pl.BlockSpec((dec_seq, A_W), lambda b, pt: (b, 0))
    pool_spec = pl.BlockSpec(memory_space=pl.ANY)
    grid_spec = pltpu.PrefetchScalarGridSpec(
        num_scalar_prefetch=1,
        grid=(dec_batch,),
        in_specs=[tok_spec, tok_spec, tok_spec, pool_spec, pool_spec],
        out_specs=tok_spec,
        scratch_shapes=[pltpu.VMEM((n_ring, A_W, page_rows), F32),
                        pltpu.SemaphoreType.DMA((n_ring,)),
                        pltpu.VMEM((n_q, A_W), F32),
                        pltpu.VMEM((A_W, LANES), F32),
                        pltpu.VMEM((n_blocks, n_q, MOBA_BLOCK), F32),
                        pltpu.VMEM((LANES, LANES), F32),
                        pltpu.VMEM((n_q, LANES), F32),
                        pltpu.VMEM((n_q, A_W), F32)],
    )
    return pl.pallas_call(
        functools.partial(_moba_sample_kernel, n_pages=n_pages, n_blocks=n_blocks,
                          page_rows=page_rows, dec_seq=dec_seq),
        grid_spec=grid_spec,
        out_shape=jax.ShapeDtypeStruct((dec_batch * dec_seq, A_W), F32),
        compiler_params=_compiler_params(("arbitrary",)),
        name="moba_sample",
    )(page_table, qa, ka, va, ck, cv)


MLSTM_SEQS_PER_STEP = 4


def _mlstm_kernel(q_ref, k_ref, v_ref, o_ref, gate_ref, c0_ref, n0_ref, m0_ref,
                  h_ref, c_ref, n_ref, m_ref, *, chunk):
    step = pl.program_id(1)
    lp = MLSTM_CHUNK
    n_seq = q_ref.shape[0]

    @pl.when(step == 0)
    def _():
        c_ref[...] = c0_ref[...]
        n_ref[...] = n0_ref[...]
        m_ref[...] = m0_ref[...]

    def pad_rows(a):
        if chunk == lp:
            return a
        return jnp.concatenate([a, jnp.zeros((lp - chunk, a.shape[1]), a.dtype)], axis=0)

    src = lax.broadcasted_iota(jnp.int32, (lp, lp), 0)
    tgt = lax.broadcasted_iota(jnp.int32, (lp, lp), 1)
    lane8 = lax.broadcasted_iota(jnp.int32, (N_GATE, lp), 1)
    chains = [(b, h) for b in range(n_seq) for h in range(H_M)]
    lanes = [slice(h * DH_M, (h + 1) * DH_M) for h in range(H_M)]
    last = slice(chunk - 1, chunk)

    q_all = [pad_rows(q_ref[b]) for b in range(n_seq)]
    k_all = [pad_rows(k_ref[b]) for b in range(n_seq)]
    v_all = [pad_rows(v_ref[b]) for b in range(n_seq)]
    qb = {(b, h): q_all[b][:, lanes[h]].astype(BF16) for b, h in chains}
    kb = {(b, h): k_all[b][:, lanes[h]].astype(BF16) for b, h in chains}

    s_kq = {ch: lax.dot_general(kb[ch], qb[ch], NT_DIMS, preferred_element_type=F32) for ch in chains}
    c_q = {(b, h): lax.dot_general(c_ref[b, h].astype(BF16), qb[b, h], NT_DIMS, preferred_element_type=F32)
           for b, h in chains}
    n_q = {(b, h): lax.dot_general(jnp.broadcast_to(n_ref[b, h], (N_GATE, DH_M)).astype(BF16), qb[b, h], NT_DIMS,
                                   preferred_element_type=F32)[0:1] for b, h in chains}

    g_row, b_row_all = [], []
    for b in range(n_seq):
        rows = pad_rows(gate_ref[b]).T[:N_GATE]
        csum = rows
        shift = 1
        while shift < lp:
            csum = csum + jnp.where(lane8 >= shift, pltpu.roll(csum, shift, 1), 0.0)
            shift *= 2
        g_row.append(rows)
        b_row_all.append(csum)

    v_t = {(b, h): v_all[b][:, lanes[h]].T for b, h in chains}
    w_d, w_prev, m_row, decay, w_s = {}, {}, {}, {}, {}
    for b, h in chains:
        i_row, b_row = g_row[b][h:h + 1], b_row_all[b][H_M + h:H_M + h + 1]
        m_prev = m_ref[b, h]
        u_src = jnp.broadcast_to(b_row - i_row, (lp, lp)).T
        log_d = jnp.where(src <= tgt, b_row - u_src, NEG_INF)
        m_row[b, h] = jnp.maximum(m_prev + b_row, jnp.max(log_d, axis=0, keepdims=True))
        w_d[b, h] = jnp.exp(log_d - m_row[b, h])
        w_prev[b, h] = jnp.exp(m_prev + b_row - m_row[b, h])
        m_new = m_row[b, h][:, last]
        b_last = b_row[:, last]
        decay[b, h] = jnp.exp(m_prev + b_last - m_new)
        w_s[b, h] = jnp.where(lane8[0:1] < chunk, jnp.exp(b_last - b_row + i_row - m_new), 0.0)
        m_ref[b, h] = m_new

    a = {ch: s_kq[ch] * w_d[ch] for ch in chains}
    v_a = {ch: jnp.dot(v_t[ch].astype(BF16), a[ch].astype(BF16), preferred_element_type=F32) for ch in chains}
    c_upd = {ch: jnp.dot((v_t[ch] * w_s[ch]).astype(BF16), kb[ch], preferred_element_type=F32) for ch in chains}
    n_upd = {ch: jnp.dot(jnp.broadcast_to(w_s[ch], (N_GATE, lp)).astype(BF16), kb[ch],
                         preferred_element_type=F32)[0:1] for ch in chains}

    for b, h in chains:
        num = v_a[b, h] + w_prev[b, h] * c_q[b, h]
        den = jnp.sum(a[b, h], axis=0, keepdims=True) + w_prev[b, h] * n_q[b, h]
        hc = (num / jnp.maximum(jnp.abs(den), jnp.exp(-m_row[b, h]))).T
        out = jax.nn.sigmoid(pad_rows(o_ref[b])[:, lanes[h]]) * hc
        h_ref[b, :, lanes[h]] = out[:chunk]
        c_ref[b, h] = decay[b, h] * c_ref[b, h] + c_upd[b, h]
        n_ref[b, h] = decay[b, h] * n_ref[b, h] + n_upd[b, h]


def _mlstm(qm, km, vm, om, gates, c0, n0, m0, batch, seq):
    chunk = math.gcd(seq, MLSTM_CHUNK)
    n_chunks = seq // chunk
    nb = MLSTM_SEQS_PER_STEP
    assert batch % nb == 0
    n0_4 = n0.reshape(batch, H_M, 1, DH_M)
    m0_4 = m0.reshape(batch, H_M, 1, 1)
    per_seq = lambda a: a.reshape(batch, seq, a.shape[-1])
    row = lambda w: pl.BlockSpec((nb, chunk, w), lambda g, s: (g, s, 0))
    state = lambda a: pl.BlockSpec((nb,) + a.shape[1:], lambda g, s: (g, 0, 0, 0))
    h, c, n, m = pl.pallas_call(
        functools.partial(_mlstm_kernel, chunk=chunk),
        grid=(batch // nb, n_chunks),
        in_specs=[row(M_W), row(M_W), row(M_W), row(M_W), row(LANES), state(c0), state(n0_4), state(m0_4)],
        out_specs=(row(M_W), state(c0), state(n0_4), state(m0_4)),
        out_shape=(jax.ShapeDtypeStruct((batch, seq, M_W), F32),
                   jax.ShapeDtypeStruct(c0.shape, F32),
                   jax.ShapeDtypeStruct(n0_4.shape, F32),
                   jax.ShapeDtypeStruct(m0_4.shape, F32)),
        compiler_params=_compiler_params(("parallel", "arbitrary")),
        name="mlstm",
    )(per_seq(qm), per_seq(km), per_seq(vm), per_seq(om), per_seq(gates), c0, n0_4, m0_4)
    return h.reshape(batch * seq, M_W), c, n.reshape(batch, H_M, DH_M), m.reshape(batch, H_M)


def _merge_kernel(x_ref, att_ref, ml_ref, ga_ref, gm_ref, wa_ref, wm_ref, wo_ref, g_ref, o_ref):
    ya = jnp.dot(att_ref[...].astype(BF16), wa_ref[...], preferred_element_type=F32)
    ym = jnp.dot(ml_ref[...].astype(BF16), wm_ref[...], preferred_element_type=F32)
    u = jax.nn.sigmoid(ga_ref[...]) * ya + jax.nn.sigmoid(gm_ref[...]) * ym
    r = jnp.dot(u.astype(BF16), wo_ref[...], preferred_element_type=F32)
    o_ref[...] = x_ref[...] + _rms(r, g_ref[...])


def _merge(x2d, att, ml, ga, gm, wa, wm, wo, g_post_mix):
    n = x2d.shape[0]
    tm = min(512, n)
    row = lambda w: pl.BlockSpec((tm, w), lambda i: (i, 0))
    full = lambda a: pl.BlockSpec(a.shape, lambda i: (0,) * a.ndim)
    g_row = g_post_mix.reshape(1, D_MODEL)
    return pl.pallas_call(
        _merge_kernel,
        grid=(n // tm,),
        in_specs=[row(D_MODEL), row(A_W), row(M_W), row(D_MODEL), row(D_MODEL),
                  full(wa), full(wm), full(wo), full(g_row)],
        out_specs=row(D_MODEL),
        out_shape=jax.ShapeDtypeStruct((n, D_MODEL), F32),
        compiler_params=_compiler_params(("parallel",)),
        name="merge",
    )(x2d, att, ml, ga, gm, wa, wm, wo, g_row)


MLP_FF_TILE = 1024


def _mlp_kernel(x_ref, g1_ref, wu_ref, wd_ref, g2_ref, o_ref):
    x = x_ref[...]
    h = _rms(x, g1_ref[...]).astype(BF16)
    n_chunks = D_FF // MLP_FF_TILE
    cols = lambda c: slice(c * MLP_FF_TILE, (c + 1) * MLP_FF_TILE)
    up = lambda c: jnp.dot(h, wu_ref[:, cols(c)], preferred_element_type=F32)
    pending = up(0)
    acc = None
    for c in range(n_chunks):
        act = jnp.square(jnp.maximum(pending, 0.0)).astype(BF16)
        if c + 1 < n_chunks:
            pending = up(c + 1)
        part = jnp.dot(act, wd_ref[cols(c), :], preferred_element_type=F32)
        acc = part if acc is None else acc + part
    o_ref[...] = x + _rms(acc, g2_ref[...])


def _mlp(x2d, g_pre_mlp, w_up, w_down, g_post_mlp):
    n = x2d.shape[0]
    tm = min(512, n)
    g1 = g_pre_mlp.reshape(1, D_MODEL)
    g2 = g_post_mlp.reshape(1, D_MODEL)
    gspec = pl.BlockSpec((1, D_MODEL), lambda i: (0, 0))
    resident = lambda a: pl.BlockSpec(a.shape, lambda i: (0, 0), pipeline_mode=pl.Buffered(1))
    return pl.pallas_call(
        _mlp_kernel,
        grid=(n // tm,),
        in_specs=[pl.BlockSpec((tm, D_MODEL), lambda i: (i, 0)), gspec, resident(w_up), resident(w_down), gspec],
        out_specs=pl.BlockSpec((tm, D_MODEL), lambda i: (i, 0)),
        out_shape=jax.ShapeDtypeStruct((n, D_MODEL), F32),
        compiler_params=_compiler_params(("parallel",)),
        name="mlp",
    )(x2d, g1, w_up, w_down, g2)


def kernel(x_prompt, x_sample, cache_k, cache_v, state_C, state_n, state_m, page_table, g_pre_mix, w_in, b_if,
           w_attn_br, w_mlstm_br, w_out, g_post_mix, g_pre_mlp, w_up, w_down, g_post_mlp):
    bp, sp, _ = x_prompt.shape
    db, ds, _ = x_sample.shape
    past = page_table.shape[1] * cache_k.shape[1]

    gate0 = 3 * A_W + 4 * M_W
    w_main = jnp.concatenate([w_in[:, :gate0], w_in[:, gate0 + N_GATE:]], axis=1).astype(BF16)
    w_gate = jnp.pad(w_in[:, gate0:gate0 + N_GATE], ((0, 0), (0, LANES - N_GATE))).astype(BF16)
    b_if_row = jnp.pad(b_if.astype(F32), (0, LANES - N_GATE)).reshape(1, LANES)
    wa, wm, wo = w_attn_br.astype(BF16), w_mlstm_br.astype(BF16), w_out.astype(BF16)
    wu, wd = w_up.astype(BF16), w_down.astype(BF16)

    def tail(x2d, att, ml, ga, gm):
        x1 = _merge(x2d, att, ml, ga, gm, wa, wm, wo, g_post_mix)
        return _mlp(x1, g_pre_mlp, wu, wd, g_post_mlp)

    xp = x_prompt.reshape(bp * sp, D_MODEL)
    (qt, kt, vt, k_bf, vt_bf, kmean, qm, km, vm, om, ga, gm, gates) = _project(
        xp, jnp.arange(sp, dtype=F32), g_pre_mix, w_main, w_gate, b_if_row, feature_major=True)
    att_p = _moba_prompt(qt, k_bf, vt_bf, kmean, bp, sp)
    ml_p, c_p, n_p, m_p = _mlstm(qm, km, vm, om, gates,
                                 jnp.zeros((bp, H_M, DH_M, DH_M), F32), jnp.zeros((bp, H_M, DH_M), F32),
                                 jnp.zeros((bp, H_M), F32), bp, sp)
    y_prompt = tail(xp, att_p, ml_p, ga, gm).reshape(bp, sp, D_MODEL)
    k_prompt = kt.reshape(bp, H_A, DH_A, sp).transpose(0, 3, 1, 2)
    v_prompt = vt.reshape(bp, H_A, DH_A, sp).transpose(0, 3, 1, 2)

    xs = x_sample.reshape(db * ds, D_MODEL)
    (qa, ka, va, qm, km, vm, om, ga, gm, gates) = _project(
        xs, past + jnp.arange(ds, dtype=F32), g_pre_mix, w_main, w_gate, b_if_row, feature_major=False)
    att_s = _moba_sample(qa, ka, va, cache_k, cache_v, page_table, db, ds)
    ml_s, c_s, n_s, m_s = _mlstm(qm, km, vm, om, gates, state_C.astype(F32), state_n.astype(F32),
                                 state_m.astype(F32), db, ds)
    y_sample = tail(xs, att_s, ml_s, ga, gm).reshape(db, ds, D_MODEL)
    k_sample = ka.reshape(db, ds, H_A, DH_A)
    v_sample = va.reshape(db, ds, H_A, DH_A)

    return (y_prompt, y_sample, k_prompt, v_prompt, c_p, n_p, m_p, k_sample, v_sample, c_s, n_s, m_s)
```

```python
import functools
import math

import jax
import jax.numpy as jnp
from jax import lax
from jax.experimental import pallas as pl
from jax.experimental.pallas import tpu as pltpu

F32 = jnp.float32
BF16 = jnp.bfloat16
NEG_INF = float("-inf")
LOG2_E = math.log2(math.e)

D_MODEL = 1024
H_A = 8
DH_A = 64
A_W = H_A * DH_A
MOBA_BLOCK = 256
MOBA_TOPK = 3
ROT_DIMS = DH_A // 4
ROPE_THETA = 500000.0
H_M = 4
DH_M = 128
M_W = H_M * DH_M
MLSTM_CHUNK = 128
D_FF = 4 * D_MODEL
RMS_EPS = 1e-6
N_GATE = 2 * H_M

LANES = 128
BF16_SUBLANES = 16
HEADS_PER_LANE_GROUP = LANES // DH_A
VMEM_LIMIT_BYTES = 52 * 1024 * 1024

NT_DIMS = (((1,), (1,)), ((), ()))
TN_DIMS = (((0,), (0,)), ((), ()))


def _compiler_params(semantics):
    return pltpu.CompilerParams(dimension_semantics=semantics, vmem_limit_bytes=VMEM_LIMIT_BYTES)


def _rms(x, g):
    return x * lax.rsqrt(jnp.mean(x * x, axis=-1, keepdims=True) + RMS_EPS) * g


def _log_sigmoid(x):
    return jnp.minimum(x, 0.0) - jnp.log1p(jnp.exp(-jnp.abs(x)))


PROJ_ROWS = MOBA_BLOCK


def _project_kernel(x_ref, g_ref, w_ref, wb_ref, wg_ref, bif_ref, rc_ref, rs1_ref, rs2_ref, *out_refs,
                    feature_major):
    if feature_major:
        (qt_ref, kt_ref, vt_ref, kb_ref, vtb_ref, kmean_ref,
         qm_ref, km_ref, vm_ref, om_ref, ga_ref, gm_ref, gates_ref) = out_refs
    else:
        qa_ref, ka_ref, va_ref, qm_ref, km_ref, vm_ref, om_ref, ga_ref, gm_ref, gates_ref = out_refs
    hb = _rms(x_ref[...], g_ref[...]).astype(BF16)

    def mm(c0, n):
        return jnp.dot(hb, w_ref[:, c0:c0 + n], preferred_element_type=F32)

    rc, rs1, rs2 = rc_ref[...], rs1_ref[...], rs2_ref[...]

    def rot(z):
        half = ROT_DIMS // 2
        outs = []
        for c in range(z.shape[1] // LANES):
            zc = z[:, c * LANES:(c + 1) * LANES]
            outs.append(zc * rc + pltpu.roll(zc, LANES - half, 1) * rs1 + pltpu.roll(zc, half, 1) * rs2)
        return jnp.concatenate(outs, axis=1)

    qa = rot(mm(0, A_W))
    ka = rot(mm(A_W, A_W))
    va = mm(2 * A_W, A_W)
    if feature_major:
        qt_ref[0] = qa.T
        kt_ref[0] = ka.T
        kb_ref[...] = ka.astype(BF16)
        kmean_ref[...] = jnp.mean(ka, axis=0, keepdims=True).reshape(1, 1, A_W)
        vt = va.T
        vt_ref[0] = vt
        ones_row = (lax.broadcasted_iota(jnp.int32, (LANES - DH_A, vt.shape[1]), 0) == 0).astype(BF16)
        for h in range(H_A):
            vtb_ref[0, 0, h] = jnp.concatenate([vt[h * DH_A:(h + 1) * DH_A].astype(BF16), ones_row], axis=0)
    else:
        qa_ref[...] = qa
        ka_ref[...] = ka
        va_ref[...] = va
    c0 = 3 * A_W
    qm_ref[...] = mm(c0, M_W).astype(qm_ref.dtype)
    km_ref[...] = (mm(c0 + M_W, M_W) * (DH_M ** -0.5)).astype(km_ref.dtype)
    vm_ref[...] = mm(c0 + 2 * M_W, M_W)
    om_ref[...] = mm(c0 + 3 * M_W, M_W)
    ga_ref[...] = jnp.dot(hb, wb_ref[:, :D_MODEL], preferred_element_type=F32)
    gm_ref[...] = jnp.dot(hb, wb_ref[:, D_MODEL:], preferred_element_type=F32)
    zg = jnp.dot(hb, wg_ref[...], preferred_element_type=F32) + bif_ref[...]
    lane = lax.broadcasted_iota(jnp.int32, zg.shape, 1)
    is_forget = (lane >= H_M) & (lane < N_GATE)
    gates_ref[...] = jnp.where(is_forget, _log_sigmoid(zg), zg)


def _rotary_tables(pos):
    half = ROT_DIMS // 2
    inv = ROPE_THETA ** (-jnp.arange(half, dtype=F32) * 2.0 / ROT_DIMS)
    ang = pos[:, None] * inv[None, :]
    cos, sin = jnp.cos(ang), jnp.sin(ang)
    n = pos.shape[0]
    pad = jnp.zeros((n, DH_A - ROT_DIMS), F32)
    c_head = jnp.concatenate([cos, cos, pad + 1.0], axis=1)
    s1_head = jnp.concatenate([-sin, jnp.zeros_like(sin), pad], axis=1)
    s2_head = jnp.concatenate([jnp.zeros_like(sin), sin, pad], axis=1)
    rep = lambda t: jnp.tile(t, (1, HEADS_PER_LANE_GROUP))
    return rep(c_head), rep(s1_head), rep(s2_head)


def _project(x2d, pos, g_pre_mix, w_heads, w_branch, w_gate, b_if_row, feature_major):
    n = x2d.shape[0]
    tm = PROJ_ROWS
    n_tiles = n // tm
    rc, rs1, rs2 = _rotary_tables(pos)
    if pos.shape[0] < tm:
        reps = tm // pos.shape[0]
        rc, rs1, rs2 = (jnp.tile(t, (reps, 1)) for t in (rc, rs1, rs2))
    tab_tiles = rc.shape[0] // tm
    row = lambda w: pl.BlockSpec((tm, w), lambda i: (i, 0))
    full = lambda a: pl.BlockSpec(a.shape, lambda i: (0,) * a.ndim)
    tab = pl.BlockSpec((tm, LANES), lambda i: (i % tab_tiles, 0))
    f32 = lambda w: jax.ShapeDtypeStruct((n, w), F32)
    if feature_major:
        seq = pos.shape[0]
        batch = n // seq
        t_shape = jax.ShapeDtypeStruct((batch, A_W, seq), F32)
        t_spec = pl.BlockSpec((1, A_W, tm), lambda i: (i // tab_tiles, 0, i % tab_tiles))
        attn_shape = (t_shape, t_shape, t_shape, jax.ShapeDtypeStruct((n, A_W), BF16),
                      jax.ShapeDtypeStruct((batch, tab_tiles, H_A, LANES, tm), BF16),
                      jax.ShapeDtypeStruct((n_tiles, 1, A_W), F32))
        attn_specs = (t_spec, t_spec, t_spec, row(A_W),
                      pl.BlockSpec((1, 1, H_A, LANES, tm), lambda i: (i // tab_tiles, i % tab_tiles, 0, 0, 0)),
                      pl.BlockSpec((1, 1, A_W), lambda i: (i, 0, 0)))
    else:
        attn_shape = (f32(A_W), f32(A_W), f32(A_W))
        attn_specs = (row(A_W), row(A_W), row(A_W))
    qk = jax.ShapeDtypeStruct((n, M_W), BF16 if feature_major else F32)
    out_shape = attn_shape + (qk, qk, f32(M_W), f32(M_W), f32(D_MODEL), f32(D_MODEL), f32(LANES))
    out_specs = attn_specs + (row(M_W), row(M_W), row(M_W), row(M_W), row(D_MODEL), row(D_MODEL), row(LANES))
    g_row = g_pre_mix.reshape(1, D_MODEL)
    return pl.pallas_call(
        functools.partial(_project_kernel, feature_major=feature_major),
        grid=(n_tiles,),
        in_specs=[row(D_MODEL), full(g_row), full(w_heads), full(w_branch), full(w_gate), full(b_if_row),
                  tab, tab, tab],
        out_specs=out_specs,
        out_shape=out_shape,
        compiler_params=_compiler_params(("parallel",)),
        name="project",
    )(x2d, g_row, w_heads, w_branch, w_gate, b_if_row, rc, rs1, rs2)


def _select_topk(scores, block_idx, n_valid, axis):
    width = scores.shape[axis]
    sc = jnp.where(block_idx < n_valid, scores, NEG_INF)
    sel = jnp.zeros(scores.shape, jnp.bool_)
    for _ in range(MOBA_TOPK):
        mx = jnp.max(sc, axis=axis, keepdims=True)
        idx = jnp.min(jnp.where(sc == mx, block_idx, width), axis=axis, keepdims=True)
        hit = block_idx == idx
        sel = sel | hit
        sc = jnp.where(hit, NEG_INF, sc)
    return jnp.where(sel & (block_idx < n_valid), 1.0, 0.0)


def _moba_prompt_kernel(qt_ref, k_ref, vt_ref, kmean_ref, o_ref,
                        qtb_ref, sel_ref, s0_ref, s1_ref, m_ref, acc_ref, ot_ref, *, n_blocks):
    i = pl.program_id(1)
    tq = MOBA_BLOCK
    scale = DH_A ** -0.5
    key = lax.broadcasted_iota(jnp.int32, (tq, tq), 0)
    qry = lax.broadcasted_iota(jnp.int32, (tq, tq), 1)
    causal = key <= qry
    feat = lax.broadcasted_iota(jnp.int32, (LANES, tq), 0)
    blk = lax.broadcasted_iota(jnp.int32, (n_blocks, tq), 0)
    n_pairs = A_W // LANES
    pair_feats = [slice(p * LANES, (p + 1) * LANES) for p in range(n_pairs)]

    heads = [(p, p * HEADS_PER_LANE_GROUP + hh) for p in range(n_pairs) for hh in range(HEADS_PER_LANE_GROUP)]

    def score(h, p, block):
        start = block * tq if isinstance(block, int) else pl.multiple_of(block * tq, tq)
        k_blk = k_ref[0, pl.ds(start, tq), pair_feats[p]]
        return jnp.dot(k_blk, qtb_ref[h], preferred_element_type=F32)

    def fold(h, p, block, s, own):
        m_old = m_ref[h]
        if own:
            s = jnp.where(causal, s, NEG_INF)
            m_new = jnp.maximum(m_old, jnp.max(s, axis=0, keepdims=True))
            m_shift = m_exp = m_new
        else:
            picked = sel_ref[h * n_blocks + block] > 0.5
            m_new = jnp.maximum(m_old, jnp.where(picked, jnp.max(s, axis=0, keepdims=True), NEG_INF))
            m_shift = jnp.where(m_new == NEG_INF, 0.0, m_new)
            m_exp = jnp.where(picked, m_shift, jnp.inf)
        alpha = jnp.exp2(m_old - m_shift)
        pe = jnp.exp2((s - m_exp).astype(BF16))
        pv = jnp.dot(vt_ref[0, block, h], pe, preferred_element_type=F32)
        m_ref[h] = m_new
        acc_ref[h] = alpha * acc_ref[h] + pv

    for p in range(n_pairs):
        qt_pair = qt_ref[0, pair_feats[p], :]
        kmean_pair = kmean_ref[0, :, pair_feats[p]]
        for hh in range(HEADS_PER_LANE_GROUP):
            h = p * HEADS_PER_LANE_GROUP + hh
            qt_head = jnp.where((feat // DH_A) == hh, qt_pair, 0.0)
            scores = jnp.dot(kmean_pair, qt_head, precision=lax.Precision.HIGHEST, preferred_element_type=F32)
            sel = _select_topk(scores, blk, i, axis=0)
            for j in range(n_blocks):
                sel_ref[h * n_blocks + j] = sel[j:j + 1, :]
            qtb_ref[h] = (qt_head * (scale * LOG2_E)).astype(BF16)
            m_ref[h] = jnp.full((1, tq), NEG_INF, F32)
            acc_ref[h] = jnp.zeros((LANES, tq), F32)

    bufs = (s0_ref, s1_ref)

    def stage(block, cur, own, score_next=True):
        for p, h in heads:
            if score_next:
                bufs[1 - cur][h] = score(h, p, block + 1)
            fold(h, p, block, bufs[cur][h], own)

    for p, h in heads:
        s0_ref[h] = score(h, p, 0)

    def body(jj, carry):
        stage(2 * jj, 0, own=False)
        stage(2 * jj + 1, 1, own=False)
        return carry

    lax.fori_loop(0, i // 2, body, 0)

    @pl.when(i % 2 == 0)
    def _():
        stage(i, 0, own=True, score_next=False)

    @pl.when(i % 2 == 1)
    def _():
        stage(i - 1, 0, own=False)
        stage(i, 1, own=True, score_next=False)

    for _, h in heads:
        ot_ref[h * DH_A:(h + 1) * DH_A, :] = acc_ref[h, :DH_A, :] / acc_ref[h, DH_A:DH_A + 1, :]
    o_ref[...] = ot_ref[...].T.astype(o_ref.dtype)


def _moba_prompt(qt, k_bf, vt_bf, kmean, batch, seq):
    n_blocks = seq // MOBA_BLOCK
    tq = MOBA_BLOCK
    k3 = k_bf.reshape(batch, seq, A_W)
    km3 = kmean.reshape(batch, n_blocks, A_W)
    return pl.pallas_call(
        functools.partial(_moba_prompt_kernel, n_blocks=n_blocks),
        grid=(batch, n_blocks),
        in_specs=[pl.BlockSpec((1, A_W, tq), lambda b, i: (b, 0, i)),
                  pl.BlockSpec((1, seq, A_W), lambda b, i: (b, 0, 0)),
                  pl.BlockSpec((1, n_blocks, H_A, LANES, tq), lambda b, i: (b, 0, 0, 0, 0)),
                  pl.BlockSpec((1, n_blocks, A_W), lambda b, i: (b, 0, 0))],
        out_specs=pl.BlockSpec((tq, A_W), lambda b, i: (b * n_blocks + i, 0)),
        out_shape=jax.ShapeDtypeStruct((batch * seq, A_W), BF16),
        scratch_shapes=[pltpu.VMEM((H_A, LANES, tq), BF16),
                        pltpu.VMEM((H_A * n_blocks, 1, tq), F32),
                        pltpu.VMEM((H_A, tq, tq), F32),
                        pltpu.VMEM((H_A, tq, tq), F32),
                        pltpu.VMEM((H_A, 1, tq), F32),
                        pltpu.VMEM((H_A, LANES, tq), F32),
                        pltpu.VMEM((A_W, tq), F32)],
        compiler_params=_compiler_params(("parallel", "arbitrary")),
        name="moba_prompt",
    )(qt, k3, vt_bf, km3)


SAMPLE_RING_PAGES = 32
SAMPLE_BLOCKS_PER_ITER = 4


def _moba_sample_kernel(pt_ref, q_ref, kn_ref, vn_ref, ck_ref, cv_ref, o_ref,
                        ring_ref, ring_sem, qbd_ref, qbdt_ref, s_ref, score_ref, bmax_ref, acc_ref,
                        *, n_pages, n_blocks, page_rows, dec_seq):
    b = pl.program_id(0)
    n_seq = pl.num_programs(0)
    n_ring = SAMPLE_RING_PAGES
    stream_len = 2 * n_pages
    ppb = MOBA_BLOCK // page_rows
    n_q = H_A * dec_seq

    def ring_copy(pool_ref, page, slot):
        return pltpu.make_async_copy(pool_ref.at[page], ring_ref.at[slot], ring_sem.at[slot])

    def start_fetch(seq, pos, keys_only=False):
        slot = pos % n_ring
        if keys_only:
            ring_copy(ck_ref, pt_ref[seq, pos], slot).start()
            return

        @pl.when(pos < n_pages)
        def _():
            ring_copy(ck_ref, pt_ref[seq, pos], slot).start()

        @pl.when(pos >= n_pages)
        def _():
            ring_copy(cv_ref, pt_ref[seq, pos - n_pages], slot).start()

    def take(pos):
        slot = pos % n_ring
        ring_copy(ck_ref, 0, slot).wait()
        return slot

    def refill(pos):
        nxt = pos + n_ring

        @pl.when(nxt < stream_len)
        def _():
            start_fetch(b, nxt)

        @pl.when((nxt >= stream_len) & (b + 1 < n_seq))
        def _():
            start_fetch(b + 1, nxt - stream_len, keys_only=True)

    @pl.when(b == 0)
    def _():
        for pos in range(n_ring):
            start_fetch(0, pos, keys_only=True)

    scale = DH_A ** -0.5
    row = lax.broadcasted_iota(jnp.int32, (n_q, A_W), 0)
    lane_w = lax.broadcasted_iota(jnp.int32, (n_q, A_W), 1)
    head_diag = (row // dec_seq) == (lane_w // DH_A)
    lane = lax.broadcasted_iota(jnp.int32, (n_q, LANES), 1)
    tok = lax.broadcasted_iota(jnp.int32, (n_q, 1), 0) % dec_seq
    score_blk = lax.broadcasted_iota(jnp.int32, (LANES, LANES), 0)

    def page_cols(t):
        return slice(t * page_rows, (t + 1) * page_rows)

    q_rep = jnp.concatenate([q_ref[...]] * H_A, axis=0)
    qbd = jnp.where(head_diag, q_rep, 0.0)
    qbd_ref[...] = qbd
    qbdt_ref[...] = jnp.concatenate([qbd, jnp.zeros((LANES - n_q, A_W), F32)], axis=0).T
    score_ref[...] = jnp.full(score_ref.shape, NEG_INF, F32)
    bmax_ref[...] = jnp.full(bmax_ref.shape, NEG_INF, F32)

    unroll = SAMPLE_BLOCKS_PER_ITER
    pages_per_iter = unroll * ppb

    def k_blocks(it, carry):
        qb = (qbd_ref[...] * scale).astype(BF16)
        qbdt = qbdt_ref[...]
        pos0 = it * pages_per_iter
        slots = [take(pos0 + t) for t in range(pages_per_iter)]
        score, bmax = score_ref[...], bmax_ref[...]
        for u in range(unroll):
            jb = it * unroll + u
            kts = [ring_ref[slots[u * ppb + t]] for t in range(ppb)]
            kmean = jnp.sum(functools.reduce(lambda x, y: x + y, kts), axis=1, keepdims=True) * (1.0 / MOBA_BLOCK)
            sc = jnp.sum(qbdt * kmean, axis=0, keepdims=True)
            score = jnp.where(score_blk == jb, sc, score)
            s_pages = [jnp.dot(qb, kt.astype(BF16), preferred_element_type=F32) for kt in kts]
            for t in range(ppb):
                s_ref[jb, :, page_cols(t)] = s_pages[t]
            s_max = jnp.max(functools.reduce(jnp.maximum, s_pages), axis=1, keepdims=True)
            bmax = jnp.where(lane == jb, s_max, bmax)
        score_ref[...], bmax_ref[...] = score, bmax
        for t in range(pages_per_iter):
            refill(pos0 + t)
        return carry

    lax.fori_loop(0, n_blocks // unroll, k_blocks, 0)

    sel = _select_topk(score_ref[...].T[:n_q], lane, n_blocks, axis=1)
    qs = qbd * scale
    kn = kn_ref[...]
    own = []
    for t in range(dec_seq):
        so = jnp.sum(qs * kn[t:t + 1, :], axis=1, keepdims=True)
        own.append(jnp.where(t <= tok, so, NEG_INF))
    m = jnp.max(jnp.where(sel > 0.5, bmax_ref[...], NEG_INF), axis=1, keepdims=True)
    m = functools.reduce(jnp.maximum, own, m)
    p_own = jnp.zeros((n_q, LANES), F32)
    for t in range(dec_seq):
        p_own = jnp.where(lane == t, jnp.exp(own[t] - m), p_own)
    acc_ref[...] = jnp.zeros(acc_ref.shape, F32)

    def v_blocks(it, lsum):
        pos0 = n_pages + it * pages_per_iter
        slots = [take(pos0 + t) for t in range(pages_per_iter)]
        weights = []
        for u in range(unroll):
            jb = it * unroll + u
            picked = jnp.sum(jnp.where(lane == jb, sel, 0.0), axis=1, keepdims=True) > 0.5
            pj = jnp.exp(jnp.where(picked, s_ref[jb] - m, NEG_INF))
            lsum = lsum + pj
            weights.append(pj.astype(BF16))
        acc = acc_ref[...]
        for u in range(unroll):
            for t in range(ppb):
                vt = ring_ref[slots[u * ppb + t]].astype(BF16)
                acc = acc + lax.dot_general(weights[u][:, page_cols(t)], vt, NT_DIMS,
                                            preferred_element_type=F32)
        acc_ref[...] = acc
        for t in range(pages_per_iter):
            refill(pos0 + t)
        return lsum

    lsum = lax.fori_loop(0, n_blocks // unroll, v_blocks, jnp.zeros((n_q, MOBA_BLOCK), F32))

    acc = acc_ref[...]
    vn = vn_ref[...]
    for t in range(dec_seq):
        acc = acc + p_own[:, t:t + 1] * vn[t:t + 1, :]
    l = jnp.sum(lsum, axis=1, keepdims=True) + jnp.sum(p_own, axis=1, keepdims=True)
    out = jnp.where(head_diag, acc / l, 0.0)
    o_ref[...] = functools.reduce(
        lambda x, y: x + y, [out[h * dec_seq:(h + 1) * dec_seq, :] for h in range(H_A)])


def _moba_sample(qa, ka, va, cache_k, cache_v, page_table, dec_batch, dec_seq):
    n_pool, page_rows = cache_k.shape[0], cache_k.shape[1]
    n_pages = page_table.shape[1]
    past = n_pages * page_rows
    assert past % MOBA_BLOCK == 0 and MOBA_BLOCK % page_rows == 0, "cached rows must fill whole MoBA blocks"
    n_blocks = past // MOBA_BLOCK
    assert MOBA_TOPK <= n_blocks <= LANES
    n_ring = SAMPLE_RING_PAGES
    assert page_rows % LANES == 0 and n_ring <= n_pages and (2 * n_pages) % n_ring == 0
    assert n_blocks % SAMPLE_BLOCKS_PER_ITER == 0
    n_q = H_A * dec_seq
    assert n_q <= LANES
    ck = cache_k.transpose(0, 2, 3, 1).reshape(n_pool, A_W, page_rows)
    cv = cache_v.transpose(0, 2, 3, 1).reshape(n_pool, A_W, page_rows)

    tok_spec = pl.BlockSpec((dec_seq, A_W), lambda b, pt: (b, 0))
    pool_spec = pl.BlockSpec(memory_space=pl.ANY)
    grid_spec = pltpu.PrefetchScalarGridSpec(
        num_scalar_prefetch=1,
        grid=(dec_batch,),
        in_specs=[tok_spec, tok_spec, tok_spec, pool_spec, pool_spec],
        out_specs=tok_spec,
        scratch_shapes=[pltpu.VMEM((n_ring, A_W, page_rows), F32),
                        pltpu.SemaphoreType.DMA((n_ring,)),
                        pltpu.VMEM((n_q, A_W), F32),
                        pltpu.VMEM((A_W, LANES), F32),
                        pltpu.VMEM((n_blocks, n_q, MOBA_BLOCK), F32),
                        pltpu.VMEM((LANES, LANES), F32),
                        pltpu.VMEM((n_q, LANES), F32),
                        pltpu.VMEM((n_q, A_W), F32)],
    )
    return pl.pallas_call(
        functools.partial(_moba_sample_kernel, n_pages=n_pages, n_blocks=n_blocks,
                          page_rows=page_rows, dec_seq=dec_seq),
        grid_spec=grid_spec,
        out_shape=jax.ShapeDtypeStruct((dec_batch * dec_seq, A_W), F32),
        compiler_params=_compiler_params(("arbitrary",)),
        name="moba_sample",
    )(page_table, qa, ka, va, ck, cv)


MLSTM_SEQS_PER_STEP = 4


def _mlstm_kernel(q_ref, k_ref, v_ref, o_ref, gate_ref, c0_ref, n0_ref, m0_ref,
                  h_ref, c_ref, n_ref, m_ref, *, chunk):
    step = pl.program_id(1)
    lp = MLSTM_CHUNK
    n_seq = q_ref.shape[0]

    @pl.when(step == 0)
    def _():
        c_ref[...] = c0_ref[...]
        n_ref[...] = n0_ref[...]
        m_ref[...] = m0_ref[...]

    def pad_rows(a):
        if chunk == lp:
            return a
        return jnp.concatenate([a, jnp.zeros((lp - chunk, a.shape[1]), a.dtype)], axis=0)

    src = lax.broadcasted_iota(jnp.int32, (lp, lp), 0)
    tgt = lax.broadcasted_iota(jnp.int32, (lp, lp), 1)
    lane8 = lax.broadcasted_iota(jnp.int32, (N_GATE, lp), 1)
    chains = [(b, h) for b in range(n_seq) for h in range(H_M)]
    lanes = [slice(h * DH_M, (h + 1) * DH_M) for h in range(H_M)]
    last = slice(chunk - 1, chunk)

    q_all = [pad_rows(q_ref[b]) for b in range(n_seq)]
    k_all = [pad_rows(k_ref[b]) for b in range(n_seq)]
    v_all = [pad_rows(v_ref[b]) for b in range(n_seq)]
    qb = {(b, h): q_all[b][:, lanes[h]].astype(BF16) for b, h in chains}
    kb = {(b, h): k_all[b][:, lanes[h]].astype(BF16) for b, h in chains}

    s_kq = {ch: lax.dot_general(kb[ch], qb[ch], NT_DIMS, preferred_element_type=F32) for ch in chains}
    c_q = {(b, h): lax.dot_general(c_ref[b, h].astype(BF16), qb[b, h], NT_DIMS, preferred_element_type=F32)
           for b, h in chains}
    n_q = {(b, h): lax.dot_general(jnp.broadcast_to(n_ref[b, h], (N_GATE, DH_M)).astype(BF16), qb[b, h], NT_DIMS,
                                   preferred_element_type=F32)[0:1] for b, h in chains}

    g_row, b_row_all = [], []
    for b in range(n_seq):
        rows = pad_rows(gate_ref[b]).T[:N_GATE]
        csum = rows
        shift = 1
        while shift < lp:
            csum = csum + jnp.where(lane8 >= shift, pltpu.roll(csum, shift, 1), 0.0)
            shift *= 2
        g_row.append(rows)
        b_row_all.append(csum)

    v_t = {(b, h): v_all[b][:, lanes[h]].T for b, h in chains}
    w_d, w_prev, m_row, decay, w_s = {}, {}, {}, {}, {}
    for b, h in chains:
        i_row, b_row = g_row[b][h:h + 1], b_row_all[b][H_M + h:H_M + h + 1]
        m_prev = m_ref[b, h]
        u_src = jnp.broadcast_to(b_row - i_row, (lp, lp)).T
        log_d = jnp.where(src <= tgt, b_row - u_src, NEG_INF)
        m_row[b, h] = jnp.maximum(m_prev + b_row, jnp.max(log_d, axis=0, keepdims=True))
        w_d[b, h] = jnp.exp(log_d - m_row[b, h])
        w_prev[b, h] = jnp.exp(m_prev + b_row - m_row[b, h])
        m_new = m_row[b, h][:, last]
        b_last = b_row[:, last]
        decay[b, h] = jnp.exp(m_prev + b_last - m_new)
        w_s[b, h] = jnp.where(lane8[0:1] < chunk, jnp.exp(b_last - b_row + i_row - m_new), 0.0)
        m_ref[b, h] = m_new

    a = {ch: s_kq[ch] * w_d[ch] for ch in chains}
    v_a = {ch: jnp.dot(v_t[ch].astype(BF16), a[ch].astype(BF16), preferred_element_type=F32) for ch in chains}
    c_upd = {ch: jnp.dot((v_t[ch] * w_s[ch]).astype(BF16), kb[ch], preferred_element_type=F32) for ch in chains}
    n_upd = {ch: jnp.dot(jnp.broadcast_to(w_s[ch], (N_GATE, lp)).astype(BF16), kb[ch],
                         preferred_element_type=F32)[0:1] for ch in chains}

    for b, h in chains:
        num = v_a[b, h] + w_prev[b, h] * c_q[b, h]
        den = jnp.sum(a[b, h], axis=0, keepdims=True) + w_prev[b, h] * n_q[b, h]
        hc = (num / jnp.maximum(jnp.abs(den), jnp.exp(-m_row[b, h]))).T
        out = jax.nn.sigmoid(pad_rows(o_ref[b])[:, lanes[h]]) * hc
        h_ref[b, :, lanes[h]] = out[:chunk].astype(h_ref.dtype)
        c_ref[b, h] = decay[b, h] * c_ref[b, h] + c_upd[b, h]
        n_ref[b, h] = decay[b, h] * n_ref[b, h] + n_upd[b, h]


def _mlstm(qm, km, vm, om, gates, c0, n0, m0, batch, seq):
    chunk = math.gcd(seq, MLSTM_CHUNK)
    n_chunks = seq // chunk
    nb = MLSTM_SEQS_PER_STEP
    assert batch % nb == 0
    h_dtype = BF16 if chunk % BF16_SUBLANES == 0 else F32
    n0_4 = n0.reshape(batch, H_M, 1, DH_M)
    m0_4 = m0.reshape(batch, H_M, 1, 1)
    per_seq = lambda a: a.reshape(batch, seq, a.shape[-1])
    row = lambda w: pl.BlockSpec((nb, chunk, w), lambda g, s: (g, s, 0))
    state = lambda a: pl.BlockSpec((nb,) + a.shape[1:], lambda g, s: (g, 0, 0, 0))
    h, c, n, m = pl.pallas_call(
        functools.partial(_mlstm_kernel, chunk=chunk),
        grid=(batch // nb, n_chunks),
        in_specs=[row(M_W), row(M_W), row(M_W), row(M_W), row(LANES), state(c0), state(n0_4), state(m0_4)],
        out_specs=(row(M_W), state(c0), state(n0_4), state(m0_4)),
        out_shape=(jax.ShapeDtypeStruct((batch, seq, M_W), h_dtype),
                   jax.ShapeDtypeStruct(c0.shape, F32),
                   jax.ShapeDtypeStruct(n0_4.shape, F32),
                   jax.ShapeDtypeStruct(m0_4.shape, F32)),
        compiler_params=_compiler_params(("parallel", "arbitrary")),
        name="mlstm",
    )(per_seq(qm), per_seq(km), per_seq(vm), per_seq(om), per_seq(gates), c0, n0_4, m0_4)
    return h.reshape(batch * seq, M_W), c, n.reshape(batch, H_M, DH_M), m.reshape(batch, H_M)


def _merge_kernel(x_ref, att_ref, ml_ref, ga_ref, gm_ref, wa_ref, wm_ref, wo_ref, g_ref, o_ref):
    ya = jnp.dot(att_ref[...].astype(BF16), wa_ref[...], preferred_element_type=F32)
    ym = jnp.dot(ml_ref[...].astype(BF16), wm_ref[...], preferred_element_type=F32)
    u = jax.nn.sigmoid(ga_ref[...]) * ya + jax.nn.sigmoid(gm_ref[...]) * ym
    r = jnp.dot(u.astype(BF16), wo_ref[...], preferred_element_type=F32)
    o_ref[...] = x_ref[...] + _rms(r, g_ref[...])


def _merge(x2d, att, ml, ga, gm, wa, wm, wo, g_post_mix):
    n = x2d.shape[0]
    tm = min(512, n)
    row = lambda w: pl.BlockSpec((tm, w), lambda i: (i, 0))
    full = lambda a: pl.BlockSpec(a.shape, lambda i: (0,) * a.ndim)
    g_row = g_post_mix.reshape(1, D_MODEL)
    return pl.pallas_call(
        _merge_kernel,
        grid=(n // tm,),
        in_specs=[row(D_MODEL), row(A_W), row(M_W), row(D_MODEL), row(D_MODEL),
                  full(wa), full(wm), full(wo), full(g_row)],
        out_specs=row(D_MODEL),
        out_shape=jax.ShapeDtypeStruct((n, D_MODEL), F32),
        compiler_params=_compiler_params(("parallel",)),
        name="merge",
    )(x2d, att, ml, ga, gm, wa, wm, wo, g_row)


MLP_FF_TILE = 1024


def _mlp_kernel(x_ref, g1_ref, wu_ref, wd_ref, g2_ref, o_ref):
    x = x_ref[...]
    h = _rms(x, g1_ref[...]).astype(BF16)
    n_chunks = D_FF // MLP_FF_TILE
    cols = lambda c: slice(c * MLP_FF_TILE, (c + 1) * MLP_FF_TILE)
    up = lambda c: jnp.dot(h, wu_ref[:, cols(c)], preferred_element_type=F32)
    pending = up(0)
    acc = None
    for c in range(n_chunks):
        act = jnp.square(jnp.maximum(pending, 0.0)).astype(BF16)
        if c + 1 < n_chunks:
            pending = up(c + 1)
        part = jnp.dot(act, wd_ref[cols(c), :], preferred_element_type=F32)
        acc = part if acc is None else acc + part
    o_ref[...] = x + _rms(acc, g2_ref[...])


def _mlp(x2d, g_pre_mlp, w_up, w_down, g_post_mlp):
    n = x2d.shape[0]
    tm = min(512, n)
    g1 = g_pre_mlp.reshape(1, D_MODEL)
    g2 = g_post_mlp.reshape(1, D_MODEL)
    gspec = pl.BlockSpec((1, D_MODEL), lambda i: (0, 0))
    resident = lambda a: pl.BlockSpec(a.shape, lambda i: (0, 0), pipeline_mode=pl.Buffered(1))
    return pl.pallas_call(
        _mlp_kernel,
        grid=(n // tm,),
        in_specs=[pl.BlockSpec((tm, D_MODEL), lambda i: (i, 0)), gspec, resident(w_up), resident(w_down), gspec],
        out_specs=pl.BlockSpec((tm, D_MODEL), lambda i: (i, 0)),
        out_shape=jax.ShapeDtypeStruct((n, D_MODEL), F32),
        compiler_params=_compiler_params(("parallel",)),
        name="mlp",
    )(x2d, g1, w_up, w_down, g2)


def kernel(x_prompt, x_sample, cache_k, cache_v, state_C, state_n, state_m, page_table, g_pre_mix, w_in, b_if,
           w_attn_br, w_mlstm_br, w_out, g_post_mix, g_pre_mlp, w_up, w_down, g_post_mlp):
    bp, sp, _ = x_prompt.shape
    db, ds, _ = x_sample.shape
    past = page_table.shape[1] * cache_k.shape[1]

    gate0 = 3 * A_W + 4 * M_W
    w_heads = w_in[:, :gate0].astype(BF16)
    w_branch = w_in[:, gate0 + N_GATE:].astype(BF16)
    w_gate = jnp.pad(w_in[:, gate0:gate0 + N_GATE], ((0, 0), (0, LANES - N_GATE))).astype(BF16)
    b_if_row = jnp.pad(b_if.astype(F32), (0, LANES - N_GATE)).reshape(1, LANES)
    wa, wm, wo = w_attn_br.astype(BF16), w_mlstm_br.astype(BF16), w_out.astype(BF16)
    wu, wd = w_up.astype(BF16), w_down.astype(BF16)

    def tail(x2d, att, ml, ga, gm):
        x1 = _merge(x2d, att, ml, ga, gm, wa, wm, wo, g_post_mix)
        return _mlp(x1, g_pre_mlp, wu, wd, g_post_mlp)

    xp = x_prompt.reshape(bp * sp, D_MODEL)
    (qt, kt, vt, k_bf, vt_bf, kmean, qm, km, vm, om, ga, gm, gates) = _project(
        xp, jnp.arange(sp, dtype=F32), g_pre_mix, w_heads, w_branch, w_gate, b_if_row, feature_major=True)
    att_p = _moba_prompt(qt, k_bf, vt_bf, kmean, bp, sp)
    ml_p, c_p, n_p, m_p = _mlstm(qm, km, vm, om, gates,
                                 jnp.zeros((bp, H_M, DH_M, DH_M), F32), jnp.zeros((bp, H_M, DH_M), F32),
                                 jnp.zeros((bp, H_M), F32), bp, sp)
    y_prompt = tail(xp, att_p, ml_p, ga, gm).reshape(bp, sp, D_MODEL)
    k_prompt = kt.reshape(bp, H_A, DH_A, sp).transpose(0, 3, 1, 2)
    v_prompt = vt.reshape(bp, H_A, DH_A, sp).transpose(0, 3, 1, 2)

    xs = x_sample.reshape(db * ds, D_MODEL)
    (qa, ka, va, qm, km, vm, om, ga, gm, gates) = _project(
        xs, past + jnp.arange(ds, dtype=F32), g_pre_mix, w_heads, w_branch, w_gate, b_if_row, feature_major=False)
    att_s = _moba_sample(qa, ka, va, cache_k, cache_v, page_table, db, ds)
    ml_s, c_s, n_s, m_s = _mlstm(qm, km, vm, om, gates, state_C.astype(F32), state_n.astype(F32),
                                 state_m.astype(F32), db, ds)
    y_sample = tail(xs, att_s, ml_s, ga, gm).reshape(db, ds, D_MODEL)
    k_sample = ka.reshape(db, ds, H_A, DH_A)
    v_sample = va.reshape(db, ds, H_A, DH_A)

    return (y_prompt, y_sample, k_prompt, v_prompt, c_p, n_p, m_p, k_sample, v_sample, c_s, n_s, m_s)
```

```python
import functools
import math

import jax
import jax.numpy as jnp
from jax import lax
from jax.experimental import pallas as pl
from jax.experimental.pallas import tpu as pltpu

F32 = jnp.float32
BF16 = jnp.bfloat16
NEG_INF = float("-inf")
LOG2_E = math.log2(math.e)

D_MODEL = 1024
H_A = 8
DH_A = 64
A_W = H_A * DH_A
MOBA_BLOCK = 256
MOBA_TOPK = 3
ROT_DIMS = DH_A // 4
ROPE_THETA = 500000.0
H_M = 4
DH_M = 128
M_W = H_M * DH_M
MLSTM_CHUNK = 128
D_FF = 4 * D_MODEL
RMS_EPS = 1e-6
N_GATE = 2 * H_M

LANES = 128
BF16_SUBLANES = 16
HEADS_PER_LANE_GROUP = LANES // DH_A
VMEM_LIMIT_BYTES = 52 * 1024 * 1024

NT_DIMS = (((1,), (1,)), ((), ()))
TN_DIMS = (((0,), (0,)), ((), ()))


def _compiler_params(semantics):
    return pltpu.CompilerParams(dimension_semantics=semantics, vmem_limit_bytes=VMEM_LIMIT_BYTES)


def _rms(x, g):
    return x * lax.rsqrt(jnp.mean(x * x, axis=-1, keepdims=True) + RMS_EPS) * g


def _log_sigmoid(x):
    return jnp.minimum(x, 0.0) - jnp.log1p(jnp.exp(-jnp.abs(x)))


PROJ_ROWS = MOBA_BLOCK


def _project_kernel(x_ref, g_ref, w_ref, wb_ref, wg_ref, bif_ref, rc_ref, rs1_ref, rs2_ref, *out_refs,
                    feature_major):
    if feature_major:
        (qt_ref, kt_ref, vt_ref, kb_ref, vtb_ref, kmean_ref,
         qm_ref, km_ref, vm_ref, om_ref, ga_ref, gm_ref, gates_ref) = out_refs
    else:
        qa_ref, ka_ref, va_ref, qm_ref, km_ref, vm_ref, om_ref, ga_ref, gm_ref, gates_ref = out_refs
    hb = _rms(x_ref[...], g_ref[...]).astype(BF16)

    def mm(c0, n):
        return jnp.dot(hb, w_ref[:, c0:c0 + n], preferred_element_type=F32)

    rc, rs1, rs2 = rc_ref[...], rs1_ref[...], rs2_ref[...]

    def rot(z):
        half = ROT_DIMS // 2
        outs = []
        for c in range(z.shape[1] // LANES):
            zc = z[:, c * LANES:(c + 1) * LANES]
            outs.append(zc * rc + pltpu.roll(zc, LANES - half, 1) * rs1 + pltpu.roll(zc, half, 1) * rs2)
        return jnp.concatenate(outs, axis=1)

    qa = rot(mm(0, A_W))
    ka = rot(mm(A_W, A_W))
    va = mm(2 * A_W, A_W)
    if feature_major:
        qt_ref[0] = qa.T
        kt_ref[0] = ka.T
        kb_ref[...] = ka.astype(BF16)
        kmean_ref[...] = jnp.mean(ka, axis=0, keepdims=True).reshape(1, 1, A_W)
        vt = va.T
        vt_ref[0] = vt
        ones_row = (lax.broadcasted_iota(jnp.int32, (LANES - DH_A, vt.shape[1]), 0) == 0).astype(BF16)
        for h in range(H_A):
            vtb_ref[0, 0, h] = jnp.concatenate([vt[h * DH_A:(h + 1) * DH_A].astype(BF16), ones_row], axis=0)
    else:
        qa_ref[...] = qa
        ka_ref[...] = ka
        va_ref[...] = va
    c0 = 3 * A_W
    qm_ref[...] = mm(c0, M_W).astype(qm_ref.dtype)
    km_ref[...] = (mm(c0 + M_W, M_W) * (DH_M ** -0.5)).astype(km_ref.dtype)
    vm_ref[...] = mm(c0 + 2 * M_W, M_W)
    om_ref[...] = mm(c0 + 3 * M_W, M_W)
    ga_ref[...] = jax.nn.sigmoid(jnp.dot(hb, wb_ref[:, :D_MODEL], preferred_element_type=F32)).astype(ga_ref.dtype)
    gm_ref[...] = jax.nn.sigmoid(jnp.dot(hb, wb_ref[:, D_MODEL:], preferred_element_type=F32)).astype(gm_ref.dtype)
    zg = jnp.dot(hb, wg_ref[...], preferred_element_type=F32) + bif_ref[...]
    lane = lax.broadcasted_iota(jnp.int32, zg.shape, 1)
    is_forget = (lane >= H_M) & (lane < N_GATE)
    gates_ref[...] = jnp.where(is_forget, _log_sigmoid(zg), zg)


def _rotary_tables(pos):
    half = ROT_DIMS // 2
    inv = ROPE_THETA ** (-jnp.arange(half, dtype=F32) * 2.0 / ROT_DIMS)
    ang = pos[:, None] * inv[None, :]
    cos, sin = jnp.cos(ang), jnp.sin(ang)
    n = pos.shape[0]
    pad = jnp.zeros((n, DH_A - ROT_DIMS), F32)
    c_head = jnp.concatenate([cos, cos, pad + 1.0], axis=1)
    s1_head = jnp.concatenate([-sin, jnp.zeros_like(sin), pad], axis=1)
    s2_head = jnp.concatenate([jnp.zeros_like(sin), sin, pad], axis=1)
    rep = lambda t: jnp.tile(t, (1, HEADS_PER_LANE_GROUP))
    return rep(c_head), rep(s1_head), rep(s2_head)


def _project(x2d, pos, g_pre_mix, w_in_bf, w_branch, w_gate, b_if_row, feature_major):
    n = x2d.shape[0]
    tm = PROJ_ROWS
    n_tiles = n // tm
    rc, rs1, rs2 = _rotary_tables(pos)
    if pos.shape[0] < tm:
        reps = tm // pos.shape[0]
        rc, rs1, rs2 = (jnp.tile(t, (reps, 1)) for t in (rc, rs1, rs2))
    tab_tiles = rc.shape[0] // tm
    row = lambda w: pl.BlockSpec((tm, w), lambda i: (i, 0))
    full = lambda a: pl.BlockSpec(a.shape, lambda i: (0,) * a.ndim)
    tab = pl.BlockSpec((tm, LANES), lambda i: (i % tab_tiles, 0))
    f32 = lambda w: jax.ShapeDtypeStruct((n, w), F32)
    if feature_major:
        seq = pos.shape[0]
        batch = n // seq
        t_shape = jax.ShapeDtypeStruct((batch, A_W, seq), F32)
        t_spec = pl.BlockSpec((1, A_W, tm), lambda i: (i // tab_tiles, 0, i % tab_tiles))
        attn_shape = (t_shape, t_shape, t_shape, jax.ShapeDtypeStruct((n, A_W), BF16),
                      jax.ShapeDtypeStruct((batch, tab_tiles, H_A, LANES, tm), BF16),
                      jax.ShapeDtypeStruct((n_tiles, 1, A_W), F32))
        attn_specs = (t_spec, t_spec, t_spec, row(A_W),
                      pl.BlockSpec((1, 1, H_A, LANES, tm), lambda i: (i // tab_tiles, i % tab_tiles, 0, 0, 0)),
                      pl.BlockSpec((1, 1, A_W), lambda i: (i, 0, 0)))
    else:
        attn_shape = (f32(A_W), f32(A_W), f32(A_W))
        attn_specs = (row(A_W), row(A_W), row(A_W))
    qk = jax.ShapeDtypeStruct((n, M_W), BF16 if feature_major else F32)
    gate = jax.ShapeDtypeStruct((n, D_MODEL), BF16)
    out_shape = attn_shape + (qk, qk, f32(M_W), f32(M_W), gate, gate, f32(LANES))
    out_specs = attn_specs + (row(M_W), row(M_W), row(M_W), row(M_W), row(D_MODEL), row(D_MODEL), row(LANES))
    g_row = g_pre_mix.reshape(1, D_MODEL)
    return pl.pallas_call(
        functools.partial(_project_kernel, feature_major=feature_major),
        grid=(n_tiles,),
        in_specs=[row(D_MODEL), full(g_row), pl.BlockSpec((D_MODEL, 3 * A_W + 4 * M_W), lambda i: (0, 0)),
                  full(w_branch), full(w_gate), full(b_if_row), tab, tab, tab],
        out_specs=out_specs,
        out_shape=out_shape,
        compiler_params=_compiler_params(("parallel",)),
        name="project",
    )(x2d, g_row, w_in_bf, w_branch, w_gate, b_if_row, rc, rs1, rs2)


def _select_topk(scores, block_idx, n_valid, axis):
    width = scores.shape[axis]
    sc = jnp.where(block_idx < n_valid, scores, NEG_INF)
    sel = jnp.zeros(scores.shape, jnp.bool_)
    for _ in range(MOBA_TOPK):
        mx = jnp.max(sc, axis=axis, keepdims=True)
        idx = jnp.min(jnp.where(sc == mx, block_idx, width), axis=axis, keepdims=True)
        hit = block_idx == idx
        sel = sel | hit
        sc = jnp.where(hit, NEG_INF, sc)
    return jnp.where(sel & (block_idx < n_valid), 1.0, 0.0)


def _moba_prompt_kernel(qt_ref, k_ref, vt_ref, kmean_ref, o_ref,
                        qtb_ref, sel_ref, s0_ref, s1_ref, m_ref, acc_ref, ot_ref, *, n_blocks):
    i = pl.program_id(1)
    tq = MOBA_BLOCK
    scale = DH_A ** -0.5
    key = lax.broadcasted_iota(jnp.int32, (tq, tq), 0)
    qry = lax.broadcasted_iota(jnp.int32, (tq, tq), 1)
    causal = key <= qry
    feat = lax.broadcasted_iota(jnp.int32, (LANES, tq), 0)
    blk = lax.broadcasted_iota(jnp.int32, (n_blocks, tq), 0)
    n_pairs = A_W // LANES
    pair_feats = [slice(p * LANES, (p + 1) * LANES) for p in range(n_pairs)]

    heads = [(p, p * HEADS_PER_LANE_GROUP + hh) for p in range(n_pairs) for hh in range(HEADS_PER_LANE_GROUP)]

    def score(h, p, block):
        start = block * tq if isinstance(block, int) else pl.multiple_of(block * tq, tq)
        k_blk = k_ref[0, pl.ds(start, tq), pair_feats[p]]
        return jnp.dot(k_blk, qtb_ref[h], preferred_element_type=F32)

    def fold(h, p, block, s, own):
        m_old = m_ref[h]
        if own:
            s = jnp.where(causal, s, NEG_INF)
            m_new = jnp.maximum(m_old, jnp.max(s, axis=0, keepdims=True))
            m_shift = m_exp = m_new
        else:
            picked = sel_ref[h * n_blocks + block] > 0.5
            m_new = jnp.maximum(m_old, jnp.where(picked, jnp.max(s, axis=0, keepdims=True), NEG_INF))
            m_shift = jnp.where(m_new == NEG_INF, 0.0, m_new)
            m_exp = jnp.where(picked, m_shift, jnp.inf)
        alpha = jnp.exp2(m_old - m_shift)
        pe = jnp.exp2((s - m_exp).astype(BF16))
        pv = jnp.dot(vt_ref[0, block, h], pe, preferred_element_type=F32)
        m_ref[h] = m_new
        acc_ref[h] = alpha * acc_ref[h] + pv

    for p in range(n_pairs):
        qt_pair = qt_ref[0, pair_feats[p], :]
        kmean_pair = kmean_ref[0, :, pair_feats[p]]
        for hh in range(HEADS_PER_LANE_GROUP):
            h = p * HEADS_PER_LANE_GROUP + hh
            qt_head = jnp.where((feat // DH_A) == hh, qt_pair, 0.0)
            scores = jnp.dot(kmean_pair, qt_head, precision=lax.Precision.HIGHEST, preferred_element_type=F32)
            sel = _select_topk(scores, blk, i, axis=0)
            for j in range(n_blocks):
                sel_ref[h * n_blocks + j] = sel[j:j + 1, :]
            qtb_ref[h] = (qt_head * (scale * LOG2_E)).astype(BF16)
            m_ref[h] = jnp.full((1, tq), NEG_INF, F32)
            acc_ref[h] = jnp.zeros((LANES, tq), F32)

    bufs = (s0_ref, s1_ref)

    def stage(block, cur, own, score_next=True):
        for p, h in heads:
            if score_next:
                bufs[1 - cur][h] = score(h, p, block + 1)
            fold(h, p, block, bufs[cur][h], own)

    for p, h in heads:
        s0_ref[h] = score(h, p, 0)

    def body(jj, carry):
        stage(2 * jj, 0, own=False)
        stage(2 * jj + 1, 1, own=False)
        return carry

    lax.fori_loop(0, i // 2, body, 0)

    @pl.when(i % 2 == 0)
    def _():
        stage(i, 0, own=True, score_next=False)

    @pl.when(i % 2 == 1)
    def _():
        stage(i - 1, 0, own=False)
        stage(i, 1, own=True, score_next=False)

    for _, h in heads:
        ot_ref[h * DH_A:(h + 1) * DH_A, :] = acc_ref[h, :DH_A, :] / acc_ref[h, DH_A:DH_A + 1, :]
    o_ref[...] = ot_ref[...].T.astype(o_ref.dtype)


def _moba_prompt(qt, k_bf, vt_bf, kmean, batch, seq):
    n_blocks = seq // MOBA_BLOCK
    tq = MOBA_BLOCK
    k3 = k_bf.reshape(batch, seq, A_W)
    km3 = kmean.reshape(batch, n_blocks, A_W)
    return pl.pallas_call(
        functools.partial(_moba_prompt_kernel, n_blocks=n_blocks),
        grid=(batch, n_blocks),
        in_specs=[pl.BlockSpec((1, A_W, tq), lambda b, i: (b, 0, i)),
                  pl.BlockSpec((1, seq, A_W), lambda b, i: (b, 0, 0)),
                  pl.BlockSpec((1, n_blocks, H_A, LANES, tq), lambda b, i: (b, 0, 0, 0, 0)),
                  pl.BlockSpec((1, n_blocks, A_W), lambda b, i: (b, 0, 0))],
        out_specs=pl.BlockSpec((tq, A_W), lambda b, i: (b * n_blocks + i, 0)),
        out_shape=jax.ShapeDtypeStruct((batch * seq, A_W), BF16),
        scratch_shapes=[pltpu.VMEM((H_A, LANES, tq), BF16),
                        pltpu.VMEM((H_A * n_blocks, 1, tq), F32),
                        pltpu.VMEM((H_A, tq, tq), F32),
                        pltpu.VMEM((H_A, tq, tq), F32),
                        pltpu.VMEM((H_A, 1, tq), F32),
                        pltpu.VMEM((H_A, LANES, tq), F32),
                        pltpu.VMEM((A_W, tq), F32)],
        compiler_params=_compiler_params(("parallel", "arbitrary")),
        name="moba_prompt",
    )(qt, k3, vt_bf, km3)


SAMPLE_RING_PAGES = 32
SAMPLE_BLOCKS_PER_ITER = 4


def _moba_sample_kernel(pt_ref, q_ref, kn_ref, vn_ref, ck_ref, cv_ref, o_ref,
                        ring_ref, ring_sem, qbd_ref, qbdt_ref, s_ref, score_ref, bmax_ref, acc_ref,
                        *, n_pages, n_blocks, page_rows, dec_seq):
    b = pl.program_id(0)
    n_seq = pl.num_programs(0)
    n_ring = SAMPLE_RING_PAGES
    stream_len = 2 * n_pages
    ppb = MOBA_BLOCK // page_rows
    n_q = H_A * dec_seq

    def ring_copy(pool_ref, page, slot):
        return pltpu.make_async_copy(pool_ref.at[page], ring_ref.at[slot], ring_sem.at[slot])

    def start_fetch(seq, pos, keys_only=False):
        slot = pos % n_ring
        if keys_only:
            ring_copy(ck_ref, pt_ref[seq, pos], slot).start()
            return

        @pl.when(pos < n_pages)
        def _():
            ring_copy(ck_ref, pt_ref[seq, pos], slot).start()

        @pl.when(pos >= n_pages)
        def _():
            ring_copy(cv_ref, pt_ref[seq, pos - n_pages], slot).start()

    def take(pos):
        slot = pos % n_ring
        ring_copy(ck_ref, 0, slot).wait()
        return slot

    def refill(pos):
        nxt = pos + n_ring

        @pl.when(nxt < stream_len)
        def _():
            start_fetch(b, nxt)

        @pl.when((nxt >= stream_len) & (b + 1 < n_seq))
        def _():
            start_fetch(b + 1, nxt - stream_len, keys_only=True)

    @pl.when(b == 0)
    def _():
        for pos in range(n_ring):
            start_fetch(0, pos, keys_only=True)

    scale = DH_A ** -0.5
    row = lax.broadcasted_iota(jnp.int32, (n_q, A_W), 0)
    lane_w = lax.broadcasted_iota(jnp.int32, (n_q, A_W), 1)
    head_diag = (row // dec_seq) == (lane_w // DH_A)
    lane = lax.broadcasted_iota(jnp.int32, (n_q, LANES), 1)
    tok = lax.broadcasted_iota(jnp.int32, (n_q, 1), 0) % dec_seq
    score_blk = lax.broadcasted_iota(jnp.int32, (LANES, LANES), 0)

    def page_cols(t):
        return slice(t * page_rows, (t + 1) * page_rows)

    q_rep = jnp.concatenate([q_ref[...]] * H_A, axis=0)
    qbd = jnp.where(head_diag, q_rep, 0.0)
    qbd_ref[...] = qbd
    qbdt_ref[...] = jnp.concatenate([qbd, jnp.zeros((LANES - n_q, A_W), F32)], axis=0).T
    score_ref[...] = jnp.full(score_ref.shape, NEG_INF, F32)
    bmax_ref[...] = jnp.full(bmax_ref.shape, NEG_INF, F32)

    unroll = SAMPLE_BLOCKS_PER_ITER
    pages_per_iter = unroll * ppb

    def k_blocks(it, carry):
        qb = (qbd_ref[...] * scale).astype(BF16)
        qbdt = qbdt_ref[...]
        pos0 = it * pages_per_iter
        slots = [take(pos0 + t) for t in range(pages_per_iter)]
        score, bmax = score_ref[...], bmax_ref[...]
        for u in range(unroll):
            jb = it * unroll + u
            kts = [ring_ref[slots[u * ppb + t]] for t in range(ppb)]
            kmean = jnp.sum(functools.reduce(lambda x, y: x + y, kts), axis=1, keepdims=True) * (1.0 / MOBA_BLOCK)
            sc = jnp.sum(qbdt * kmean, axis=0, keepdims=True)
            score = jnp.where(score_blk == jb, sc, score)
            s_pages = [jnp.dot(qb, kt.astype(BF16), preferred_element_type=F32) for kt in kts]
            for t in range(ppb):
                s_ref[jb, :, page_cols(t)] = s_pages[t]
            s_max = jnp.max(functools.reduce(jnp.maximum, s_pages), axis=1, keepdims=True)
            bmax = jnp.where(lane == jb, s_max, bmax)
        score_ref[...], bmax_ref[...] = score, bmax
        for t in range(pages_per_iter):
            refill(pos0 + t)
        return carry

    lax.fori_loop(0, n_blocks // unroll, k_blocks, 0)

    sel = _select_topk(score_ref[...].T[:n_q], lane, n_blocks, axis=1)
    qs = qbd * scale
    kn = kn_ref[...]
    own = []
    for t in range(dec_seq):
        so = jnp.sum(qs * kn[t:t + 1, :], axis=1, keepdims=True)
        own.append(jnp.where(t <= tok, so, NEG_INF))
    m = jnp.max(jnp.where(sel > 0.5, bmax_ref[...], NEG_INF), axis=1, keepdims=True)
    m = functools.reduce(jnp.maximum, own, m)
    p_own = jnp.zeros((n_q, LANES), F32)
    for t in range(dec_seq):
        p_own = jnp.where(lane == t, jnp.exp(own[t] - m), p_own)
    acc_ref[...] = jnp.zeros(acc_ref.shape, F32)

    def v_blocks(it, lsum):
        pos0 = n_pages + it * pages_per_iter
        slots = [take(pos0 + t) for t in range(pages_per_iter)]
        weights = []
        for u in range(unroll):
            jb = it * unroll + u
            picked = jnp.sum(jnp.where(lane == jb, sel, 0.0), axis=1, keepdims=True) > 0.5
            pj = jnp.exp(jnp.where(picked, s_ref[jb] - m, NEG_INF))
            lsum = lsum + pj
            weights.append(pj.astype(BF16))
        acc = acc_ref[...]
        for u in range(unroll):
            for t in range(ppb):
                vt = ring_ref[slots[u * ppb + t]].astype(BF16)
                acc = acc + lax.dot_general(weights[u][:, page_cols(t)], vt, NT_DIMS,
                                            preferred_element_type=F32)
        acc_ref[...] = acc
        for t in range(pages_per_iter):
            refill(pos0 + t)
        return lsum

    lsum = lax.fori_loop(0, n_blocks // unroll, v_blocks, jnp.zeros((n_q, MOBA_BLOCK), F32))

    acc = acc_ref[...]
    vn = vn_ref[...]
    for t in range(dec_seq):
        acc = acc + p_own[:, t:t + 1] * vn[t:t + 1, :]
    l = jnp.sum(lsum, axis=1, keepdims=True) + jnp.sum(p_own, axis=1, keepdims=True)
    out = jnp.where(head_diag, acc / l, 0.0)
    o_ref[...] = functools.reduce(
        lambda x, y: x + y, [out[h * dec_seq:(h + 1) * dec_seq, :] for h in range(H_A)])


def _moba_sample(qa, ka, va, cache_k, cache_v, page_table, dec_batch, dec_seq):
    n_pool, page_rows = cache_k.shape[0], cache_k.shape[1]
    n_pages = page_table.shape[1]
    past = n_pages * page_rows
    assert past % MOBA_BLOCK == 0 and MOBA_BLOCK % page_rows == 0, "cached rows must fill whole MoBA blocks"
    n_blocks = past // MOBA_BLOCK
    assert MOBA_TOPK <= n_blocks <= LANES
    n_ring = SAMPLE_RING_PAGES
    assert page_rows % LANES == 0 and n_ring <= n_pages and (2 * n_pages) % n_ring == 0
    assert n_blocks % SAMPLE_BLOCKS_PER_ITER == 0
    n_q = H_A * dec_seq
    assert n_q <= LANES
    ck = cache_k.transpose(0, 2, 3, 1).reshape(n_pool, A_W, page_rows)
    cv = cache_v.transpose(0, 2, 3, 1).reshape(n_pool, A_W, page_rows)

    tok_spec = pl.BlockSpec((dec_seq, A_W), lambda b, pt: (b, 0))
    pool_spec = pl.BlockSpec(memory_space=pl.ANY)
    grid_spec = pltpu.PrefetchScalarGridSpec(
        num_scalar_prefetch=1,
        grid=(dec_batch,),
        in_specs=[tok_spec, tok_spec, tok_spec, pool_spec, pool_spec],
        out_specs=tok_spec,
        scratch_shapes=[pltpu.VMEM((n_ring, A_W, page_rows), F32),
                        pltpu.SemaphoreType.DMA((n_ring,)),
                        pltpu.VMEM((n_q, A_W), F32),
                        pltpu.VMEM((A_W, LANES), F32),
                        pltpu.VMEM((n_blocks, n_q, MOBA_BLOCK), F32),
                        pltpu.VMEM((LANES, LANES), F32),
                        pltpu.VMEM((n_q, LANES), F32),
                        pltpu.VMEM((n_q, A_W), F32)],
    )
    return pl.pallas_call(
        functools.partial(_moba_sample_kernel, n_pages=n_pages, n_blocks=n_blocks,
                          page_rows=page_rows, dec_seq=dec_seq),
        grid_spec=grid_spec,
        out_shape=jax.ShapeDtypeStruct((dec_batch * dec_seq, A_W), F32),
        compiler_params=_compiler_params(("arbitrary",)),
        name="moba_sample",
    )(page_table, qa, ka, va, ck, cv)


MLSTM_SEQS_PER_STEP = 4


def _mlstm_kernel(q_ref, k_ref, v_ref, o_ref, gate_ref, c0_ref, n0_ref, m0_ref,
                  h_ref, c_ref, n_ref, m_ref, *, chunk):
    step = pl.program_id(1)
    lp = MLSTM_CHUNK
    n_seq = q_ref.shape[0]

    @pl.when(step == 0)
    def _():
        c_ref[...] = c0_ref[...]
        n_ref[...] = n0_ref[...]
        m_ref[...] = m0_ref[...]

    def pad_rows(a):
        if chunk == lp:
            return a
        return jnp.concatenate([a, jnp.zeros((lp - chunk, a.shape[1]), a.dtype)], axis=0)

    src = lax.broadcasted_iota(jnp.int32, (lp, lp), 0)
    tgt = lax.broadcasted_iota(jnp.int32, (lp, lp), 1)
    lane8 = lax.broadcasted_iota(jnp.int32, (N_GATE, lp), 1)
    chains = [(b, h) for b in range(n_seq) for h in range(H_M)]
    lanes = [slice(h * DH_M, (h + 1) * DH_M) for h in range(H_M)]
    last = slice(chunk - 1, chunk)

    q_all = [pad_rows(q_ref[b]) for b in range(n_seq)]
    k_all = [pad_rows(k_ref[b]) for b in range(n_seq)]
    v_all = [pad_rows(v_ref[b]) for b in range(n_seq)]
    qb = {(b, h): q_all[b][:, lanes[h]].astype(BF16) for b, h in chains}
    kb = {(b, h): k_all[b][:, lanes[h]].astype(BF16) for b, h in chains}

    s_kq = {ch: lax.dot_general(kb[ch], qb[ch], NT_DIMS, preferred_element_type=F32) for ch in chains}
    c_q = {(b, h): lax.dot_general(c_ref[b, h].astype(BF16), qb[b, h], NT_DIMS, preferred_element_type=F32)
           for b, h in chains}
    n_q = {(b, h): lax.dot_general(jnp.broadcast_to(n_ref[b, h], (N_GATE, DH_M)).astype(BF16), qb[b, h], NT_DIMS,
                                   preferred_element_type=F32)[0:1] for b, h in chains}

    g_row, b_row_all = [], []
    for b in range(n_seq):
        rows = pad_rows(gate_ref[b]).T[:N_GATE]
        csum = rows
        shift = 1
        while shift < lp:
            csum = csum + jnp.where(lane8 >= shift, pltpu.roll(csum, shift, 1), 0.0)
            shift *= 2
        g_row.append(rows)
        b_row_all.append(csum)

    v_t = {(b, h): v_all[b][:, lanes[h]].T for b, h in chains}
    w_d, w_prev, m_row, decay, w_s = {}, {}, {}, {}, {}
    for b, h in chains:
        i_row, b_row = g_row[b][h:h + 1], b_row_all[b][H_M + h:H_M + h + 1]
        m_prev = m_ref[b, h]
        u_src = jnp.broadcast_to(b_row - i_row, (lp, lp)).T
        log_d = jnp.where(src <= tgt, b_row - u_src, NEG_INF)
        m_row[b, h] = jnp.maximum(m_prev + b_row, jnp.max(log_d, axis=0, keepdims=True))
        w_d[b, h] = jnp.exp(log_d - m_row[b, h])
        w_prev[b, h] = jnp.exp(m_prev + b_row - m_row[b, h])
        m_new = m_row[b, h][:, last]
        b_last = b_row[:, last]
        decay[b, h] = jnp.exp(m_prev + b_last - m_new)
        w_s[b, h] = jnp.where(lane8[0:1] < chunk, jnp.exp(b_last - b_row + i_row - m_new), 0.0)
        m_ref[b, h] = m_new

    a = {ch: s_kq[ch] * w_d[ch] for ch in chains}
    v_a = {ch: jnp.dot(v_t[ch].astype(BF16), a[ch].astype(BF16), preferred_element_type=F32) for ch in chains}
    c_upd = {ch: jnp.dot((v_t[ch] * w_s[ch]).astype(BF16), kb[ch], preferred_element_type=F32) for ch in chains}
    n_upd = {ch: jnp.dot(jnp.broadcast_to(w_s[ch], (N_GATE, lp)).astype(BF16), kb[ch],
                         preferred_element_type=F32)[0:1] for ch in chains}

    for b, h in chains:
        num = v_a[b, h] + w_prev[b, h] * c_q[b, h]
        den = jnp.sum(a[b, h], axis=0, keepdims=True) + w_prev[b, h] * n_q[b, h]
        hc = (num / jnp.maximum(jnp.abs(den), jnp.exp(-m_row[b, h]))).T
        out = jax.nn.sigmoid(pad_rows(o_ref[b])[:, lanes[h]]) * hc
        h_ref[b, :, lanes[h]] = out[:chunk].astype(h_ref.dtype)
        c_ref[b, h] = decay[b, h] * c_ref[b, h] + c_upd[b, h]
        n_ref[b, h] = decay[b, h] * n_ref[b, h] + n_upd[b, h]


def _mlstm(qm, km, vm, om, gates, c0, n0, m0, batch, seq):
    chunk = math.gcd(seq, MLSTM_CHUNK)
    n_chunks = seq // chunk
    nb = MLSTM_SEQS_PER_STEP
    assert batch % nb == 0
    h_dtype = BF16 if chunk % BF16_SUBLANES == 0 else F32
    n0_4 = n0.reshape(batch, H_M, 1, DH_M)
    m0_4 = m0.reshape(batch, H_M, 1, 1)
    per_seq = lambda a: a.reshape(batch, seq, a.shape[-1])
    row = lambda w: pl.BlockSpec((nb, chunk, w), lambda g, s: (g, s, 0))
    state = lambda a: pl.BlockSpec((nb,) + a.shape[1:], lambda g, s: (g, 0, 0, 0))
    h, c, n, m = pl.pallas_call(
        functools.partial(_mlstm_kernel, chunk=chunk),
        grid=(batch // nb, n_chunks),
        in_specs=[row(M_W), row(M_W), row(M_W), row(M_W), row(LANES), state(c0), state(n0_4), state(m0_4)],
        out_specs=(row(M_W), state(c0), state(n0_4), state(m0_4)),
        out_shape=(jax.ShapeDtypeStruct((batch, seq, M_W), h_dtype),
                   jax.ShapeDtypeStruct(c0.shape, F32),
                   jax.ShapeDtypeStruct(n0_4.shape, F32),
                   jax.ShapeDtypeStruct(m0_4.shape, F32)),
        compiler_params=_compiler_params(("parallel", "arbitrary")),
        name="mlstm",
    )(per_seq(qm), per_seq(km), per_seq(vm), per_seq(om), per_seq(gates), c0, n0_4, m0_4)
    return h.reshape(batch * seq, M_W), c, n.reshape(batch, H_M, DH_M), m.reshape(batch, H_M)


def _merge_kernel(x_ref, att_ref, ml_ref, gate_a_ref, gate_m_ref, wa_ref, wm_ref, wo_ref, g_ref, o_ref):
    ya = jnp.dot(att_ref[...].astype(BF16), wa_ref[...], preferred_element_type=F32)
    ym = jnp.dot(ml_ref[...].astype(BF16), wm_ref[...], preferred_element_type=F32)
    u = gate_a_ref[...].astype(F32) * ya + gate_m_ref[...].astype(F32) * ym
    r = jnp.dot(u.astype(BF16), wo_ref[...], preferred_element_type=F32)
    o_ref[...] = x_ref[...] + _rms(r, g_ref[...])


def _merge(x2d, att, ml, ga, gm, wa, wm, wo, g_post_mix):
    n = x2d.shape[0]
    tm = min(512, n)
    row = lambda w: pl.BlockSpec((tm, w), lambda i: (i, 0))
    full = lambda a: pl.BlockSpec(a.shape, lambda i: (0,) * a.ndim)
    g_row = g_post_mix.reshape(1, D_MODEL)
    return pl.pallas_call(
        _merge_kernel,
        grid=(n // tm,),
        in_specs=[row(D_MODEL), row(A_W), row(M_W), row(D_MODEL), row(D_MODEL),
                  full(wa), full(wm), full(wo), full(g_row)],
        out_specs=row(D_MODEL),
        out_shape=jax.ShapeDtypeStruct((n, D_MODEL), F32),
        compiler_params=_compiler_params(("parallel",)),
        name="merge",
    )(x2d, att, ml, ga, gm, wa, wm, wo, g_row)


MLP_FF_TILE = 1024


def _mlp_kernel(x_ref, g1_ref, wu_ref, wd_ref, g2_ref, o_ref):
    x = x_ref[...]
    h = _rms(x, g1_ref[...]).astype(BF16)
    n_chunks = D_FF // MLP_FF_TILE
    cols = lambda c: slice(c * MLP_FF_TILE, (c + 1) * MLP_FF_TILE)
    up = lambda c: jnp.dot(h, wu_ref[:, cols(c)], preferred_element_type=F32)
    pending = up(0)
    acc = None
    for c in range(n_chunks):
        act = jnp.square(jnp.maximum(pending, 0.0)).astype(BF16)
        if c + 1 < n_chunks:
            pending = up(c + 1)
        part = jnp.dot(act, wd_ref[cols(c), :], preferred_element_type=F32)
        acc = part if acc is None else acc + part
    o_ref[...] = x + _rms(acc, g2_ref[...])


def _mlp(x2d, g_pre_mlp, w_up, w_down, g_post_mlp):
    n = x2d.shape[0]
    tm = min(512, n)
    g1 = g_pre_mlp.reshape(1, D_MODEL)
    g2 = g_post_mlp.reshape(1, D_MODEL)
    gspec = pl.BlockSpec((1, D_MODEL), lambda i: (0, 0))
    resident = lambda a: pl.BlockSpec(a.shape, lambda i: (0, 0), pipeline_mode=pl.Buffered(1))
    return pl.pallas_call(
        _mlp_kernel,
        grid=(n // tm,),
        in_specs=[pl.BlockSpec((tm, D_MODEL), lambda i: (i, 0)), gspec, resident(w_up), resident(w_down), gspec],
        out_specs=pl.BlockSpec((tm, D_MODEL), lambda i: (i, 0)),
        out_shape=jax.ShapeDtypeStruct((n, D_MODEL), F32),
        compiler_params=_compiler_params(("parallel",)),
        name="mlp",
    )(x2d, g1, w_up, w_down, g2)


def kernel(x_prompt, x_sample, cache_k, cache_v, state_C, state_n, state_m, page_table, g_pre_mix, w_in, b_if,
           w_attn_br, w_mlstm_br, w_out, g_post_mix, g_pre_mlp, w_up, w_down, g_post_mlp):
    bp, sp, _ = x_prompt.shape
    db, ds, _ = x_sample.shape
    past = page_table.shape[1] * cache_k.shape[1]

    gate0 = 3 * A_W + 4 * M_W
    w_in_bf = w_in.astype(BF16)
    w_branch = w_in_bf[:, gate0 + N_GATE:]
    w_gate = jnp.pad(w_in_bf[:, gate0:gate0 + N_GATE], ((0, 0), (0, LANES - N_GATE)))
    b_if_row = jnp.pad(b_if.astype(F32), (0, LANES - N_GATE)).reshape(1, LANES)
    wa, wm, wo = w_attn_br.astype(BF16), w_mlstm_br.astype(BF16), w_out.astype(BF16)
    wu, wd = w_up.astype(BF16), w_down.astype(BF16)

    def tail(x2d, att, ml, ga, gm):
        x1 = _merge(x2d, att, ml, ga, gm, wa, wm, wo, g_post_mix)
        return _mlp(x1, g_pre_mlp, wu, wd, g_post_mlp)

    xp = x_prompt.reshape(bp * sp, D_MODEL)
    (qt, kt, vt, k_bf, vt_bf, kmean, qm, km, vm, om, ga, gm, gates) = _project(
        xp, jnp.arange(sp, dtype=F32), g_pre_mix, w_in_bf, w_branch, w_gate, b_if_row, feature_major=True)
    att_p = _moba_prompt(qt, k_bf, vt_bf, kmean, bp, sp)
    ml_p, c_p, n_p, m_p = _mlstm(qm, km, vm, om, gates,
                                 jnp.zeros((bp, H_M, DH_M, DH_M), F32), jnp.zeros((bp, H_M, DH_M), F32),
                                 jnp.zeros((bp, H_M), F32), bp, sp)
    y_prompt = tail(xp, att_p, ml_p, ga, gm).reshape(bp, sp, D_MODEL)
    k_prompt = kt.reshape(bp, H_A, DH_A, sp).transpose(0, 3, 1, 2)
    v_prompt = vt.reshape(bp, H_A, DH_A, sp).transpose(0, 3, 1, 2)

    xs = x_sample.reshape(db * ds, D_MODEL)
    (qa, ka, va, qm, km, vm, om, ga, gm, gates) = _project(
        xs, past + jnp.arange(ds, dtype=F32), g_pre_mix, w_in_bf, w_branch, w_gate, b_if_row, feature_major=False)
    att_s = _moba_sample(qa, ka, va, cache_k, cache_v, page_table, db, ds)
    ml_s, c_s, n_s, m_s = _mlstm(qm, km, vm, om, gates, state_C.astype(F32), state_n.astype(F32),
                                 state_m.astype(F32), db, ds)
    y_sample = tail(xs, att_s, ml_s, ga, gm).reshape(db, ds, D_MODEL)
    k_sample = ka.reshape(db, ds, H_A, DH_A)
    v_sample = va.reshape(db, ds, H_A, DH_A)

    return (y_prompt, y_sample, k_prompt, v_prompt, c_p, n_p, m_p, k_sample, v_sample, c_s, n_s, m_s)
```

```python
import functools
import math

import jax
import jax.numpy as jnp
from jax import lax
from jax.experimental import pallas as pl
from jax.experimental.pallas import tpu as pltpu

F32 = jnp.float32
BF16 = jnp.bfloat16
NEG_INF = float("-inf")
LOG2_E = math.log2(math.e)

D_MODEL = 1024
H_A = 8
DH_A = 64
A_W = H_A * DH_A
MOBA_BLOCK = 256
MOBA_TOPK = 3
ROT_DIMS = DH_A // 4
ROPE_THETA = 500000.0
H_M = 4
DH_M = 128
M_W = H_M * DH_M
MLSTM_CHUNK = 128
D_FF = 4 * D_MODEL
RMS_EPS = 1e-6
N_GATE = 2 * H_M

LANES = 128
BF16_SUBLANES = 16
HEADS_PER_LANE_GROUP = LANES // DH_A
VMEM_LIMIT_BYTES = 52 * 1024 * 1024

NT_DIMS = (((1,), (1,)), ((), ()))
TN_DIMS = (((0,), (0,)), ((), ()))


def _compiler_params(semantics):
    return pltpu.CompilerParams(dimension_semantics=semantics, vmem_limit_bytes=VMEM_LIMIT_BYTES)


def _rms(x, g):
    return x * lax.rsqrt(jnp.mean(x * x, axis=-1, keepdims=True) + RMS_EPS) * g


def _log_sigmoid(x):
    return jnp.minimum(x, 0.0) - jnp.log1p(jnp.exp(-jnp.abs(x)))


PROJ_ROWS = MOBA_BLOCK


def _project_kernel(x_ref, g_ref, w_ref, wb_ref, wg_ref, bif_ref, rc_ref, rs1_ref, rs2_ref, *out_refs,
                    feature_major):
    if feature_major:
        (qt_ref, kt_ref, vt_ref, kb_ref, vtb_ref, kmean_ref,
         qm_ref, km_ref, vm_ref, om_ref, ga_ref, gm_ref, gates_ref) = out_refs
    else:
        qa_ref, ka_ref, va_ref, qm_ref, km_ref, vm_ref, om_ref, ga_ref, gm_ref, gates_ref = out_refs
    hb = _rms(x_ref[...], g_ref[...]).astype(BF16)

    def mm(c0, n):
        return jnp.dot(hb, w_ref[:, c0:c0 + n], preferred_element_type=F32)

    rc, rs1, rs2 = rc_ref[...], rs1_ref[...], rs2_ref[...]

    def rot(z):
        half = ROT_DIMS // 2
        outs = []
        for c in range(z.shape[1] // LANES):
            zc = z[:, c * LANES:(c + 1) * LANES]
            outs.append(zc * rc + pltpu.roll(zc, LANES - half, 1) * rs1 + pltpu.roll(zc, half, 1) * rs2)
        return jnp.concatenate(outs, axis=1)

    qa = rot(mm(0, A_W))
    ka = rot(mm(A_W, A_W))
    va = mm(2 * A_W, A_W)
    if feature_major:
        qt_ref[0] = qa.T
        kt_ref[0] = ka.T
        kb_ref[...] = ka.astype(BF16)
        kmean_ref[...] = jnp.mean(ka, axis=0, keepdims=True).reshape(1, 1, A_W)
        vt = va.T
        vt_ref[0] = vt
        ones_row = (lax.broadcasted_iota(jnp.int32, (LANES - DH_A, vt.shape[1]), 0) == 0).astype(BF16)
        for h in range(H_A):
            vtb_ref[0, 0, h] = jnp.concatenate([vt[h * DH_A:(h + 1) * DH_A].astype(BF16), ones_row], axis=0)
    else:
        qa_ref[...] = qa
        ka_ref[...] = ka
        va_ref[...] = va
    c0 = 3 * A_W
    qm_ref[...] = mm(c0, M_W).astype(qm_ref.dtype)
    km_ref[...] = (mm(c0 + M_W, M_W) * (DH_M ** -0.5)).astype(km_ref.dtype)
    vm_ref[...] = mm(c0 + 2 * M_W, M_W)
    om_ref[...] = mm(c0 + 3 * M_W, M_W)
    ga_ref[...] = jax.nn.sigmoid(jnp.dot(hb, wb_ref[:, :D_MODEL], preferred_element_type=F32)).astype(ga_ref.dtype)
    gm_ref[...] = jax.nn.sigmoid(jnp.dot(hb, wb_ref[:, D_MODEL:], preferred_element_type=F32)).astype(gm_ref.dtype)
    zg = jnp.dot(hb, wg_ref[...], preferred_element_type=F32) + bif_ref[...]
    lane = lax.broadcasted_iota(jnp.int32, zg.shape, 1)
    is_forget = (lane >= H_M) & (lane < N_GATE)
    gates_ref[...] = jnp.where(is_forget, _log_sigmoid(zg), zg)


def _rotary_tables(pos):
    half = ROT_DIMS // 2
    inv = ROPE_THETA ** (-jnp.arange(half, dtype=F32) * 2.0 / ROT_DIMS)
    ang = pos[:, None] * inv[None, :]
    cos, sin = jnp.cos(ang), jnp.sin(ang)
    n = pos.shape[0]
    pad = jnp.zeros((n, DH_A - ROT_DIMS), F32)
    c_head = jnp.concatenate([cos, cos, pad + 1.0], axis=1)
    s1_head = jnp.concatenate([-sin, jnp.zeros_like(sin), pad], axis=1)
    s2_head = jnp.concatenate([jnp.zeros_like(sin), sin, pad], axis=1)
    rep = lambda t: jnp.tile(t, (1, HEADS_PER_LANE_GROUP))
    return rep(c_head), rep(s1_head), rep(s2_head)


def _project(x2d, pos, g_pre_mix, w_in_bf, w_branch, w_gate, b_if_row, feature_major):
    n = x2d.shape[0]
    tm = PROJ_ROWS
    n_tiles = n // tm
    rc, rs1, rs2 = _rotary_tables(pos)
    if pos.shape[0] < tm:
        reps = tm // pos.shape[0]
        rc, rs1, rs2 = (jnp.tile(t, (reps, 1)) for t in (rc, rs1, rs2))
    tab_tiles = rc.shape[0] // tm
    row = lambda w: pl.BlockSpec((tm, w), lambda i: (i, 0))
    full = lambda a: pl.BlockSpec(a.shape, lambda i: (0,) * a.ndim)
    tab = pl.BlockSpec((tm, LANES), lambda i: (i % tab_tiles, 0))
    f32 = lambda w: jax.ShapeDtypeStruct((n, w), F32)
    if feature_major:
        seq = pos.shape[0]
        batch = n // seq
        t_shape = jax.ShapeDtypeStruct((batch, A_W, seq), F32)
        t_spec = pl.BlockSpec((1, A_W, tm), lambda i: (i // tab_tiles, 0, i % tab_tiles))
        attn_shape = (t_shape, t_shape, t_shape, jax.ShapeDtypeStruct((n, A_W), BF16),
                      jax.ShapeDtypeStruct((batch, tab_tiles, H_A, LANES, tm), BF16),
                      jax.ShapeDtypeStruct((n_tiles, 1, A_W), F32))
        attn_specs = (t_spec, t_spec, t_spec, row(A_W),
                      pl.BlockSpec((1, 1, H_A, LANES, tm), lambda i: (i // tab_tiles, i % tab_tiles, 0, 0, 0)),
                      pl.BlockSpec((1, 1, A_W), lambda i: (i, 0, 0)))
    else:
        attn_shape = (f32(A_W), f32(A_W), f32(A_W))
        attn_specs = (row(A_W), row(A_W), row(A_W))
    qk = jax.ShapeDtypeStruct((n, M_W), BF16 if feature_major else F32)
    gate = jax.ShapeDtypeStruct((n, D_MODEL), BF16)
    out_shape = attn_shape + (qk, qk, f32(M_W), f32(M_W), gate, gate, f32(LANES))
    out_specs = attn_specs + (row(M_W), row(M_W), row(M_W), row(M_W), row(D_MODEL), row(D_MODEL), row(LANES))
    g_row = g_pre_mix.reshape(1, D_MODEL)
    return pl.pallas_call(
        functools.partial(_project_kernel, feature_major=feature_major),
        grid=(n_tiles,),
        in_specs=[row(D_MODEL), full(g_row), pl.BlockSpec((D_MODEL, 3 * A_W + 4 * M_W), lambda i: (0, 0)),
                  full(w_branch), full(w_gate), full(b_if_row), tab, tab, tab],
        out_specs=out_specs,
        out_shape=out_shape,
        compiler_params=_compiler_params(("parallel",)),
        name="project",
    )(x2d, g_row, w_in_bf, w_branch, w_gate, b_if_row, rc, rs1, rs2)


def _select_topk(scores, block_idx, n_valid, axis):
    width = scores.shape[axis]
    sc = jnp.where(block_idx < n_valid, scores, NEG_INF)
    sel = jnp.zeros(scores.shape, jnp.bool_)
    for _ in range(MOBA_TOPK):
        mx = jnp.max(sc, axis=axis, keepdims=True)
        idx = jnp.min(jnp.where(sc == mx, block_idx, width), axis=axis, keepdims=True)
        hit = block_idx == idx
        sel = sel | hit
        sc = jnp.where(hit, NEG_INF, sc)
    return jnp.where(sel & (block_idx < n_valid), 1.0, 0.0)


def _moba_prompt_kernel(qt_ref, k_ref, vt_ref, kmean_ref, o_ref,
                        qtb_ref, sel_ref, s0_ref, s1_ref, m_ref, acc_ref, ot_ref, *, n_blocks):
    i = pl.program_id(1)
    tq = MOBA_BLOCK
    scale = DH_A ** -0.5
    key = lax.broadcasted_iota(jnp.int32, (tq, tq), 0)
    qry = lax.broadcasted_iota(jnp.int32, (tq, tq), 1)
    causal = key <= qry
    feat = lax.broadcasted_iota(jnp.int32, (LANES, tq), 0)
    blk = lax.broadcasted_iota(jnp.int32, (n_blocks, tq), 0)
    n_pairs = A_W // LANES
    pair_feats = [slice(p * LANES, (p + 1) * LANES) for p in range(n_pairs)]

    heads = [(p, p * HEADS_PER_LANE_GROUP + hh) for p in range(n_pairs) for hh in range(HEADS_PER_LANE_GROUP)]

    def score(h, p, block):
        start = block * tq if isinstance(block, int) else pl.multiple_of(block * tq, tq)
        k_blk = k_ref[0, pl.ds(start, tq), pair_feats[p]]
        return jnp.dot(k_blk, qtb_ref[h], preferred_element_type=F32)

    def fold(h, p, block, s, own):
        m_old = m_ref[h]
        if own:
            s = jnp.where(causal, s, NEG_INF)
            m_new = jnp.maximum(m_old, jnp.max(s, axis=0, keepdims=True))
            m_shift = m_exp = m_new
        else:
            picked = sel_ref[h * n_blocks + block] > 0.5
            m_new = jnp.maximum(m_old, jnp.where(picked, jnp.max(s, axis=0, keepdims=True), NEG_INF))
            m_shift = jnp.where(m_new == NEG_INF, 0.0, m_new)
            m_exp = jnp.where(picked, m_shift, jnp.inf)
        alpha = jnp.exp2(m_old - m_shift)
        pe = jnp.exp2((s - m_exp).astype(BF16))
        pv = jnp.dot(vt_ref[0, block, h], pe, preferred_element_type=F32)
        m_ref[h] = m_new
        acc_ref[h] = alpha * acc_ref[h] + pv

    for p in range(n_pairs):
        qt_pair = qt_ref[0, pair_feats[p], :]
        kmean_pair = kmean_ref[0, :, pair_feats[p]]
        for hh in range(HEADS_PER_LANE_GROUP):
            h = p * HEADS_PER_LANE_GROUP + hh
            qt_head = jnp.where((feat // DH_A) == hh, qt_pair, 0.0)
            scores = jnp.dot(kmean_pair, qt_head, precision=lax.Precision.HIGHEST, preferred_element_type=F32)
            sel = _select_topk(scores, blk, i, axis=0)
            for j in range(n_blocks):
                sel_ref[h * n_blocks + j] = sel[j:j + 1, :]
            qtb_ref[h] = (qt_head * (scale * LOG2_E)).astype(BF16)
            m_ref[h] = jnp.full((1, tq), NEG_INF, F32)
            acc_ref[h] = jnp.zeros((LANES, tq), F32)

    bufs = (s0_ref, s1_ref)

    def stage(block, cur, own, score_next=True):
        for p, h in heads:
            if score_next:
                bufs[1 - cur][h] = score(h, p, block + 1)
            fold(h, p, block, bufs[cur][h], own)

    for p, h in heads:
        s0_ref[h] = score(h, p, 0)

    def body(jj, carry):
        stage(2 * jj, 0, own=False)
        stage(2 * jj + 1, 1, own=False)
        return carry

    lax.fori_loop(0, i // 2, body, 0)

    @pl.when(i % 2 == 0)
    def _():
        stage(i, 0, own=True, score_next=False)

    @pl.when(i % 2 == 1)
    def _():
        stage(i - 1, 0, own=False)
        stage(i, 1, own=True, score_next=False)

    for _, h in heads:
        ot_ref[h * DH_A:(h + 1) * DH_A, :] = acc_ref[h, :DH_A, :] / acc_ref[h, DH_A:DH_A + 1, :]
    o_ref[...] = ot_ref[...].T.astype(o_ref.dtype)


def _moba_prompt(qt, k_bf, vt_bf, kmean, batch, seq):
    n_blocks = seq // MOBA_BLOCK
    tq = MOBA_BLOCK
    k3 = k_bf.reshape(batch, seq, A_W)
    km3 = kmean.reshape(batch, n_blocks, A_W)
    return pl.pallas_call(
        functools.partial(_moba_prompt_kernel, n_blocks=n_blocks),
        grid=(batch, n_blocks),
        in_specs=[pl.BlockSpec((1, A_W, tq), lambda b, i: (b, 0, i)),
                  pl.BlockSpec((1, seq, A_W), lambda b, i: (b, 0, 0)),
                  pl.BlockSpec((1, n_blocks, H_A, LANES, tq), lambda b, i: (b, 0, 0, 0, 0)),
                  pl.BlockSpec((1, n_blocks, A_W), lambda b, i: (b, 0, 0))],
        out_specs=pl.BlockSpec((tq, A_W), lambda b, i: (b * n_blocks + i, 0)),
        out_shape=jax.ShapeDtypeStruct((batch * seq, A_W), BF16),
        scratch_shapes=[pltpu.VMEM((H_A, LANES, tq), BF16),
                        pltpu.VMEM((H_A * n_blocks, 1, tq), F32),
                        pltpu.VMEM((H_A, tq, tq), F32),
                        pltpu.VMEM((H_A, tq, tq), F32),
                        pltpu.VMEM((H_A, 1, tq), F32),
                        pltpu.VMEM((H_A, LANES, tq), F32),
                        pltpu.VMEM((A_W, tq), F32)],
        compiler_params=_compiler_params(("parallel", "arbitrary")),
        name="moba_prompt",
    )(qt, k3, vt_bf, km3)


SAMPLE_RING_PAGES = 64
SAMPLE_BLOCKS_PER_ITER = 8


def _moba_sample_kernel(pt_ref, q_ref, kn_ref, vn_ref, ck_ref, cv_ref, o_ref,
                        ring_ref, ring_sem, qbd_ref, qbdt_ref, s_ref, score_ref, bmax_ref, acc_ref,
                        *, n_pages, n_blocks, page_rows, dec_seq):
    b = pl.program_id(0)
    n_seq = pl.num_programs(0)
    n_ring = SAMPLE_RING_PAGES
    stream_len = 2 * n_pages
    ppb = MOBA_BLOCK // page_rows
    n_q = H_A * dec_seq

    def ring_copy(pool_ref, page, slot):
        return pltpu.make_async_copy(pool_ref.at[page], ring_ref.at[slot], ring_sem.at[slot])

    def start_fetch(seq, pos, keys_only=False):
        slot = pos % n_ring
        if keys_only:
            ring_copy(ck_ref, pt_ref[seq, pos], slot).start()
            return

        @pl.when(pos < n_pages)
        def _():
            ring_copy(ck_ref, pt_ref[seq, pos], slot).start()

        @pl.when(pos >= n_pages)
        def _():
            ring_copy(cv_ref, pt_ref[seq, pos - n_pages], slot).start()

    def take(pos):
        slot = pos % n_ring
        ring_copy(ck_ref, 0, slot).wait()
        return slot

    def refill(pos):
        nxt = pos + n_ring

        @pl.when(nxt < stream_len)
        def _():
            start_fetch(b, nxt)

        @pl.when((nxt >= stream_len) & (b + 1 < n_seq))
        def _():
            start_fetch(b + 1, nxt - stream_len, keys_only=True)

    @pl.when(b == 0)
    def _():
        for pos in range(n_ring):
            start_fetch(0, pos, keys_only=True)

    scale = DH_A ** -0.5
    row = lax.broadcasted_iota(jnp.int32, (n_q, A_W), 0)
    lane_w = lax.broadcasted_iota(jnp.int32, (n_q, A_W), 1)
    head_diag = (row // dec_seq) == (lane_w // DH_A)
    lane = lax.broadcasted_iota(jnp.int32, (n_q, LANES), 1)
    tok = lax.broadcasted_iota(jnp.int32, (n_q, 1), 0) % dec_seq
    score_blk = lax.broadcasted_iota(jnp.int32, (LANES, LANES), 0)

    def page_cols(t):
        return slice(t * page_rows, (t + 1) * page_rows)

    q_rep = jnp.concatenate([q_ref[...]] * H_A, axis=0)
    qbd = jnp.where(head_diag, q_rep, 0.0)
    qbd_ref[...] = qbd
    qbdt_ref[...] = jnp.concatenate([qbd, jnp.zeros((LANES - n_q, A_W), F32)], axis=0).T
    score_ref[...] = jnp.full(score_ref.shape, NEG_INF, F32)
    bmax_ref[...] = jnp.full(bmax_ref.shape, NEG_INF, F32)

    unroll = SAMPLE_BLOCKS_PER_ITER
    pages_per_iter = unroll * ppb

    def k_blocks(it, carry):
        qb = (qbd_ref[...] * scale).astype(BF16)
        qbdt = qbdt_ref[...]
        pos0 = it * pages_per_iter
        slots = [take(pos0 + t) for t in range(pages_per_iter)]
        score, bmax = score_ref[...], bmax_ref[...]
        for u in range(unroll):
            jb = it * unroll + u
            kts = [ring_ref[slots[u * ppb + t]] for t in range(ppb)]
            kmean = jnp.sum(functools.reduce(lambda x, y: x + y, kts), axis=1, keepdims=True) * (1.0 / MOBA_BLOCK)
            sc = jnp.sum(qbdt * kmean, axis=0, keepdims=True)
            score = jnp.where(score_blk == jb, sc, score)
            s_pages = [jnp.dot(qb, kt.astype(BF16), preferred_element_type=F32) for kt in kts]
            for t in range(ppb):
                s_ref[jb, :, page_cols(t)] = s_pages[t]
            s_max = jnp.max(functools.reduce(jnp.maximum, s_pages), axis=1, keepdims=True)
            bmax = jnp.where(lane == jb, s_max, bmax)
        score_ref[...], bmax_ref[...] = score, bmax
        for t in range(pages_per_iter):
            refill(pos0 + t)
        return carry

    lax.fori_loop(0, n_blocks // unroll, k_blocks, 0)

    sel = _select_topk(score_ref[...].T[:n_q], lane, n_blocks, axis=1)
    qs = qbd * scale
    kn = kn_ref[...]
    own = []
    for t in range(dec_seq):
        so = jnp.sum(qs * kn[t:t + 1, :], axis=1, keepdims=True)
        own.append(jnp.where(t <= tok, so, NEG_INF))
    m = jnp.max(jnp.where(sel > 0.5, bmax_ref[...], NEG_INF), axis=1, keepdims=True)
    m = functools.reduce(jnp.maximum, own, m)
    p_own = jnp.zeros((n_q, LANES), F32)
    for t in range(dec_seq):
        p_own = jnp.where(lane == t, jnp.exp(own[t] - m), p_own)
    acc_ref[...] = jnp.zeros(acc_ref.shape, F32)

    def v_blocks(it, lsum):
        pos0 = n_pages + it * pages_per_iter
        slots = [take(pos0 + t) for t in range(pages_per_iter)]
        weights = []
        for u in range(unroll):
            jb = it * unroll + u
            picked = jnp.sum(jnp.where(lane == jb, sel, 0.0), axis=1, keepdims=True) > 0.5
            pj = jnp.exp(jnp.where(picked, s_ref[jb] - m, NEG_INF))
            lsum = lsum + pj
            weights.append(pj.astype(BF16))
        acc = acc_ref[...]
        for u in range(unroll):
            for t in range(ppb):
                vt = ring_ref[slots[u * ppb + t]].astype(BF16)
                acc = acc + lax.dot_general(weights[u][:, page_cols(t)], vt, NT_DIMS,
                                            preferred_element_type=F32)
        acc_ref[...] = acc
        for t in range(pages_per_iter):
            refill(pos0 + t)
        return lsum

    lsum = lax.fori_loop(0, n_blocks // unroll, v_blocks, jnp.zeros((n_q, MOBA_BLOCK), F32))

    acc = acc_ref[...]
    vn = vn_ref[...]
    for t in range(dec_seq):
        acc = acc + p_own[:, t:t + 1] * vn[t:t + 1, :]
    l = jnp.sum(lsum, axis=1, keepdims=True) + jnp.sum(p_own, axis=1, keepdims=True)
    out = jnp.where(head_diag, acc / l, 0.0)
    o_ref[...] = functools.reduce(
        lambda x, y: x + y, [out[h * dec_seq:(h + 1) * dec_seq, :] for h in range(H_A)])


def _moba_sample(qa, ka, va, cache_k, cache_v, page_table, dec_batch, dec_seq):
    n_pool, page_rows = cache_k.shape[0], cache_k.shape[1]
    n_pages = page_table.shape[1]
    past = n_pages * page_rows
    assert past % MOBA_BLOCK == 0 and MOBA_BLOCK % page_rows == 0, "cached rows must fill whole MoBA blocks"
    n_blocks = past // MOBA_BLOCK
    assert MOBA_TOPK <= n_blocks <= LANES
    n_ring = SAMPLE_RING_PAGES
    assert page_rows % LANES == 0 and n_ring <= n_pages and (2 * n_pages) % n_ring == 0
    assert n_blocks % SAMPLE_BLOCKS_PER_ITER == 0
    n_q = H_A * dec_seq
    assert n_q <= LANES
    ck = cache_k.transpose(0, 2, 3, 1).reshape(n_pool, A_W, page_rows)
    cv = cache_v.transpose(0, 2, 3, 1).reshape(n_pool, A_W, page_rows)

    tok_spec = pl.BlockSpec((dec_seq, A_W), lambda b, pt: (b, 0))
    pool_spec = pl.BlockSpec(memory_space=pl.ANY)
    grid_spec = pltpu.PrefetchScalarGridSpec(
        num_scalar_prefetch=1,
        grid=(dec_batch,),
        in_specs=[tok_spec, tok_spec, tok_spec, pool_spec, pool_spec],
        out_specs=tok_spec,
        scratch_shapes=[pltpu.VMEM((n_ring, A_W, page_rows), F32),
                        pltpu.SemaphoreType.DMA((n_ring,)),
                        pltpu.VMEM((n_q, A_W), F32),
                        pltpu.VMEM((A_W, LANES), F32),
                        pltpu.VMEM((n_blocks, n_q, MOBA_BLOCK), F32),
                        pltpu.VMEM((LANES, LANES), F32),
                        pltpu.VMEM((n_q, LANES), F32),
                        pltpu.VMEM((n_q, A_W), F32)],
    )
    return pl.pallas_call(
        functools.partial(_moba_sample_kernel, n_pages=n_pages, n_blocks=n_blocks,
                          page_rows=page_rows, dec_seq=dec_seq),
        grid_spec=grid_spec,
        out_shape=jax.ShapeDtypeStruct((dec_batch * dec_seq, A_W), F32),
        compiler_params=_compiler_params(("arbitrary",)),
        name="moba_sample",
    )(page_table, qa, ka, va, ck, cv)


MLSTM_SEQS_PER_STEP = 4


def _mlstm_kernel(q_ref, k_ref, v_ref, o_ref, gate_ref, c0_ref, n0_ref, m0_ref,
                  h_ref, c_ref, n_ref, m_ref, *, chunk):
    step = pl.program_id(1)
    lp = MLSTM_CHUNK
    n_seq = q_ref.shape[0]

    @pl.when(step == 0)
    def _():
        c_ref[...] = c0_ref[...]
        n_ref[...] = n0_ref[...]
        m_ref[...] = m0_ref[...]

    def pad_rows(a):
        if chunk == lp:
            return a
        return jnp.concatenate([a, jnp.zeros((lp - chunk, a.shape[1]), a.dtype)], axis=0)

    src = lax.broadcasted_iota(jnp.int32, (lp, lp), 0)
    tgt = lax.broadcasted_iota(jnp.int32, (lp, lp), 1)
    lane8 = lax.broadcasted_iota(jnp.int32, (N_GATE, lp), 1)
    chains = [(b, h) for b in range(n_seq) for h in range(H_M)]
    lanes = [slice(h * DH_M, (h + 1) * DH_M) for h in range(H_M)]
    last = slice(chunk - 1, chunk)

    q_all = [pad_rows(q_ref[b]) for b in range(n_seq)]
    k_all = [pad_rows(k_ref[b]) for b in range(n_seq)]
    v_all = [pad_rows(v_ref[b]) for b in range(n_seq)]
    qb = {(b, h): q_all[b][:, lanes[h]].astype(BF16) for b, h in chains}
    kb = {(b, h): k_all[b][:, lanes[h]].astype(BF16) for b, h in chains}

    s_kq = {ch: lax.dot_general(kb[ch], qb[ch], NT_DIMS, preferred_element_type=F32) for ch in chains}
    c_q = {(b, h): lax.dot_general(c_ref[b, h].astype(BF16), qb[b, h], NT_DIMS, preferred_element_type=F32)
           for b, h in chains}
    n_q = {(b, h): lax.dot_general(jnp.broadcast_to(n_ref[b, h], (N_GATE, DH_M)).astype(BF16), qb[b, h], NT_DIMS,
                                   preferred_element_type=F32)[0:1] for b, h in chains}

    g_row, b_row_all = [], []
    for b in range(n_seq):
        g = pad_rows(gate_ref[b])
        csum = g
        shift = 1
        while shift < lp:
            csum = csum + jnp.where(src >= shift, pltpu.roll(csum, shift, 0), 0.0)
            shift *= 2
        g_row.append(g.T[:N_GATE])
        b_row_all.append(csum.T[:N_GATE])

    v_t = {(b, h): v_all[b][:, lanes[h]].T for b, h in chains}
    w_d, w_prev, m_row, decay, w_s = {}, {}, {}, {}, {}
    for b, h in chains:
        i_row, b_row = g_row[b][h:h + 1], b_row_all[b][H_M + h:H_M + h + 1]
        m_prev = m_ref[b, h]
        u_src = jnp.broadcast_to(b_row - i_row, (lp, lp)).T
        log_d = jnp.where(src <= tgt, b_row - u_src, NEG_INF)
        m_row[b, h] = jnp.maximum(m_prev + b_row, jnp.max(log_d, axis=0, keepdims=True))
        w_d[b, h] = jnp.exp(log_d - m_row[b, h])
        w_prev[b, h] = jnp.exp(m_prev + b_row - m_row[b, h])
        m_new = m_row[b, h][:, last]
        b_last = b_row[:, last]
        decay[b, h] = jnp.exp(m_prev + b_last - m_new)
        w_s[b, h] = jnp.where(lane8[0:1] < chunk, jnp.exp(b_last - b_row + i_row - m_new), 0.0)
        m_ref[b, h] = m_new

    a = {ch: s_kq[ch] * w_d[ch] for ch in chains}
    v_a = {ch: jnp.dot(v_t[ch].astype(BF16), a[ch].astype(BF16), preferred_element_type=F32) for ch in chains}
    c_upd = {ch: jnp.dot((v_t[ch] * w_s[ch]).astype(BF16), kb[ch], preferred_element_type=F32) for ch in chains}
    n_upd = {ch: jnp.dot(jnp.broadcast_to(w_s[ch], (N_GATE, lp)).astype(BF16), kb[ch],
                         preferred_element_type=F32)[0:1] for ch in chains}

    for b, h in chains:
        num = v_a[b, h] + w_prev[b, h] * c_q[b, h]
        den = jnp.sum(a[b, h], axis=0, keepdims=True) + w_prev[b, h] * n_q[b, h]
        hc = (num / jnp.maximum(jnp.abs(den), jnp.exp(-m_row[b, h]))).T
        out = jax.nn.sigmoid(pad_rows(o_ref[b])[:, lanes[h]]) * hc
        h_ref[b, :, lanes[h]] = out[:chunk].astype(h_ref.dtype)
        c_ref[b, h] = decay[b, h] * c_ref[b, h] + c_upd[b, h]
        n_ref[b, h] = decay[b, h] * n_ref[b, h] + n_upd[b, h]


def _mlstm(qm, km, vm, om, gates, c0, n0, m0, batch, seq):
    chunk = math.gcd(seq, MLSTM_CHUNK)
    n_chunks = seq // chunk
    nb = MLSTM_SEQS_PER_STEP
    assert batch % nb == 0
    h_dtype = BF16 if chunk % BF16_SUBLANES == 0 else F32
    n0_4 = n0.reshape(batch, H_M, 1, DH_M)
    m0_4 = m0.reshape(batch, H_M, 1, 1)
    per_seq = lambda a: a.reshape(batch, seq, a.shape[-1])
    row = lambda w: pl.BlockSpec((nb, chunk, w), lambda g, s: (g, s, 0))
    state = lambda a: pl.BlockSpec((nb,) + a.shape[1:], lambda g, s: (g, 0, 0, 0))
    h, c, n, m = pl.pallas_call(
        functools.partial(_mlstm_kernel, chunk=chunk),
        grid=(batch // nb, n_chunks),
        in_specs=[row(M_W), row(M_W), row(M_W), row(M_W), row(LANES), state(c0), state(n0_4), state(m0_4)],
        out_specs=(row(M_W), state(c0), state(n0_4), state(m0_4)),
        out_shape=(jax.ShapeDtypeStruct((batch, seq, M_W), h_dtype),
                   jax.ShapeDtypeStruct(c0.shape, F32),
                   jax.ShapeDtypeStruct(n0_4.shape, F32),
                   jax.ShapeDtypeStruct(m0_4.shape, F32)),
        compiler_params=_compiler_params(("parallel", "arbitrary")),
        name="mlstm",
    )(per_seq(qm), per_seq(km), per_seq(vm), per_seq(om), per_seq(gates), c0, n0_4, m0_4)
    return h.reshape(batch * seq, M_W), c, n.reshape(batch, H_M, DH_M), m.reshape(batch, H_M)


def _merge_kernel(x_ref, att_ref, ml_ref, gate_a_ref, gate_m_ref, wa_ref, wm_ref, wo_ref, g_ref, o_ref):
    ya = jnp.dot(att_ref[...].astype(BF16), wa_ref[...], preferred_element_type=F32)
    ym = jnp.dot(ml_ref[...].astype(BF16), wm_ref[...], preferred_element_type=F32)
    u = gate_a_ref[...].astype(F32) * ya + gate_m_ref[...].astype(F32) * ym
    r = jnp.dot(u.astype(BF16), wo_ref[...], preferred_element_type=F32)
    o_ref[...] = x_ref[...] + _rms(r, g_ref[...])


def _merge(x2d, att, ml, ga, gm, wa, wm, wo, g_post_mix):
    n = x2d.shape[0]
    tm = min(512, n)
    row = lambda w: pl.BlockSpec((tm, w), lambda i: (i, 0))
    full = lambda a: pl.BlockSpec(a.shape, lambda i: (0,) * a.ndim)
    g_row = g_post_mix.reshape(1, D_MODEL)
    return pl.pallas_call(
        _merge_kernel,
        grid=(n // tm,),
        in_specs=[row(D_MODEL), row(A_W), row(M_W), row(D_MODEL), row(D_MODEL),
                  full(wa), full(wm), full(wo), full(g_row)],
        out_specs=row(D_MODEL),
        out_shape=jax.ShapeDtypeStruct((n, D_MODEL), F32),
        compiler_params=_compiler_params(("parallel",)),
        name="merge",
    )(x2d, att, ml, ga, gm, wa, wm, wo, g_row)


MLP_FF_TILE = 1024


def _mlp_kernel(x_ref, g1_ref, wu_ref, wd_ref, g2_ref, o_ref):
    x = x_ref[...]
    h = _rms(x, g1_ref[...]).astype(BF16)
    n_chunks = D_FF // MLP_FF_TILE
    cols = lambda c: slice(c * MLP_FF_TILE, (c + 1) * MLP_FF_TILE)
    up = lambda c: jnp.dot(h, wu_ref[:, cols(c)], preferred_element_type=F32)
    pending = up(0)
    acc = None
    for c in range(n_chunks):
        act = jnp.square(jnp.maximum(pending, 0.0)).astype(BF16)
        if c + 1 < n_chunks:
            pending = up(c + 1)
        part = jnp.dot(act, wd_ref[cols(c), :], preferred_element_type=F32)
        acc = part if acc is None else acc + part
    o_ref[...] = x + _rms(acc, g2_ref[...])


def _mlp(x2d, g_pre_mlp, w_up, w_down, g_post_mlp):
    n = x2d.shape[0]
    tm = min(512, n)
    g1 = g_pre_mlp.reshape(1, D_MODEL)
    g2 = g_post_mlp.reshape(1, D_MODEL)
    gspec = pl.BlockSpec((1, D_MODEL), lambda i: (0, 0))
    resident = lambda a: pl.BlockSpec(a.shape, lambda i: (0, 0), pipeline_mode=pl.Buffered(1))
    return pl.pallas_call(
        _mlp_kernel,
        grid=(n // tm,),
        in_specs=[pl.BlockSpec((tm, D_MODEL), lambda i: (i, 0)), gspec, resident(w_up), resident(w_down), gspec],
        out_specs=pl.BlockSpec((tm, D_MODEL), lambda i: (i, 0)),
        out_shape=jax.ShapeDtypeStruct((n, D_MODEL), F32),
        compiler_params=_compiler_params(("parallel",)),
        name="mlp",
    )(x2d, g1, w_up, w_down, g2)


def kernel(x_prompt, x_sample, cache_k, cache_v, state_C, state_n, state_m, page_table, g_pre_mix, w_in, b_if,
           w_attn_br, w_mlstm_br, w_out, g_post_mix, g_pre_mlp, w_up, w_down, g_post_mlp):
    bp, sp, _ = x_prompt.shape
    db, ds, _ = x_sample.shape
    past = page_table.shape[1] * cache_k.shape[1]

    gate0 = 3 * A_W + 4 * M_W
    w_in_bf = w_in.astype(BF16)
    w_branch = w_in_bf[:, gate0 + N_GATE:]
    w_gate = jnp.pad(w_in_bf[:, gate0:gate0 + N_GATE], ((0, 0), (0, LANES - N_GATE)))
    b_if_row = jnp.pad(b_if.astype(F32), (0, LANES - N_GATE)).reshape(1, LANES)
    wa, wm, wo = w_attn_br.astype(BF16), w_mlstm_br.astype(BF16), w_out.astype(BF16)
    wu, wd = w_up.astype(BF16), w_down.astype(BF16)

    def tail(x2d, att, ml, ga, gm):
        x1 = _merge(x2d, att, ml, ga, gm, wa, wm, wo, g_post_mix)
        return _mlp(x1, g_pre_mlp, wu, wd, g_post_mlp)

    xp = x_prompt.reshape(bp * sp, D_MODEL)
    (qt, kt, vt, k_bf, vt_bf, kmean, qm, km, vm, om, ga, gm, gates) = _project(
        xp, jnp.arange(sp, dtype=F32), g_pre_mix, w_in_bf, w_branch, w_gate, b_if_row, feature_major=True)
    att_p = _moba_prompt(qt, k_bf, vt_bf, kmean, bp, sp)
    ml_p, c_p, n_p, m_p = _mlstm(qm, km, vm, om, gates,
                                 jnp.zeros((bp, H_M, DH_M, DH_M), F32), jnp.zeros((bp, H_M, DH_M), F32),
                                 jnp.zeros((bp, H_M), F32), bp, sp)
    y_prompt = tail(xp, att_p, ml_p, ga, gm).reshape(bp, sp, D_MODEL)
    k_prompt = kt.reshape(bp, H_A, DH_A, sp).transpose(0, 3, 1, 2)
    v_prompt = vt.reshape(bp, H_A, DH_A, sp).transpose(0, 3, 1, 2)

    xs = x_sample.reshape(db * ds, D_MODEL)
    (qa, ka, va, qm, km, vm, om, ga, gm, gates) = _project(
        xs, past + jnp.arange(ds, dtype=F32), g_pre_mix, w_in_bf, w_branch, w_gate, b_if_row, feature_major=False)
    att_s = _moba_sample(qa, ka, va, cache_k, cache_v, page_table, db, ds)
    ml_s, c_s, n_s, m_s = _mlstm(qm, km, vm, om, gates, state_C.astype(F32), state_n.astype(F32),
                                 state_m.astype(F32), db, ds)
    y_sample = tail(xs, att_s, ml_s, ga, gm).reshape(db, ds, D_MODEL)
    k_sample = ka.reshape(db, ds, H_A, DH_A)
    v_sample = va.reshape(db, ds, H_A, DH_A)

    return (y_prompt, y_sample, k_prompt, v_prompt, c_p, n_p, m_p, k_sample, v_sample, c_s, n_s, m_s)
```

```python
import functools
import math

import jax
import jax.numpy as jnp
from jax import lax
from jax.experimental import pallas as pl
from jax.experimental.pallas import tpu as pltpu

F32 = jnp.float32
BF16 = jnp.bfloat16
NEG_INF = float("-inf")
LOG2_E = math.log2(math.e)

D_MODEL = 1024
H_A = 8
DH_A = 64
A_W = H_A * DH_A
MOBA_BLOCK = 256
MOBA_TOPK = 3
ROT_DIMS = DH_A // 4
ROPE_THETA = 500000.0
H_M = 4
DH_M = 128
M_W = H_M * DH_M
MLSTM_CHUNK = 128
D_FF = 4 * D_MODEL
RMS_EPS = 1e-6
N_GATE = 2 * H_M

LANES = 128
BF16_SUBLANES = 16
HEADS_PER_LANE_GROUP = LANES // DH_A
VMEM_LIMIT_BYTES = 52 * 1024 * 1024

NT_DIMS = (((1,), (1,)), ((), ()))
TN_DIMS = (((0,), (0,)), ((), ()))


def _compiler_params(semantics):
    return pltpu.CompilerParams(dimension_semantics=semantics, vmem_limit_bytes=VMEM_LIMIT_BYTES)


def _rms(x, g):
    return x * lax.rsqrt(jnp.mean(x * x, axis=-1, keepdims=True) + RMS_EPS) * g


def _log_sigmoid(x):
    return jnp.minimum(x, 0.0) - jnp.log1p(jnp.exp(-jnp.abs(x)))


PROJ_ROWS = 2 * MOBA_BLOCK
PV_ROWS = DH_A + BF16_SUBLANES


def _project_kernel(x_ref, g_ref, w_ref, wb_ref, wg_ref, bif_ref, rc_ref, rs1_ref, rs2_ref, *out_refs,
                    feature_major):
    if feature_major:
        (qt_ref, kt_ref, vt_ref, kb_ref, vtb_ref, kmean_ref,
         qm_ref, km_ref, vm_ref, om_ref, ga_ref, gm_ref, gates_ref) = out_refs
    else:
        qa_ref, ka_ref, va_ref, qm_ref, km_ref, vm_ref, om_ref, ga_ref, gm_ref, gates_ref = out_refs
    hb = _rms(x_ref[...], g_ref[...]).astype(BF16)

    def mm(c0, n):
        return jnp.dot(hb, w_ref[:, c0:c0 + n], preferred_element_type=F32)

    rc, rs1, rs2 = rc_ref[...], rs1_ref[...], rs2_ref[...]

    def rot(z):
        half = ROT_DIMS // 2
        outs = []
        for c in range(z.shape[1] // LANES):
            zc = z[:, c * LANES:(c + 1) * LANES]
            outs.append(zc * rc + pltpu.roll(zc, LANES - half, 1) * rs1 + pltpu.roll(zc, half, 1) * rs2)
        return jnp.concatenate(outs, axis=1)

    qa = rot(mm(0, A_W))
    ka = rot(mm(A_W, A_W))
    va = mm(2 * A_W, A_W)
    if feature_major:
        qt_ref[0] = qa.T
        kt_ref[0] = ka.T
        kb_ref[...] = ka.astype(BF16)
        vt = va.T
        vt_ref[0] = vt
        ones_row = (lax.broadcasted_iota(jnp.int32, (PV_ROWS - DH_A, MOBA_BLOCK), 0) == 0).astype(BF16)
        for j in range(ka.shape[0] // MOBA_BLOCK):
            rows = slice(j * MOBA_BLOCK, (j + 1) * MOBA_BLOCK)
            kmean_ref[j] = jnp.mean(ka[rows], axis=0, keepdims=True)
            for h in range(H_A):
                vtb_ref[0, j, h] = jnp.concatenate([vt[h * DH_A:(h + 1) * DH_A, rows].astype(BF16), ones_row], axis=0)
    else:
        qa_ref[...] = qa
        ka_ref[...] = ka
        va_ref[...] = va
    c0 = 3 * A_W
    qm_ref[...] = mm(c0, M_W).astype(qm_ref.dtype)
    km_ref[...] = (mm(c0 + M_W, M_W) * (DH_M ** -0.5)).astype(km_ref.dtype)
    vm_ref[...] = mm(c0 + 2 * M_W, M_W)
    om_ref[...] = mm(c0 + 3 * M_W, M_W)
    ga_ref[...] = jax.nn.sigmoid(jnp.dot(hb, wb_ref[:, :D_MODEL], preferred_element_type=F32)).astype(ga_ref.dtype)
    gm_ref[...] = jax.nn.sigmoid(jnp.dot(hb, wb_ref[:, D_MODEL:], preferred_element_type=F32)).astype(gm_ref.dtype)
    zg = jnp.dot(hb, wg_ref[...], preferred_element_type=F32) + bif_ref[...]
    lane = lax.broadcasted_iota(jnp.int32, zg.shape, 1)
    is_forget = (lane >= H_M) & (lane < N_GATE)
    gates_ref[...] = jnp.where(is_forget, _log_sigmoid(zg), zg)


def _rotary_tables(pos):
    half = ROT_DIMS // 2
    inv = ROPE_THETA ** (-jnp.arange(half, dtype=F32) * 2.0 / ROT_DIMS)
    ang = pos[:, None] * inv[None, :]
    cos, sin = jnp.cos(ang), jnp.sin(ang)
    n = pos.shape[0]
    pad = jnp.zeros((n, DH_A - ROT_DIMS), F32)
    c_head = jnp.concatenate([cos, cos, pad + 1.0], axis=1)
    s1_head = jnp.concatenate([-sin, jnp.zeros_like(sin), pad], axis=1)
    s2_head = jnp.concatenate([jnp.zeros_like(sin), sin, pad], axis=1)
    rep = lambda t: jnp.tile(t, (1, HEADS_PER_LANE_GROUP))
    return rep(c_head), rep(s1_head), rep(s2_head)


def _project(x2d, pos, g_pre_mix, w_in_bf, w_branch, w_gate, b_if_row, feature_major):
    n = x2d.shape[0]
    tm = min(PROJ_ROWS, n)
    n_tiles = n // tm
    rc, rs1, rs2 = _rotary_tables(pos)
    if pos.shape[0] < tm:
        reps = tm // pos.shape[0]
        rc, rs1, rs2 = (jnp.tile(t, (reps, 1)) for t in (rc, rs1, rs2))
    tab_tiles = rc.shape[0] // tm
    row = lambda w: pl.BlockSpec((tm, w), lambda i: (i, 0))
    full = lambda a: pl.BlockSpec(a.shape, lambda i: (0,) * a.ndim)
    tab = pl.BlockSpec((tm, LANES), lambda i: (i % tab_tiles, 0))
    f32 = lambda w: jax.ShapeDtypeStruct((n, w), F32)
    if feature_major:
        seq = pos.shape[0]
        batch = n // seq
        t_shape = jax.ShapeDtypeStruct((batch, A_W, seq), F32)
        t_spec = pl.BlockSpec((1, A_W, tm), lambda i: (i // tab_tiles, 0, i % tab_tiles))
        bpt = tm // MOBA_BLOCK
        attn_shape = (t_shape, t_shape, t_shape, jax.ShapeDtypeStruct((n, A_W), BF16),
                      jax.ShapeDtypeStruct((batch, tab_tiles * bpt, H_A, PV_ROWS, MOBA_BLOCK), BF16),
                      jax.ShapeDtypeStruct((n_tiles * bpt, 1, A_W), F32))
        attn_specs = (t_spec, t_spec, t_spec, row(A_W),
                      pl.BlockSpec((1, bpt, H_A, PV_ROWS, MOBA_BLOCK),
                                   lambda i: (i // tab_tiles, i % tab_tiles, 0, 0, 0)),
                      pl.BlockSpec((bpt, 1, A_W), lambda i: (i, 0, 0)))
    else:
        attn_shape = (f32(A_W), f32(A_W), f32(A_W))
        attn_specs = (row(A_W), row(A_W), row(A_W))
    qk = jax.ShapeDtypeStruct((n, M_W), BF16 if feature_major else F32)
    gate = jax.ShapeDtypeStruct((n, D_MODEL), BF16)
    out_shape = attn_shape + (qk, qk, f32(M_W), f32(M_W), gate, gate, f32(LANES))
    out_specs = attn_specs + (row(M_W), row(M_W), row(M_W), row(M_W), row(D_MODEL), row(D_MODEL), row(LANES))
    g_row = g_pre_mix.reshape(1, D_MODEL)
    return pl.pallas_call(
        functools.partial(_project_kernel, feature_major=feature_major),
        grid=(n_tiles,),
        in_specs=[row(D_MODEL), full(g_row),
                  pl.BlockSpec((D_MODEL, 3 * A_W + 4 * M_W), lambda i: (0, 0), pipeline_mode=pl.Buffered(1)),
                  pl.BlockSpec(w_branch.shape, lambda i: (0, 0), pipeline_mode=pl.Buffered(1)),
                  full(w_gate), full(b_if_row), tab, tab, tab],
        out_specs=out_specs,
        out_shape=out_shape,
        compiler_params=_compiler_params(("parallel",)),
        name="project",
    )(x2d, g_row, w_in_bf, w_branch, w_gate, b_if_row, rc, rs1, rs2)


def _select_topk(scores, block_idx, n_valid, axis):
    width = scores.shape[axis]
    sc = jnp.where(block_idx < n_valid, scores, NEG_INF)
    sel = jnp.zeros(scores.shape, jnp.bool_)
    for _ in range(MOBA_TOPK):
        mx = jnp.max(sc, axis=axis, keepdims=True)
        idx = jnp.min(jnp.where(sc == mx, block_idx, width), axis=axis, keepdims=True)
        hit = block_idx == idx
        sel = sel | hit
        sc = jnp.where(hit, NEG_INF, sc)
    return jnp.where(sel & (block_idx < n_valid), 1.0, 0.0)


def _moba_prompt_kernel(qt_ref, k_ref, vt_ref, kmean_ref, o_ref,
                        qtb_ref, sel_ref, s0_ref, s1_ref, m_ref, acc_ref, ot_ref, *, n_blocks):
    i = pl.program_id(1)
    tq = MOBA_BLOCK
    scale = DH_A ** -0.5
    key = lax.broadcasted_iota(jnp.int32, (tq, tq), 0)
    qry = lax.broadcasted_iota(jnp.int32, (tq, tq), 1)
    causal = key <= qry
    feat = lax.broadcasted_iota(jnp.int32, (LANES, tq), 0)
    blk = lax.broadcasted_iota(jnp.int32, (n_blocks, tq), 0)
    n_pairs = A_W // LANES
    pair_feats = [slice(p * LANES, (p + 1) * LANES) for p in range(n_pairs)]

    heads = [(p, p * HEADS_PER_LANE_GROUP + hh) for p in range(n_pairs) for hh in range(HEADS_PER_LANE_GROUP)]

    def score(h, p, block):
        start = block * tq if isinstance(block, int) else pl.multiple_of(block * tq, tq)
        k_blk = k_ref[0, pl.ds(start, tq), pair_feats[p]]
        return jnp.dot(k_blk, qtb_ref[h], preferred_element_type=F32)

    def fold(h, p, block, s, own):
        m_old = m_ref[h]
        if own:
            s = jnp.where(causal, s, NEG_INF)
            m_new = jnp.maximum(m_old, jnp.max(s, axis=0, keepdims=True))
            m_shift = m_exp = m_new
        else:
            picked = sel_ref[h * n_blocks + block] > 0.5
            m_new = jnp.maximum(m_old, jnp.where(picked, jnp.max(s, axis=0, keepdims=True), NEG_INF))
            m_shift = jnp.where(m_new == NEG_INF, 0.0, m_new)
            m_exp = jnp.where(picked, m_shift, jnp.inf)
        alpha = jnp.exp2(m_old - m_shift)
        pe = jnp.exp2((s - m_exp).astype(BF16))
        pv = jnp.dot(vt_ref[0, block, h], pe, preferred_element_type=F32)
        m_ref[h] = m_new
        acc_ref[h] = alpha * acc_ref[h] + pv

    for p in range(n_pairs):
        qt_pair = qt_ref[0, pair_feats[p], :]
        kmean_pair = kmean_ref[0, :, pair_feats[p]]
        for hh in range(HEADS_PER_LANE_GROUP):
            h = p * HEADS_PER_LANE_GROUP + hh
            qt_head = jnp.where((feat // DH_A) == hh, qt_pair, 0.0)
            scores = jnp.dot(kmean_pair, qt_head, precision=lax.Precision.HIGHEST, preferred_element_type=F32)
            sel = _select_topk(scores, blk, i, axis=0)
            for j in range(n_blocks):
                sel_ref[h * n_blocks + j] = sel[j:j + 1, :]
            qtb_ref[h] = (qt_head * (scale * LOG2_E)).astype(BF16)
            m_ref[h] = jnp.full((1, tq), NEG_INF, F32)
            acc_ref[h] = jnp.zeros((PV_ROWS, tq), F32)

    bufs = (s0_ref, s1_ref)

    def stage(block, cur, own, score_next=True):
        for p, h in heads:
            if score_next:
                bufs[1 - cur][h] = score(h, p, block + 1)
            fold(h, p, block, bufs[cur][h], own)

    for p, h in heads:
        s0_ref[h] = score(h, p, 0)

    def body(jj, carry):
        stage(2 * jj, 0, own=False)
        stage(2 * jj + 1, 1, own=False)
        return carry

    lax.fori_loop(0, i // 2, body, 0)

    @pl.when(i % 2 == 0)
    def _():
        stage(i, 0, own=True, score_next=False)

    @pl.when(i % 2 == 1)
    def _():
        stage(i - 1, 0, own=False)
        stage(i, 1, own=True, score_next=False)

    for _, h in heads:
        ot_ref[h * DH_A:(h + 1) * DH_A, :] = acc_ref[h, :DH_A, :] / acc_ref[h, DH_A:DH_A + 1, :]
    o_ref[...] = ot_ref[...].T.astype(o_ref.dtype)


def _moba_prompt(qt, k_bf, vt_bf, kmean, batch, seq):
    n_blocks = seq // MOBA_BLOCK
    tq = MOBA_BLOCK
    k3 = k_bf.reshape(batch, seq, A_W)
    km3 = kmean.reshape(batch, n_blocks, A_W)
    return pl.pallas_call(
        functools.partial(_moba_prompt_kernel, n_blocks=n_blocks),
        grid=(batch, n_blocks),
        in_specs=[pl.BlockSpec((1, A_W, tq), lambda b, i: (b, 0, i)),
                  pl.BlockSpec((1, seq, A_W), lambda b, i: (b, 0, 0)),
                  pl.BlockSpec((1, n_blocks, H_A, PV_ROWS, tq), lambda b, i: (b, 0, 0, 0, 0)),
                  pl.BlockSpec((1, n_blocks, A_W), lambda b, i: (b, 0, 0))],
        out_specs=pl.BlockSpec((tq, A_W), lambda b, i: (b * n_blocks + i, 0)),
        out_shape=jax.ShapeDtypeStruct((batch * seq, A_W), BF16),
        scratch_shapes=[pltpu.VMEM((H_A, LANES, tq), BF16),
                        pltpu.VMEM((H_A * n_blocks, 1, tq), F32),
                        pltpu.VMEM((H_A, tq, tq), F32),
                        pltpu.VMEM((H_A, tq, tq), F32),
                        pltpu.VMEM((H_A, 1, tq), F32),
                        pltpu.VMEM((H_A, PV_ROWS, tq), F32),
                        pltpu.VMEM((A_W, tq), F32)],
        compiler_params=_compiler_params(("parallel", "arbitrary")),
        name="moba_prompt",
    )(qt, k3, vt_bf, km3)


SAMPLE_RING_PAGES = 128
SAMPLE_BLOCKS_PER_ITER = 8


def _moba_sample_kernel(pt_ref, q_ref, kn_ref, vn_ref, ck_ref, cv_ref, o_ref,
                        ring_ref, ring_sem, qbd_ref, qbdt_ref, s_ref, score_ref, bmax_ref, acc_ref,
                        *, n_pages, n_blocks, page_rows, dec_seq):
    b = pl.program_id(0)
    n_seq = pl.num_programs(0)
    n_ring = SAMPLE_RING_PAGES
    stream_len = 2 * n_pages
    ppb = MOBA_BLOCK // page_rows
    n_q = H_A * dec_seq

    def ring_copy(pool_ref, page, slot):
        return pltpu.make_async_copy(pool_ref.at[page], ring_ref.at[slot], ring_sem.at[slot])

    def start_fetch(seq, pos, keys_only=False):
        slot = pos % n_ring
        if keys_only:
            ring_copy(ck_ref, pt_ref[seq, pos], slot).start()
            return

        @pl.when(pos < n_pages)
        def _():
            ring_copy(ck_ref, pt_ref[seq, pos], slot).start()

        @pl.when(pos >= n_pages)
        def _():
            ring_copy(cv_ref, pt_ref[seq, pos - n_pages], slot).start()

    def take(pos):
        slot = pos % n_ring
        ring_copy(ck_ref, 0, slot).wait()
        return slot

    def refill(pos):
        nxt = pos + n_ring

        @pl.when(nxt < stream_len)
        def _():
            start_fetch(b, nxt)

        @pl.when((nxt >= stream_len) & (b + 1 < n_seq))
        def _():
            start_fetch(b + 1, nxt - stream_len, keys_only=True)

    @pl.when(b == 0)
    def _():
        for pos in range(n_ring):
            start_fetch(0, pos, keys_only=True)

    scale = DH_A ** -0.5
    row = lax.broadcasted_iota(jnp.int32, (n_q, A_W), 0)
    lane_w = lax.broadcasted_iota(jnp.int32, (n_q, A_W), 1)
    head_diag = (row // dec_seq) == (lane_w // DH_A)
    lane = lax.broadcasted_iota(jnp.int32, (n_q, LANES), 1)
    tok = lax.broadcasted_iota(jnp.int32, (n_q, 1), 0) % dec_seq
    score_blk = lax.broadcasted_iota(jnp.int32, (LANES, LANES), 0)

    def page_cols(t):
        return slice(t * page_rows, (t + 1) * page_rows)

    q_rep = jnp.concatenate([q_ref[...]] * H_A, axis=0)
    qbd = jnp.where(head_diag, q_rep, 0.0)
    qbd_ref[...] = qbd
    qbdt_ref[...] = jnp.concatenate([qbd, jnp.zeros((LANES - n_q, A_W), F32)], axis=0).T
    score_ref[...] = jnp.full(score_ref.shape, NEG_INF, F32)
    bmax_ref[...] = jnp.full(bmax_ref.shape, NEG_INF, F32)

    unroll = SAMPLE_BLOCKS_PER_ITER
    pages_per_iter = unroll * ppb

    def k_blocks(it, carry):
        qb = (qbd_ref[...] * scale).astype(BF16)
        qbdt = qbdt_ref[...]
        pos0 = it * pages_per_iter
        slots = [take(pos0 + t) for t in range(pages_per_iter)]
        score, bmax = score_ref[...], bmax_ref[...]
        for u in range(unroll):
            jb = it * unroll + u
            kts = [ring_ref[slots[u * ppb + t]] for t in range(ppb)]
            kmean = jnp.sum(functools.reduce(lambda x, y: x + y, kts), axis=1, keepdims=True) * (1.0 / MOBA_BLOCK)
            sc = jnp.sum(qbdt * kmean, axis=0, keepdims=True)
            score = jnp.where(score_blk == jb, sc, score)
            s_pages = [jnp.dot(qb, kt.astype(BF16), preferred_element_type=F32) for kt in kts]
            for t in range(ppb):
                s_ref[jb, :, page_cols(t)] = s_pages[t]
            s_max = jnp.max(functools.reduce(jnp.maximum, s_pages), axis=1, keepdims=True)
            bmax = jnp.where(lane == jb, s_max, bmax)
        score_ref[...], bmax_ref[...] = score, bmax
        for t in range(pages_per_iter):
            refill(pos0 + t)
        return carry

    lax.fori_loop(0, n_blocks // unroll, k_blocks, 0)

    sel = _select_topk(score_ref[...].T[:n_q], lane, n_blocks, axis=1)
    qs = qbd * scale
    kn = kn_ref[...]
    own = []
    for t in range(dec_seq):
        so = jnp.sum(qs * kn[t:t + 1, :], axis=1, keepdims=True)
        own.append(jnp.where(t <= tok, so, NEG_INF))
    m = jnp.max(jnp.where(sel > 0.5, bmax_ref[...], NEG_INF), axis=1, keepdims=True)
    m = functools.reduce(jnp.maximum, own, m)
    p_own = jnp.zeros((n_q, LANES), F32)
    for t in range(dec_seq):
        p_own = jnp.where(lane == t, jnp.exp(own[t] - m), p_own)
    acc_ref[...] = jnp.zeros(acc_ref.shape, F32)

    def v_blocks(it, lsum):
        pos0 = n_pages + it * pages_per_iter
        slots = [take(pos0 + t) for t in range(pages_per_iter)]
        weights = []
        for u in range(unroll):
            jb = it * unroll + u
            picked = jnp.sum(jnp.where(lane == jb, sel, 0.0), axis=1, keepdims=True) > 0.5
            pj = jnp.exp(jnp.where(picked, s_ref[jb] - m, NEG_INF))
            lsum = lsum + pj
            weights.append(pj.astype(BF16))
        acc = acc_ref[...]
        for u in range(unroll):
            for t in range(ppb):
                vt = ring_ref[slots[u * ppb + t]].astype(BF16)
                acc = acc + lax.dot_general(weights[u][:, page_cols(t)], vt, NT_DIMS,
                                            preferred_element_type=F32)
        acc_ref[...] = acc
        for t in range(pages_per_iter):
            refill(pos0 + t)
        return lsum

    lsum = lax.fori_loop(0, n_blocks // unroll, v_blocks, jnp.zeros((n_q, MOBA_BLOCK), F32))

    acc = acc_ref[...]
    vn = vn_ref[...]
    for t in range(dec_seq):
        acc = acc + p_own[:, t:t + 1] * vn[t:t + 1, :]
    l = jnp.sum(lsum, axis=1, keepdims=True) + jnp.sum(p_own, axis=1, keepdims=True)
    out = jnp.where(head_diag, acc / l, 0.0)
    o_ref[...] = functools.reduce(
        lambda x, y: x + y, [out[h * dec_seq:(h + 1) * dec_seq, :] for h in range(H_A)])


def _moba_sample(qa, ka, va, cache_k, cache_v, page_table, dec_batch, dec_seq):
    n_pool, page_rows = cache_k.shape[0], cache_k.shape[1]
    n_pages = page_table.shape[1]
    past = n_pages * page_rows
    assert past % MOBA_BLOCK == 0 and MOBA_BLOCK % page_rows == 0, "cached rows must fill whole MoBA blocks"
    n_blocks = past // MOBA_BLOCK
    assert MOBA_TOPK <= n_blocks <= LANES
    n_ring = SAMPLE_RING_PAGES
    assert page_rows % LANES == 0 and n_ring <= n_pages and (2 * n_pages) % n_ring == 0
    assert n_blocks % SAMPLE_BLOCKS_PER_ITER == 0
    n_q = H_A * dec_seq
    assert n_q <= LANES
    ck = cache_k.transpose(0, 2, 3, 1).reshape(n_pool, A_W, page_rows)
    cv = cache_v.transpose(0, 2, 3, 1).reshape(n_pool, A_W, page_rows)

    tok_spec = pl.BlockSpec((dec_seq, A_W), lambda b, pt: (b, 0))
    pool_spec = pl.BlockSpec(memory_space=pl.ANY)
    grid_spec = pltpu.PrefetchScalarGridSpec(
        num_scalar_prefetch=1,
        grid=(dec_batch,),
        in_specs=[tok_spec, tok_spec, tok_spec, pool_spec, pool_spec],
        out_specs=tok_spec,
        scratch_shapes=[pltpu.VMEM((n_ring, A_W, page_rows), F32),
                        pltpu.SemaphoreType.DMA((n_ring,)),
                        pltpu.VMEM((n_q, A_W), F32),
                        pltpu.VMEM((A_W, LANES), F32),
                        pltpu.VMEM((n_blocks, n_q, MOBA_BLOCK), F32),
                        pltpu.VMEM((LANES, LANES), F32),
                        pltpu.VMEM((n_q, LANES), F32),
                        pltpu.VMEM((n_q, A_W), F32)],
    )
    return pl.pallas_call(
        functools.partial(_moba_sample_kernel, n_pages=n_pages, n_blocks=n_blocks,
                          page_rows=page_rows, dec_seq=dec_seq),
        grid_spec=grid_spec,
        out_shape=jax.ShapeDtypeStruct((dec_batch * dec_seq, A_W), F32),
        compiler_params=_compiler_params(("arbitrary",)),
        name="moba_sample",
    )(page_table, qa, ka, va, ck, cv)


MLSTM_SEQS_PER_STEP = 4


def _mlstm_kernel(q_ref, k_ref, v_ref, o_ref, gate_ref, c0_ref, n0_ref, m0_ref,
                  h_ref, c_ref, n_ref, m_ref, *, chunk):
    step = pl.program_id(1)
    lp = MLSTM_CHUNK
    n_seq = q_ref.shape[0]

    @pl.when(step == 0)
    def _():
        c_ref[...] = c0_ref[...]
        n_ref[...] = n0_ref[...]
        m_ref[...] = m0_ref[...]

    def pad_rows(a):
        if chunk == lp:
            return a
        return jnp.concatenate([a, jnp.zeros((lp - chunk, a.shape[1]), a.dtype)], axis=0)

    src = lax.broadcasted_iota(jnp.int32, (lp, lp), 0)
    tgt = lax.broadcasted_iota(jnp.int32, (lp, lp), 1)
    lane8 = lax.broadcasted_iota(jnp.int32, (N_GATE, lp), 1)
    chains = [(b, h) for b in range(n_seq) for h in range(H_M)]
    lanes = [slice(h * DH_M, (h + 1) * DH_M) for h in range(H_M)]
    last = slice(chunk - 1, chunk)

    q_all = [pad_rows(q_ref[b]) for b in range(n_seq)]
    k_all = [pad_rows(k_ref[b]) for b in range(n_seq)]
    v_all = [pad_rows(v_ref[b]) for b in range(n_seq)]
    qb = {(b, h): q_all[b][:, lanes[h]].astype(BF16) for b, h in chains}
    kb = {(b, h): k_all[b][:, lanes[h]].astype(BF16) for b, h in chains}

    s_kq = {ch: lax.dot_general(kb[ch], qb[ch], NT_DIMS, preferred_element_type=F32) for ch in chains}
    c_q = {(b, h): lax.dot_general(c_ref[b, h].astype(BF16), qb[b, h], NT_DIMS, preferred_element_type=F32)
           for b, h in chains}
    n_q = {(b, h): lax.dot_general(jnp.broadcast_to(n_ref[b, h], (N_GATE, DH_M)).astype(BF16), qb[b, h], NT_DIMS,
                                   preferred_element_type=F32)[0:1] for b, h in chains}

    g_row, b_row_all = [], []
    for b in range(n_seq):
        g = pad_rows(gate_ref[b])
        csum = g
        shift = 1
        while shift < lp:
            csum = csum + jnp.where(src >= shift, pltpu.roll(csum, shift, 0), 0.0)
            shift *= 2
        g_row.append(g.T[:N_GATE])
        b_row_all.append(csum.T[:N_GATE])

    v_t = {(b, h): v_all[b][:, lanes[h]].T for b, h in chains}
    w_d, w_prev, m_row, decay, w_s = {}, {}, {}, {}, {}
    for b, h in chains:
        i_row, b_row = g_row[b][h:h + 1], b_row_all[b][H_M + h:H_M + h + 1]
        m_prev = m_ref[b, h]
        u_src = jnp.broadcast_to(b_row - i_row, (lp, lp)).T
        log_d = jnp.where(src <= tgt, b_row - u_src, NEG_INF)
        m_row[b, h] = jnp.maximum(m_prev + b_row, jnp.max(log_d, axis=0, keepdims=True))
        w_d[b, h] = jnp.exp(log_d - m_row[b, h])
        w_prev[b, h] = jnp.exp(m_prev + b_row - m_row[b, h])
        m_new = m_row[b, h][:, last]
        b_last = b_row[:, last]
        decay[b, h] = jnp.exp(m_prev + b_last - m_new)
        w_s[b, h] = jnp.where(lane8[0:1] < chunk, jnp.exp(b_last - b_row + i_row - m_new), 0.0)
        m_ref[b, h] = m_new

    a = {ch: s_kq[ch] * w_d[ch] for ch in chains}
    v_a = {ch: jnp.dot(v_t[ch].astype(BF16), a[ch].astype(BF16), preferred_element_type=F32) for ch in chains}
    c_upd = {ch: jnp.dot((v_t[ch] * w_s[ch]).astype(BF16), kb[ch], preferred_element_type=F32) for ch in chains}
    n_upd = {ch: jnp.dot(jnp.broadcast_to(w_s[ch], (N_GATE, lp)).astype(BF16), kb[ch],
                         preferred_element_type=F32)[0:1] for ch in chains}

    for b, h in chains:
        num = v_a[b, h] + w_prev[b, h] * c_q[b, h]
        den = jnp.sum(a[b, h], axis=0, keepdims=True) + w_prev[b, h] * n_q[b, h]
        hc = (num / jnp.maximum(jnp.abs(den), jnp.exp(-m_row[b, h]))).T
        out = jax.nn.sigmoid(pad_rows(o_ref[b])[:, lanes[h]]) * hc
        h_ref[b, :, lanes[h]] = out[:chunk].astype(h_ref.dtype)
        c_ref[b, h] = decay[b, h] * c_ref[b, h] + c_upd[b, h]
        n_ref[b, h] = decay[b, h] * n_ref[b, h] + n_upd[b, h]


def _mlstm(qm, km, vm, om, gates, c0, n0, m0, batch, seq):
    chunk = math.gcd(seq, MLSTM_CHUNK)
    n_chunks = seq // chunk
    nb = MLSTM_SEQS_PER_STEP
    assert batch % nb == 0
    h_dtype = BF16 if chunk % BF16_SUBLANES == 0 else F32
    n0_4 = n0.reshape(batch, H_M, 1, DH_M)
    m0_4 = m0.reshape(batch, H_M, 1, 1)
    per_seq = lambda a: a.reshape(batch, seq, a.shape[-1])
    row = lambda w: pl.BlockSpec((nb, chunk, w), lambda g, s: (g, s, 0))
    state = lambda a: pl.BlockSpec((nb,) + a.shape[1:], lambda g, s: (g, 0, 0, 0))
    h, c, n, m = pl.pallas_call(
        functools.partial(_mlstm_kernel, chunk=chunk),
        grid=(batch // nb, n_chunks),
        in_specs=[row(M_W), row(M_W), row(M_W), row(M_W), row(LANES), state(c0), state(n0_4), state(m0_4)],
        out_specs=(row(M_W), state(c0), state(n0_4), state(m0_4)),
        out_shape=(jax.ShapeDtypeStruct((batch, seq, M_W), h_dtype),
                   jax.ShapeDtypeStruct(c0.shape, F32),
                   jax.ShapeDtypeStruct(n0_4.shape, F32),
                   jax.ShapeDtypeStruct(m0_4.shape, F32)),
        compiler_params=_compiler_params(("parallel", "arbitrary")),
        name="mlstm",
    )(per_seq(qm), per_seq(km), per_seq(vm), per_seq(om), per_seq(gates), c0, n0_4, m0_4)
    return h.reshape(batch * seq, M_W), c, n.reshape(batch, H_M, DH_M), m.reshape(batch, H_M)


def _merge_kernel(x_ref, att_ref, ml_ref, gate_a_ref, gate_m_ref, wa_ref, wm_ref, wo_ref, g_ref, o_ref):
    ya = jnp.dot(att_ref[...].astype(BF16), wa_ref[...], preferred_element_type=F32)
    ym = jnp.dot(ml_ref[...].astype(BF16), wm_ref[...], preferred_element_type=F32)
    u = gate_a_ref[...].astype(F32) * ya + gate_m_ref[...].astype(F32) * ym
    r = jnp.dot(u.astype(BF16), wo_ref[...], preferred_element_type=F32)
    o_ref[...] = x_ref[...] + _rms(r, g_ref[...])


def _merge(x2d, att, ml, ga, gm, wa, wm, wo, g_post_mix):
    n = x2d.shape[0]
    tm = min(512, n)
    row = lambda w: pl.BlockSpec((tm, w), lambda i: (i, 0))
    full = lambda a: pl.BlockSpec(a.shape, lambda i: (0,) * a.ndim)
    g_row = g_post_mix.reshape(1, D_MODEL)
    return pl.pallas_call(
        _merge_kernel,
        grid=(n // tm,),
        in_specs=[row(D_MODEL), row(A_W), row(M_W), row(D_MODEL), row(D_MODEL),
                  full(wa), full(wm), full(wo), full(g_row)],
        out_specs=row(D_MODEL),
        out_shape=jax.ShapeDtypeStruct((n, D_MODEL), F32),
        compiler_params=_compiler_params(("parallel",)),
        name="merge",
    )(x2d, att, ml, ga, gm, wa, wm, wo, g_row)


MLP_FF_TILE = 1024


def _mlp_kernel(x_ref, g1_ref, wu_ref, wd_ref, g2_ref, o_ref):
    x = x_ref[...]
    h = _rms(x, g1_ref[...]).astype(BF16)
    n_chunks = D_FF // MLP_FF_TILE
    cols = lambda c: slice(c * MLP_FF_TILE, (c + 1) * MLP_FF_TILE)
    up = lambda c: jnp.dot(h, wu_ref[:, cols(c)], preferred_element_type=F32)
    pending = up(0)
    acc = None
    for c in range(n_chunks):
        act = jnp.square(jnp.maximum(pending, 0.0)).astype(BF16)
        if c + 1 < n_chunks:
            pending = up(c + 1)
        part = jnp.dot(act, wd_ref[cols(c), :], preferred_element_type=F32)
        acc = part if acc is None else acc + part
    o_ref[...] = x + _rms(acc, g2_ref[...])


def _mlp(x2d, g_pre_mlp, w_up, w_down, g_post_mlp):
    n = x2d.shape[0]
    tm = min(512, n)
    g1 = g_pre_mlp.reshape(1, D_MODEL)
    g2 = g_post_mlp.reshape(1, D_MODEL)
    gspec = pl.BlockSpec((1, D_MODEL), lambda i: (0, 0))
    resident = lambda a: pl.BlockSpec(a.shape, lambda i: (0, 0), pipeline_mode=pl.Buffered(1))
    return pl.pallas_call(
        _mlp_kernel,
        grid=(n // tm,),
        in_specs=[pl.BlockSpec((tm, D_MODEL), lambda i: (i, 0)), gspec, resident(w_up), resident(w_down), gspec],
        out_specs=pl.BlockSpec((tm, D_MODEL), lambda i: (i, 0)),
        out_shape=jax.ShapeDtypeStruct((n, D_MODEL), F32),
        compiler_params=_compiler_params(("parallel",)),
        name="mlp",
    )(x2d, g1, w_up, w_down, g2)


def kernel(x_prompt, x_sample, cache_k, cache_v, state_C, state_n, state_m, page_table, g_pre_mix, w_in, b_if,
           w_attn_br, w_mlstm_br, w_out, g_post_mix, g_pre_mlp, w_up, w_down, g_post_mlp):
    bp, sp, _ = x_prompt.shape
    db, ds, _ = x_sample.shape
    past = page_table.shape[1] * cache_k.shape[1]

    gate0 = 3 * A_W + 4 * M_W
    w_in_bf = w_in.astype(BF16)
    w_branch = w_in_bf[:, gate0 + N_GATE:]
    w_gate = jnp.pad(w_in_bf[:, gate0:gate0 + N_GATE], ((0, 0), (0, LANES - N_GATE)))
    b_if_row = jnp.pad(b_if.astype(F32), (0, LANES - N_GATE)).reshape(1, LANES)
    wa, wm, wo = w_attn_br.astype(BF16), w_mlstm_br.astype(BF16), w_out.astype(BF16)
    wu, wd = w_up.astype(BF16), w_down.astype(BF16)

    def tail(x2d, att, ml, ga, gm):
        x1 = _merge(x2d, att, ml, ga, gm, wa, wm, wo, g_post_mix)
        return _mlp(x1, g_pre_mlp, wu, wd, g_post_mlp)

    xp = x_prompt.reshape(bp * sp, D_MODEL)
    (qt, kt, vt, k_bf, vt_bf, kmean, qm, km, vm, om, ga, gm, gates) = _project(
        xp, jnp.arange(sp, dtype=F32), g_pre_mix, w_in_bf, w_branch, w_gate, b_if_row, feature_major=True)
    att_p = _moba_prompt(qt, k_bf, vt_bf, kmean, bp, sp)
    ml_p, c_p, n_p, m_p = _mlstm(qm, km, vm, om, gates,
                                 jnp.zeros((bp, H_M, DH_M, DH_M), F32), jnp.zeros((bp, H_M, DH_M), F32),
                                 jnp.zeros((bp, H_M), F32), bp, sp)
    y_prompt = tail(xp, att_p, ml_p, ga, gm).reshape(bp, sp, D_MODEL)
    k_prompt = kt.reshape(bp, H_A, DH_A, sp).transpose(0, 3, 1, 2)
    v_prompt = vt.reshape(bp, H_A, DH_A, sp).transpose(0, 3, 1, 2)

    xs = x_sample.reshape(db * ds, D_MODEL)
    (qa, ka, va, qm, km, vm, om, ga, gm, gates) = _project(
        xs, past + jnp.arange(ds, dtype=F32), g_pre_mix, w_in_bf, w_branch, w_gate, b_if_row, feature_major=False)
    att_s = _moba_sample(qa, ka, va, cache_k, cache_v, page_table, db, ds)
    ml_s, c_s, n_s, m_s = _mlstm(qm, km, vm, om, gates, state_C.astype(F32), state_n.astype(F32),
                                 state_m.astype(F32), db, ds)
    y_sample = tail(xs, att_s, ml_s, ga, gm).reshape(db, ds, D_MODEL)
    k_sample = ka.reshape(db, ds, H_A, DH_A)
    v_sample = va.reshape(db, ds, H_A, DH_A)

    return (y_prompt, y_sample, k_prompt, v_prompt, c_p, n_p, m_p, k_sample, v_sample, c_s, n_s, m_s)
```

```python
import functools
import math

import jax
import jax.numpy as jnp
from jax import lax
from jax.experimental import pallas as pl
from jax.experimental.pallas import tpu as pltpu

F32 = jnp.float32
BF16 = jnp.bfloat16
NEG_INF = float("-inf")
LOG2_E = math.log2(math.e)

D_MODEL = 1024
H_A = 8
DH_A = 64
A_W = H_A * DH_A
MOBA_BLOCK = 256
MOBA_TOPK = 3
ROT_DIMS = DH_A // 4
ROPE_THETA = 500000.0
H_M = 4
DH_M = 128
M_W = H_M * DH_M
MLSTM_CHUNK = 128
D_FF = 4 * D_MODEL
RMS_EPS = 1e-6
N_GATE = 2 * H_M

LANES = 128
BF16_SUBLANES = 16
HEADS_PER_LANE_GROUP = LANES // DH_A
VMEM_LIMIT_BYTES = 52 * 1024 * 1024

NT_DIMS = (((1,), (1,)), ((), ()))
HEAD_COLS = 3 * A_W + 4 * M_W


def _compiler_params(semantics):
    return pltpu.CompilerParams(dimension_semantics=semantics, vmem_limit_bytes=VMEM_LIMIT_BYTES)


def _rms(x, g):
    return x * lax.rsqrt(jnp.mean(x * x, axis=-1, keepdims=True) + RMS_EPS) * g


def _log_sigmoid(x):
    return jnp.minimum(x, 0.0) - jnp.log1p(jnp.exp(-jnp.abs(x)))


PROJ_ROWS = 2 * MOBA_BLOCK
PV_ROWS = DH_A + BF16_SUBLANES


def _project_kernel(x_ref, g_ref, w_ref, wb_ref, wg_ref, bif_ref, rc_ref, rs1_ref, rs2_ref, *out_refs,
                    feature_major):
    if feature_major:
        (qt_ref, kt_ref, vt_ref, kb_ref, vtb_ref, kmean_ref,
         qm_ref, km_ref, vm_ref, om_ref, ga_ref, gm_ref, gates_ref) = out_refs
    else:
        qa_ref, ka_ref, va_ref, qm_ref, km_ref, vm_ref, om_ref, ga_ref, gm_ref, gates_ref = out_refs
    hb = _rms(x_ref[...], g_ref[...]).astype(BF16)

    def mm(c0, n):
        return jnp.dot(hb, w_ref[:, c0:c0 + n], preferred_element_type=F32)

    rc, rs1, rs2 = rc_ref[...], rs1_ref[...], rs2_ref[...]

    def rot(z):
        half = ROT_DIMS // 2
        outs = []
        for c in range(z.shape[1] // LANES):
            zc = z[:, c * LANES:(c + 1) * LANES]
            outs.append(zc * rc + pltpu.roll(zc, LANES - half, 1) * rs1 + pltpu.roll(zc, half, 1) * rs2)
        return jnp.concatenate(outs, axis=1)

    qa = rot(mm(0, A_W))
    ka = rot(mm(A_W, A_W))
    va = mm(2 * A_W, A_W)
    if feature_major:
        qt_ref[0] = qa.T
        kt_ref[0] = ka.T
        kb_ref[...] = ka.astype(BF16)
        vt = va.T
        vt_ref[0] = vt
        ones_row = (lax.broadcasted_iota(jnp.int32, (PV_ROWS - DH_A, MOBA_BLOCK), 0) == 0).astype(BF16)
        for j in range(ka.shape[0] // MOBA_BLOCK):
            rows = slice(j * MOBA_BLOCK, (j + 1) * MOBA_BLOCK)
            kmean_ref[j] = jnp.mean(ka[rows], axis=0, keepdims=True)
            for h in range(H_A):
                vtb_ref[0, j, h] = jnp.concatenate([vt[h * DH_A:(h + 1) * DH_A, rows].astype(BF16), ones_row], axis=0)
    else:
        qa_ref[...] = qa
        ka_ref[...] = ka
        va_ref[...] = va
    c0 = 3 * A_W
    qm_ref[...] = mm(c0, M_W).astype(qm_ref.dtype)
    km_ref[...] = (mm(c0 + M_W, M_W) * (DH_M ** -0.5)).astype(km_ref.dtype)
    vm_ref[...] = mm(c0 + 2 * M_W, M_W)
    om_ref[...] = mm(c0 + 3 * M_W, M_W)
    ga_ref[...] = jax.nn.sigmoid(jnp.dot(hb, wb_ref[:, :D_MODEL], preferred_element_type=F32)).astype(ga_ref.dtype)
    gm_ref[...] = jax.nn.sigmoid(jnp.dot(hb, wb_ref[:, D_MODEL:], preferred_element_type=F32)).astype(gm_ref.dtype)
    zg = jnp.dot(hb, wg_ref[...], preferred_element_type=F32) + bif_ref[...]
    lane = lax.broadcasted_iota(jnp.int32, zg.shape, 1)
    is_forget = (lane >= H_M) & (lane < N_GATE)
    gates_ref[...] = jnp.where(is_forget, _log_sigmoid(zg), zg)


def _rotary_tables(pos):
    half = ROT_DIMS // 2
    inv = ROPE_THETA ** (-jnp.arange(half, dtype=F32) * 2.0 / ROT_DIMS)
    ang = pos[:, None] * inv[None, :]
    cos, sin = jnp.cos(ang), jnp.sin(ang)
    n = pos.shape[0]
    pad = jnp.zeros((n, DH_A - ROT_DIMS), F32)
    c_head = jnp.concatenate([cos, cos, pad + 1.0], axis=1)
    s1_head = jnp.concatenate([-sin, jnp.zeros_like(sin), pad], axis=1)
    s2_head = jnp.concatenate([jnp.zeros_like(sin), sin, pad], axis=1)
    rep = lambda t: jnp.tile(t, (1, HEADS_PER_LANE_GROUP))
    return rep(c_head), rep(s1_head), rep(s2_head)


def _project(x2d, pos, g_pre_mix, w_in_bf, w_branch, w_gate, b_if_row, feature_major):
    n = x2d.shape[0]
    tm = min(PROJ_ROWS, n)
    n_tiles = n // tm
    rc, rs1, rs2 = _rotary_tables(pos)
    if pos.shape[0] < tm:
        reps = tm // pos.shape[0]
        rc, rs1, rs2 = (jnp.tile(t, (reps, 1)) for t in (rc, rs1, rs2))
    tab_tiles = rc.shape[0] // tm
    row = lambda w: pl.BlockSpec((tm, w), lambda i: (i, 0))
    full = lambda a: pl.BlockSpec(a.shape, lambda i: (0,) * a.ndim)
    tab = pl.BlockSpec((tm, LANES), lambda i: (i % tab_tiles, 0))
    f32 = lambda w: jax.ShapeDtypeStruct((n, w), F32)
    if feature_major:
        seq = pos.shape[0]
        batch = n // seq
        t_shape = jax.ShapeDtypeStruct((batch, A_W, seq), F32)
        t_spec = pl.BlockSpec((1, A_W, tm), lambda i: (i // tab_tiles, 0, i % tab_tiles))
        bpt = tm // MOBA_BLOCK
        attn_shape = (t_shape, t_shape, t_shape, jax.ShapeDtypeStruct((n, A_W), BF16),
                      jax.ShapeDtypeStruct((batch, tab_tiles * bpt, H_A, PV_ROWS, MOBA_BLOCK), BF16),
                      jax.ShapeDtypeStruct((n_tiles * bpt, 1, A_W), F32))
        attn_specs = (t_spec, t_spec, t_spec, row(A_W),
                      pl.BlockSpec((1, bpt, H_A, PV_ROWS, MOBA_BLOCK),
                                   lambda i: (i // tab_tiles, i % tab_tiles, 0, 0, 0)),
                      pl.BlockSpec((bpt, 1, A_W), lambda i: (i, 0, 0)))
    else:
        attn_shape = (f32(A_W), f32(A_W), f32(A_W))
        attn_specs = (row(A_W), row(A_W), row(A_W))
    qk = jax.ShapeDtypeStruct((n, M_W), BF16 if feature_major else F32)
    gate = jax.ShapeDtypeStruct((n, D_MODEL), BF16)
    out_shape = attn_shape + (qk, qk, f32(M_W), f32(M_W), gate, gate, f32(LANES))
    out_specs = attn_specs + (row(M_W), row(M_W), row(M_W), row(M_W), row(D_MODEL), row(D_MODEL), row(LANES))
    g_row = g_pre_mix.reshape(1, D_MODEL)
    return pl.pallas_call(
        functools.partial(_project_kernel, feature_major=feature_major),
        grid=(n_tiles,),
        in_specs=[row(D_MODEL), full(g_row),
                  pl.BlockSpec((D_MODEL, HEAD_COLS), lambda i: (0, 0), pipeline_mode=pl.Buffered(1)),
                  pl.BlockSpec(w_branch.shape, lambda i: (0, 0), pipeline_mode=pl.Buffered(1)),
                  full(w_gate), full(b_if_row), tab, tab, tab],
        out_specs=out_specs,
        out_shape=out_shape,
        compiler_params=_compiler_params(("parallel",)),
        name="project",
    )(x2d, g_row, w_in_bf, w_branch, w_gate, b_if_row, rc, rs1, rs2)


def _select_topk(scores, block_idx, n_valid, axis):
    width = scores.shape[axis]
    sc = jnp.where(block_idx < n_valid, scores, NEG_INF)
    sel = jnp.zeros(scores.shape, jnp.bool_)
    for _ in range(MOBA_TOPK):
        mx = jnp.max(sc, axis=axis, keepdims=True)
        idx = jnp.min(jnp.where(sc == mx, block_idx, width), axis=axis, keepdims=True)
        hit = block_idx == idx
        sel = sel | hit
        sc = jnp.where(hit, NEG_INF, sc)
    return jnp.where(sel & (block_idx < n_valid), 1.0, 0.0)


def _moba_prompt_kernel(qt_ref, k_ref, vt_ref, kmean_ref, o_ref,
                        qtb_ref, sel_ref, s0_ref, s1_ref, m_ref, acc_ref, ot_ref, *, n_blocks):
    i = pl.program_id(1)
    tq = MOBA_BLOCK
    scale = DH_A ** -0.5
    key = lax.broadcasted_iota(jnp.int32, (tq, tq), 0)
    qry = lax.broadcasted_iota(jnp.int32, (tq, tq), 1)
    causal = key <= qry
    feat = lax.broadcasted_iota(jnp.int32, (LANES, tq), 0)
    blk = lax.broadcasted_iota(jnp.int32, (n_blocks, tq), 0)
    n_pairs = A_W // LANES
    pair_feats = [slice(p * LANES, (p + 1) * LANES) for p in range(n_pairs)]

    heads = [(p, p * HEADS_PER_LANE_GROUP + hh) for p in range(n_pairs) for hh in range(HEADS_PER_LANE_GROUP)]

    def score(h, p, block):
        start = block * tq if isinstance(block, int) else pl.multiple_of(block * tq, tq)
        k_blk = k_ref[0, pl.ds(start, tq), pair_feats[p]]
        return jnp.dot(k_blk, qtb_ref[h], preferred_element_type=F32)

    def fold(h, p, block, s, own):
        m_old = m_ref[h]
        if own:
            s = jnp.where(causal, s, NEG_INF)
            m_new = jnp.maximum(m_old, jnp.max(s, axis=0, keepdims=True))
            m_shift = m_exp = m_new
        else:
            picked = sel_ref[h * n_blocks + block] > 0.5
            m_new = jnp.maximum(m_old, jnp.where(picked, jnp.max(s, axis=0, keepdims=True), NEG_INF))
            m_shift = jnp.where(m_new == NEG_INF, 0.0, m_new)
            m_exp = jnp.where(picked, m_shift, jnp.inf)
        alpha = jnp.exp2(m_old - m_shift)
        pe = jnp.exp2((s - m_exp).astype(BF16))
        pv = jnp.dot(vt_ref[0, block, h], pe, preferred_element_type=F32)
        m_ref[h] = m_new
        acc_ref[h] = alpha * acc_ref[h] + pv

    for p in range(n_pairs):
        qt_pair = qt_ref[0, pair_feats[p], :]
        kmean_pair = kmean_ref[0, :, pair_feats[p]]
        for hh in range(HEADS_PER_LANE_GROUP):
            h = p * HEADS_PER_LANE_GROUP + hh
            qt_head = jnp.where((feat // DH_A) == hh, qt_pair, 0.0)
            scores = jnp.dot(kmean_pair, qt_head, precision=lax.Precision.HIGHEST, preferred_element_type=F32)
            sel = _select_topk(scores, blk, i, axis=0)
            for j in range(n_blocks):
                sel_ref[h * n_blocks + j] = sel[j:j + 1, :]
            qtb_ref[h] = (qt_head * (scale * LOG2_E)).astype(BF16)
            m_ref[h] = jnp.full((1, tq), NEG_INF, F32)
            acc_ref[h] = jnp.zeros((PV_ROWS, tq), F32)

    bufs = (s0_ref, s1_ref)

    def stage(block, cur, own, score_next=True):
        for p, h in heads:
            if score_next:
                bufs[1 - cur][h] = score(h, p, block + 1)
            fold(h, p, block, bufs[cur][h], own)

    for p, h in heads:
        s0_ref[h] = score(h, p, 0)

    def body(jj, carry):
        stage(2 * jj, 0, own=False)
        stage(2 * jj + 1, 1, own=False)
        return carry

    lax.fori_loop(0, i // 2, body, 0)

    @pl.when(i % 2 == 0)
    def _():
        stage(i, 0, own=True, score_next=False)

    @pl.when(i % 2 == 1)
    def _():
        stage(i - 1, 0, own=False)
        stage(i, 1, own=True, score_next=False)

    for _, h in heads:
        ot_ref[h * DH_A:(h + 1) * DH_A, :] = acc_ref[h, :DH_A, :] / acc_ref[h, DH_A:DH_A + 1, :]
    o_ref[...] = ot_ref[...].T.astype(o_ref.dtype)


def _moba_prompt(qt, k_bf, vt_bf, kmean, batch, seq):
    n_blocks = seq // MOBA_BLOCK
    tq = MOBA_BLOCK
    k3 = k_bf.reshape(batch, seq, A_W)
    km3 = kmean.reshape(batch, n_blocks, A_W)
    return pl.pallas_call(
        functools.partial(_moba_prompt_kernel, n_blocks=n_blocks),
        grid=(batch, n_blocks),
        in_specs=[pl.BlockSpec((1, A_W, tq), lambda b, i: (b, 0, i)),
                  pl.BlockSpec((1, seq, A_W), lambda b, i: (b, 0, 0)),
                  pl.BlockSpec((1, n_blocks, H_A, PV_ROWS, tq), lambda b, i: (b, 0, 0, 0, 0)),
                  pl.BlockSpec((1, n_blocks, A_W), lambda b, i: (b, 0, 0))],
        out_specs=pl.BlockSpec((tq, A_W), lambda b, i: (b * n_blocks + i, 0)),
        out_shape=jax.ShapeDtypeStruct((batch * seq, A_W), BF16),
        scratch_shapes=[pltpu.VMEM((H_A, LANES, tq), BF16),
                        pltpu.VMEM((H_A * n_blocks, 1, tq), F32),
                        pltpu.VMEM((H_A, tq, tq), F32),
                        pltpu.VMEM((H_A, tq, tq), F32),
                        pltpu.VMEM((H_A, 1, tq), F32),
                        pltpu.VMEM((H_A, PV_ROWS, tq), F32),
                        pltpu.VMEM((A_W, tq), F32)],
        compiler_params=_compiler_params(("parallel", "arbitrary")),
        name="moba_prompt",
    )(qt, k3, vt_bf, km3)


SAMPLE_RING_PAGES = 64
SAMPLE_BLOCKS_PER_ITER = 8


def _moba_sample_kernel(pt_ref, q_ref, kn_ref, vn_ref, ck_ref, cv_ref, o_ref,
                        ring_ref, ring_sem, qbd_ref, qbdt_ref, s_ref, score_ref, bmax_ref, acc_ref,
                        *, n_pages, n_blocks, page_rows, dec_seq):
    b = pl.program_id(0)
    n_seq = pl.num_programs(0)
    n_ring = SAMPLE_RING_PAGES
    stream_len = 2 * n_pages
    ppb = MOBA_BLOCK // page_rows
    n_q = H_A * dec_seq

    def ring_copy(pool_ref, page, slot):
        return pltpu.make_async_copy(pool_ref.at[page], ring_ref.at[slot], ring_sem.at[slot])

    def start_fetch(seq, pos, keys_only=False):
        slot = pos % n_ring
        if keys_only:
            ring_copy(ck_ref, pt_ref[seq, pos], slot).start()
            return

        @pl.when(pos < n_pages)
        def _():
            ring_copy(ck_ref, pt_ref[seq, pos], slot).start()

        @pl.when(pos >= n_pages)
        def _():
            ring_copy(cv_ref, pt_ref[seq, pos - n_pages], slot).start()

    def take(pos):
        slot = pos % n_ring
        ring_copy(ck_ref, 0, slot).wait()
        return slot

    def refill(pos):
        nxt = pos + n_ring

        @pl.when(nxt < stream_len)
        def _():
            start_fetch(b, nxt)

        @pl.when((nxt >= stream_len) & (b + 1 < n_seq))
        def _():
            start_fetch(b + 1, nxt - stream_len, keys_only=True)

    @pl.when(b == 0)
    def _():
        for pos in range(n_ring):
            start_fetch(0, pos, keys_only=True)

    scale = DH_A ** -0.5
    row = lax.broadcasted_iota(jnp.int32, (n_q, A_W), 0)
    lane_w = lax.broadcasted_iota(jnp.int32, (n_q, A_W), 1)
    head_diag = (row // dec_seq) == (lane_w // DH_A)
    lane = lax.broadcasted_iota(jnp.int32, (n_q, LANES), 1)
    tok = lax.broadcasted_iota(jnp.int32, (n_q, 1), 0) % dec_seq
    score_blk = lax.broadcasted_iota(jnp.int32, (LANES, LANES), 0)

    def page_cols(t):
        return slice(t * page_rows, (t + 1) * page_rows)

    q_rep = jnp.concatenate([q_ref[...]] * H_A, axis=0)
    qbd = jnp.where(head_diag, q_rep, 0.0)
    qbd_ref[...] = qbd
    qbdt_ref[...] = jnp.concatenate([qbd, jnp.zeros((LANES - n_q, A_W), F32)], axis=0).T
    score_ref[...] = jnp.full(score_ref.shape, NEG_INF, F32)
    bmax_ref[...] = jnp.full(bmax_ref.shape, NEG_INF, F32)

    unroll = SAMPLE_BLOCKS_PER_ITER
    pages_per_iter = unroll * ppb

    def k_blocks(it, carry):
        qb = (qbd_ref[...] * scale).astype(BF16)
        qbdt = qbdt_ref[...]
        pos0 = it * pages_per_iter
        slots = [take(pos0 + t) for t in range(pages_per_iter)]
        score, bmax = score_ref[...], bmax_ref[...]
        for u in range(unroll):
            jb = it * unroll + u
            kts = [ring_ref[slots[u * ppb + t]] for t in range(ppb)]
            kmean = jnp.sum(functools.reduce(lambda x, y: x + y, kts), axis=1, keepdims=True) * (1.0 / MOBA_BLOCK)
            sc = jnp.sum(qbdt * kmean, axis=0, keepdims=True)
            score = jnp.where(score_blk == jb, sc, score)
            s_pages = [jnp.dot(qb, kt.astype(BF16), preferred_element_type=F32) for kt in kts]
            for t in range(ppb):
                s_ref[jb, :, page_cols(t)] = s_pages[t]
            s_max = jnp.max(functools.reduce(jnp.maximum, s_pages), axis=1, keepdims=True)
            bmax = jnp.where(lane == jb, s_max, bmax)
        score_ref[...], bmax_ref[...] = score, bmax
        for t in range(pages_per_iter):
            refill(pos0 + t)
        return carry

    lax.fori_loop(0, n_blocks // unroll, k_blocks, 0)

    sel = _select_topk(score_ref[...].T[:n_q], lane, n_blocks, axis=1)
    qs = qbd * scale
    kn = kn_ref[...]
    own = []
    for t in range(dec_seq):
        so = jnp.sum(qs * kn[t:t + 1, :], axis=1, keepdims=True)
        own.append(jnp.where(t <= tok, so, NEG_INF))
    m = jnp.max(jnp.where(sel > 0.5, bmax_ref[...], NEG_INF), axis=1, keepdims=True)
    m = functools.reduce(jnp.maximum, own, m)
    p_own = jnp.zeros((n_q, LANES), F32)
    for t in range(dec_seq):
        p_own = jnp.where(lane == t, jnp.exp(own[t] - m), p_own)
    acc_ref[...] = jnp.zeros(acc_ref.shape, F32)

    def v_blocks(it, lsum):
        pos0 = n_pages + it * pages_per_iter
        slots = [take(pos0 + t) for t in range(pages_per_iter)]
        weights = []
        for u in range(unroll):
            jb = it * unroll + u
            picked = jnp.sum(jnp.where(lane == jb, sel, 0.0), axis=1, keepdims=True) > 0.5
            pj = jnp.exp(jnp.where(picked, s_ref[jb] - m, NEG_INF))
            lsum = lsum + pj
            weights.append(pj.astype(BF16))
        acc = acc_ref[...]
        for u in range(unroll):
            for t in range(ppb):
                vt = ring_ref[slots[u * ppb + t]].astype(BF16)
                acc = acc + lax.dot_general(weights[u][:, page_cols(t)], vt, NT_DIMS,
                                            preferred_element_type=F32)
        acc_ref[...] = acc
        for t in range(pages_per_iter):
            refill(pos0 + t)
        return lsum

    lsum = lax.fori_loop(0, n_blocks // unroll, v_blocks, jnp.zeros((n_q, MOBA_BLOCK), F32))

    acc = acc_ref[...]
    vn = vn_ref[...]
    for t in range(dec_seq):
        acc = acc + p_own[:, t:t + 1] * vn[t:t + 1, :]
    l = jnp.sum(lsum, axis=1, keepdims=True) + jnp.sum(p_own, axis=1, keepdims=True)
    out = jnp.where(head_diag, acc / l, 0.0)
    o_ref[...] = functools.reduce(
        lambda x, y: x + y, [out[h * dec_seq:(h + 1) * dec_seq, :] for h in range(H_A)])


def _moba_sample(qa, ka, va, cache_k, cache_v, page_table, dec_batch, dec_seq):
    n_pool, page_rows = cache_k.shape[0], cache_k.shape[1]
    n_pages = page_table.shape[1]
    past = n_pages * page_rows
    assert past % MOBA_BLOCK == 0 and MOBA_BLOCK % page_rows == 0, "cached rows must fill whole MoBA blocks"
    n_blocks = past // MOBA_BLOCK
    assert MOBA_TOPK <= n_blocks <= LANES
    n_ring = SAMPLE_RING_PAGES
    assert page_rows % LANES == 0 and n_ring <= n_pages and (2 * n_pages) % n_ring == 0
    assert n_blocks % SAMPLE_BLOCKS_PER_ITER == 0
    n_q = H_A * dec_seq
    assert n_q <= LANES
    ck = cache_k.transpose(0, 2, 3, 1).reshape(n_pool, A_W, page_rows)
    cv = cache_v.transpose(0, 2, 3, 1).reshape(n_pool, A_W, page_rows)

    tok_spec = pl.BlockSpec((dec_seq, A_W), lambda b, pt: (b, 0))
    pool_spec = pl.BlockSpec(memory_space=pl.ANY)
    grid_spec = pltpu.PrefetchScalarGridSpec(
        num_scalar_prefetch=1,
        grid=(dec_batch,),
        in_specs=[tok_spec, tok_spec, tok_spec, pool_spec, pool_spec],
        out_specs=tok_spec,
        scratch_shapes=[pltpu.VMEM((n_ring, A_W, page_rows), F32),
                        pltpu.SemaphoreType.DMA((n_ring,)),
                        pltpu.VMEM((n_q, A_W), F32),
                        pltpu.VMEM((A_W, LANES), F32),
                        pltpu.VMEM((n_blocks, n_q, MOBA_BLOCK), F32),
                        pltpu.VMEM((LANES, LANES), F32),
                        pltpu.VMEM((n_q, LANES), F32),
                        pltpu.VMEM((n_q, A_W), F32)],
    )
    return pl.pallas_call(
        functools.partial(_moba_sample_kernel, n_pages=n_pages, n_blocks=n_blocks,
                          page_rows=page_rows, dec_seq=dec_seq),
        grid_spec=grid_spec,
        out_shape=jax.ShapeDtypeStruct((dec_batch * dec_seq, A_W), F32),
        compiler_params=_compiler_params(("arbitrary",)),
        name="moba_sample",
    )(page_table, qa, ka, va, ck, cv)


MLSTM_SEQS_PER_STEP = 4


def _mlstm_kernel(q_ref, k_ref, v_ref, o_ref, gate_ref, c0_ref, n0_ref, m0_ref,
                  h_ref, c_ref, n_ref, m_ref, *, chunk):
    step = pl.program_id(1)
    lp = MLSTM_CHUNK
    n_seq = q_ref.shape[0]

    @pl.when(step == 0)
    def _():
        c_ref[...] = c0_ref[...]
        n_ref[...] = n0_ref[...]
        m_ref[...] = m0_ref[...]

    def pad_rows(a):
        if chunk == lp:
            return a
        return jnp.concatenate([a, jnp.zeros((lp - chunk, a.shape[1]), a.dtype)], axis=0)

    src = lax.broadcasted_iota(jnp.int32, (lp, lp), 0)
    tgt = lax.broadcasted_iota(jnp.int32, (lp, lp), 1)
    lane8 = lax.broadcasted_iota(jnp.int32, (N_GATE, lp), 1)
    chains = [(b, h) for b in range(n_seq) for h in range(H_M)]
    lanes = [slice(h * DH_M, (h + 1) * DH_M) for h in range(H_M)]
    last = slice(chunk - 1, chunk)

    q_all = [pad_rows(q_ref[b]) for b in range(n_seq)]
    k_all = [pad_rows(k_ref[b]) for b in range(n_seq)]
    v_all = [pad_rows(v_ref[b]) for b in range(n_seq)]
    qb = {(b, h): q_all[b][:, lanes[h]].astype(BF16) for b, h in chains}
    kb = {(b, h): k_all[b][:, lanes[h]].astype(BF16) for b, h in chains}

    s_kq = {ch: lax.dot_general(kb[ch], qb[ch], NT_DIMS, preferred_element_type=F32) for ch in chains}
    c_q = {(b, h): lax.dot_general(c_ref[b, h].astype(BF16), qb[b, h], NT_DIMS, preferred_element_type=F32)
           for b, h in chains}
    n_q = {(b, h): lax.dot_general(jnp.broadcast_to(n_ref[b, h], (N_GATE, DH_M)).astype(BF16), qb[b, h], NT_DIMS,
                                   preferred_element_type=F32)[0:1] for b, h in chains}

    g_row, b_row_all = [], []
    for b in range(n_seq):
        g = pad_rows(gate_ref[b])
        csum = g
        shift = 1
        while shift < lp:
            csum = csum + jnp.where(src >= shift, pltpu.roll(csum, shift, 0), 0.0)
            shift *= 2
        g_row.append(g.T[:N_GATE])
        b_row_all.append(csum.T[:N_GATE])

    v_t = {(b, h): v_all[b][:, lanes[h]].T for b, h in chains}
    w_d, w_prev, m_row, decay, w_s = {}, {}, {}, {}, {}
    for b, h in chains:
        i_row, b_row = g_row[b][h:h + 1], b_row_all[b][H_M + h:H_M + h + 1]
        m_prev = m_ref[b, h]
        u_src = jnp.broadcast_to(b_row - i_row, (lp, lp)).T
        log_d = jnp.where(src <= tgt, b_row - u_src, NEG_INF)
        m_row[b, h] = jnp.maximum(m_prev + b_row, jnp.max(log_d, axis=0, keepdims=True))
        w_d[b, h] = jnp.exp(log_d - m_row[b, h])
        w_prev[b, h] = jnp.exp(m_prev + b_row - m_row[b, h])
        m_new = m_row[b, h][:, last]
        b_last = b_row[:, last]
        decay[b, h] = jnp.exp(m_prev + b_last - m_new)
        w_s[b, h] = jnp.where(lane8[0:1] < chunk, jnp.exp(b_last - b_row + i_row - m_new), 0.0)
        m_ref[b, h] = m_new

    a = {ch: s_kq[ch] * w_d[ch] for ch in chains}
    v_a = {ch: jnp.dot(v_t[ch].astype(BF16), a[ch].astype(BF16), preferred_element_type=F32) for ch in chains}
    c_upd = {ch: jnp.dot((v_t[ch] * w_s[ch]).astype(BF16), kb[ch], preferred_element_type=F32) for ch in chains}
    n_upd = {ch: jnp.dot(jnp.broadcast_to(w_s[ch], (N_GATE, lp)).astype(BF16), kb[ch],
                         preferred_element_type=F32)[0:1] for ch in chains}

    for b, h in chains:
        num = v_a[b, h] + w_prev[b, h] * c_q[b, h]
        den = jnp.sum(a[b, h], axis=0, keepdims=True) + w_prev[b, h] * n_q[b, h]
        hc = (num / jnp.maximum(jnp.abs(den), jnp.exp(-m_row[b, h]))).T
        out = jax.nn.sigmoid(pad_rows(o_ref[b])[:, lanes[h]]) * hc
        h_ref[b, :, lanes[h]] = out[:chunk].astype(h_ref.dtype)
        c_ref[b, h] = decay[b, h] * c_ref[b, h] + c_upd[b, h]
        n_ref[b, h] = decay[b, h] * n_ref[b, h] + n_upd[b, h]


def _mlstm(qm, km, vm, om, gates, c0, n0, m0, batch, seq):
    chunk = math.gcd(seq, MLSTM_CHUNK)
    n_chunks = seq // chunk
    nb = MLSTM_SEQS_PER_STEP
    assert batch % nb == 0
    h_dtype = BF16 if chunk % BF16_SUBLANES == 0 else F32
    n0_4 = n0.reshape(batch, H_M, 1, DH_M)
    m0_4 = m0.reshape(batch, H_M, 1, 1)
    per_seq = lambda a: a.reshape(batch, seq, a.shape[-1])
    row = lambda w: pl.BlockSpec((nb, chunk, w), lambda g, s: (g, s, 0))
    state = lambda a: pl.BlockSpec((nb,) + a.shape[1:], lambda g, s: (g, 0, 0, 0))
    h, c, n, m = pl.pallas_call(
        functools.partial(_mlstm_kernel, chunk=chunk),
        grid=(batch // nb, n_chunks),
        in_specs=[row(M_W), row(M_W), row(M_W), row(M_W), row(LANES), state(c0), state(n0_4), state(m0_4)],
        out_specs=(row(M_W), state(c0), state(n0_4), state(m0_4)),
        out_shape=(jax.ShapeDtypeStruct((batch, seq, M_W), h_dtype),
                   jax.ShapeDtypeStruct(c0.shape, F32),
                   jax.ShapeDtypeStruct(n0_4.shape, F32),
                   jax.ShapeDtypeStruct(m0_4.shape, F32)),
        compiler_params=_compiler_params(("parallel", "arbitrary")),
        name="mlstm",
    )(per_seq(qm), per_seq(km), per_seq(vm), per_seq(om), per_seq(gates), c0, n0_4, m0_4)
    return h.reshape(batch * seq, M_W), c, n.reshape(batch, H_M, DH_M), m.reshape(batch, H_M)


def _merge_kernel(x_ref, att_ref, ml_ref, gate_a_ref, gate_m_ref, wa_ref, wm_ref, wo_ref, g_ref, o_ref):
    ya = jnp.dot(att_ref[...].astype(BF16), wa_ref[...], preferred_element_type=F32)
    ym = jnp.dot(ml_ref[...].astype(BF16), wm_ref[...], preferred_element_type=F32)
    u = gate_a_ref[...].astype(F32) * ya + gate_m_ref[...].astype(F32) * ym
    r = jnp.dot(u.astype(BF16), wo_ref[...], preferred_element_type=F32)
    o_ref[...] = x_ref[...] + _rms(r, g_ref[...])


def _merge(x2d, att, ml, ga, gm, wa, wm, wo, g_post_mix):
    n = x2d.shape[0]
    tm = min(512, n)
    row = lambda w: pl.BlockSpec((tm, w), lambda i: (i, 0))
    full = lambda a: pl.BlockSpec(a.shape, lambda i: (0,) * a.ndim)
    g_row = g_post_mix.reshape(1, D_MODEL)
    return pl.pallas_call(
        _merge_kernel,
        grid=(n // tm,),
        in_specs=[row(D_MODEL), row(A_W), row(M_W), row(D_MODEL), row(D_MODEL),
                  full(wa), full(wm), full(wo), full(g_row)],
        out_specs=row(D_MODEL),
        out_shape=jax.ShapeDtypeStruct((n, D_MODEL), F32),
        compiler_params=_compiler_params(("parallel",)),
        name="merge",
    )(x2d, att, ml, ga, gm, wa, wm, wo, g_row)


MLP_FF_TILE = 1024


def _mlp_kernel(x_ref, g1_ref, wu_ref, wd_ref, g2_ref, o_ref):
    x = x_ref[...]
    h = _rms(x, g1_ref[...]).astype(BF16)
    n_chunks = D_FF // MLP_FF_TILE
    cols = lambda c: slice(c * MLP_FF_TILE, (c + 1) * MLP_FF_TILE)
    up = lambda c: jnp.dot(h, wu_ref[:, cols(c)], preferred_element_type=F32)
    pending = up(0)
    acc = None
    for c in range(n_chunks):
        act = jnp.square(jnp.maximum(pending, 0.0)).astype(BF16)
        if c + 1 < n_chunks:
            pending = up(c + 1)
        part = jnp.dot(act, wd_ref[cols(c), :], preferred_element_type=F32)
        acc = part if acc is None else acc + part
    o_ref[...] = x + _rms(acc, g2_ref[...])


def _mlp(x2d, g_pre_mlp, w_up, w_down, g_post_mlp):
    n = x2d.shape[0]
    tm = min(512, n)
    g1 = g_pre_mlp.reshape(1, D_MODEL)
    g2 = g_post_mlp.reshape(1, D_MODEL)
    gspec = pl.BlockSpec((1, D_MODEL), lambda i: (0, 0))
    resident = lambda a: pl.BlockSpec(a.shape, lambda i: (0, 0), pipeline_mode=pl.Buffered(1))
    return pl.pallas_call(
        _mlp_kernel,
        grid=(n // tm,),
        in_specs=[pl.BlockSpec((tm, D_MODEL), lambda i: (i, 0)), gspec, resident(w_up), resident(w_down), gspec],
        out_specs=pl.BlockSpec((tm, D_MODEL), lambda i: (i, 0)),
        out_shape=jax.ShapeDtypeStruct((n, D_MODEL), F32),
        compiler_params=_compiler_params(("parallel",)),
        name="mlp",
    )(x2d, g1, w_up, w_down, g2)


def kernel(x_prompt, x_sample, cache_k, cache_v, state_C, state_n, state_m, page_table, g_pre_mix, w_in, b_if,
           w_attn_br, w_mlstm_br, w_out, g_post_mix, g_pre_mlp, w_up, w_down, g_post_mlp):
    bp, sp, _ = x_prompt.shape
    db, ds, _ = x_sample.shape
    past = page_table.shape[1] * cache_k.shape[1]

    w_in_bf = w_in.astype(BF16)
    w_branch = w_in_bf[:, HEAD_COLS + N_GATE:]
    w_gate = jnp.pad(w_in_bf[:, HEAD_COLS:HEAD_COLS + N_GATE], ((0, 0), (0, LANES - N_GATE)))
    b_if_row = jnp.pad(b_if.astype(F32), (0, LANES - N_GATE)).reshape(1, LANES)
    wa, wm, wo = w_attn_br.astype(BF16), w_mlstm_br.astype(BF16), w_out.astype(BF16)
    wu, wd = w_up.astype(BF16), w_down.astype(BF16)

    def tail(x2d, att, ml, ga, gm):
        x1 = _merge(x2d, att, ml, ga, gm, wa, wm, wo, g_post_mix)
        return _mlp(x1, g_pre_mlp, wu, wd, g_post_mlp)

    xp = x_prompt.reshape(bp * sp, D_MODEL)
    (qt, kt, vt, k_bf, vt_bf, kmean, qm, km, vm, om, ga, gm, gates) = _project(
        xp, jnp.arange(sp, dtype=F32), g_pre_mix, w_in_bf, w_branch, w_gate, b_if_row, feature_major=True)
    att_p = _moba_prompt(qt, k_bf, vt_bf, kmean, bp, sp)
    ml_p, c_p, n_p, m_p = _mlstm(qm, km, vm, om, gates,
                                 jnp.zeros((bp, H_M, DH_M, DH_M), F32), jnp.zeros((bp, H_M, DH_M), F32),
                                 jnp.zeros((bp, H_M), F32), bp, sp)
    y_prompt = tail(xp, att_p, ml_p, ga, gm).reshape(bp, sp, D_MODEL)
    k_prompt = kt.reshape(bp, H_A, DH_A, sp).transpose(0, 3, 1, 2)
    v_prompt = vt.reshape(bp, H_A, DH_A, sp).transpose(0, 3, 1, 2)

    xs = x_sample.reshape(db * ds, D_MODEL)
    (qa, ka, va, qm, km, vm, om, ga, gm, gates) = _project(
        xs, past + jnp.arange(ds, dtype=F32), g_pre_mix, w_in_bf, w_branch, w_gate, b_if_row, feature_major=False)
    att_s = _moba_sample(qa, ka, va, cache_k, cache_v, page_table, db, ds)
    ml_s, c_s, n_s, m_s = _mlstm(qm, km, vm, om, gates, state_C.astype(F32), state_n.astype(F32),
                                 state_m.astype(F32), db, ds)
    y_sample = tail(xs, att_s, ml_s, ga, gm).reshape(db, ds, D_MODEL)
    k_sample = ka.reshape(db, ds, H_A, DH_A)
    v_sample = va.reshape(db, ds, H_A, DH_A)

    return (y_prompt, y_sample, k_prompt, v_prompt, c_p, n_p, m_p, k_sample, v_sample, c_s, n_s, m_s)
```

```python
import functools
import math

import jax
import jax.numpy as jnp
from jax import lax
from jax.experimental import pallas as pl
from jax.experimental.pallas import tpu as pltpu

F32 = jnp.float32
BF16 = jnp.bfloat16
NEG_INF = float("-inf")
LOG2_E = math.log2(math.e)

D_MODEL = 1024
H_A = 8
DH_A = 64
A_W = H_A * DH_A
MOBA_BLOCK = 256
MOBA_TOPK = 3
ROT_DIMS = DH_A // 4
ROPE_THETA = 500000.0
H_M = 4
DH_M = 128
M_W = H_M * DH_M
MLSTM_CHUNK = 128
D_FF = 4 * D_MODEL
RMS_EPS = 1e-6
N_GATE = 2 * H_M

LANES = 128
BF16_SUBLANES = 16
HEADS_PER_LANE_GROUP = LANES // DH_A
VMEM_LIMIT_BYTES = 52 * 1024 * 1024

NT_DIMS = (((1,), (1,)), ((), ()))
HEAD_COLS = 3 * A_W + 4 * M_W


def _compiler_params(semantics):
    return pltpu.CompilerParams(dimension_semantics=semantics, vmem_limit_bytes=VMEM_LIMIT_BYTES)


def _rms(x, g):
    return x * lax.rsqrt(jnp.mean(x * x, axis=-1, keepdims=True) + RMS_EPS) * g


def _log_sigmoid(x):
    return jnp.minimum(x, 0.0) - jnp.log1p(jnp.exp(-jnp.abs(x)))


PROJ_ROWS = 2 * MOBA_BLOCK
PV_ROWS = DH_A + BF16_SUBLANES


def _project_kernel(x_ref, g_ref, w_ref, wb_ref, wg_ref, bif_ref, rc_ref, rs1_ref, rs2_ref, *out_refs,
                    feature_major):
    if feature_major:
        (qt_ref, kt_ref, vt_ref, kb_ref, vtb_ref, kmean_ref,
         qm_ref, km_ref, vm_ref, om_ref, ga_ref, gm_ref, gates_ref) = out_refs
    else:
        qa_ref, ka_ref, va_ref, qm_ref, km_ref, vm_ref, om_ref, ga_ref, gm_ref, gates_ref = out_refs
    hb = _rms(x_ref[...], g_ref[...]).astype(BF16)

    def mm(c0, n):
        return jnp.dot(hb, w_ref[:, c0:c0 + n], preferred_element_type=F32)

    rc, rs1, rs2 = rc_ref[...], rs1_ref[...], rs2_ref[...]

    def rot(z):
        half = ROT_DIMS // 2
        outs = []
        for c in range(z.shape[1] // LANES):
            zc = z[:, c * LANES:(c + 1) * LANES]
            outs.append(zc * rc + pltpu.roll(zc, LANES - half, 1) * rs1 + pltpu.roll(zc, half, 1) * rs2)
        return jnp.concatenate(outs, axis=1)

    qa = rot(mm(0, A_W))
    ka = rot(mm(A_W, A_W))
    va = mm(2 * A_W, A_W)
    if feature_major:
        qt_ref[0] = qa.T
        kt_ref[0] = ka.T
        kb_ref[...] = ka.astype(BF16)
        vt = va.T
        vt_ref[0] = vt
        ones_row = (lax.broadcasted_iota(jnp.int32, (PV_ROWS - DH_A, MOBA_BLOCK), 0) == 0).astype(BF16)
        for j in range(ka.shape[0] // MOBA_BLOCK):
            rows = slice(j * MOBA_BLOCK, (j + 1) * MOBA_BLOCK)
            kmean_ref[j] = jnp.mean(ka[rows], axis=0, keepdims=True)
            for h in range(H_A):
                vtb_ref[0, j, h] = jnp.concatenate([vt[h * DH_A:(h + 1) * DH_A, rows].astype(BF16), ones_row], axis=0)
    else:
        qa_ref[...] = qa
        ka_ref[...] = ka
        va_ref[...] = va
    c0 = 3 * A_W
    qm_ref[...] = mm(c0, M_W).astype(qm_ref.dtype)
    km_ref[...] = (mm(c0 + M_W, M_W) * (DH_M ** -0.5)).astype(km_ref.dtype)
    vm_ref[...] = mm(c0 + 2 * M_W, M_W)
    om_ref[...] = mm(c0 + 3 * M_W, M_W)
    ga_ref[...] = jax.nn.sigmoid(jnp.dot(hb, wb_ref[:, :D_MODEL], preferred_element_type=F32)).astype(ga_ref.dtype)
    gm_ref[...] = jax.nn.sigmoid(jnp.dot(hb, wb_ref[:, D_MODEL:], preferred_element_type=F32)).astype(gm_ref.dtype)
    zg = jnp.dot(hb, wg_ref[...], preferred_element_type=F32) + bif_ref[...]
    lane = lax.broadcasted_iota(jnp.int32, zg.shape, 1)
    is_forget = (lane >= H_M) & (lane < N_GATE)
    gates_ref[...] = jnp.where(is_forget, _log_sigmoid(zg), zg)


def _rotary_tables(pos):
    half = ROT_DIMS // 2
    inv = ROPE_THETA ** (-jnp.arange(half, dtype=F32) * 2.0 / ROT_DIMS)
    ang = pos[:, None] * inv[None, :]
    cos, sin = jnp.cos(ang), jnp.sin(ang)
    n = pos.shape[0]
    pad = jnp.zeros((n, DH_A - ROT_DIMS), F32)
    c_head = jnp.concatenate([cos, cos, pad + 1.0], axis=1)
    s1_head = jnp.concatenate([-sin, jnp.zeros_like(sin), pad], axis=1)
    s2_head = jnp.concatenate([jnp.zeros_like(sin), sin, pad], axis=1)
    rep = lambda t: jnp.tile(t, (1, HEADS_PER_LANE_GROUP))
    return rep(c_head), rep(s1_head), rep(s2_head)


def _project(x2d, pos, g_pre_mix, w_in_bf, w_branch, w_gate, b_if_row, feature_major):
    n = x2d.shape[0]
    tm = min(PROJ_ROWS, n)
    n_tiles = n // tm
    rc, rs1, rs2 = _rotary_tables(pos)
    if pos.shape[0] < tm:
        reps = tm // pos.shape[0]
        rc, rs1, rs2 = (jnp.tile(t, (reps, 1)) for t in (rc, rs1, rs2))
    tab_tiles = rc.shape[0] // tm
    row = lambda w: pl.BlockSpec((tm, w), lambda i: (i, 0))
    full = lambda a: pl.BlockSpec(a.shape, lambda i: (0,) * a.ndim)
    tab = pl.BlockSpec((tm, LANES), lambda i: (i % tab_tiles, 0))
    f32 = lambda w: jax.ShapeDtypeStruct((n, w), F32)
    if feature_major:
        seq = pos.shape[0]
        batch = n // seq
        t_shape = jax.ShapeDtypeStruct((batch, A_W, seq), F32)
        t_spec = pl.BlockSpec((1, A_W, tm), lambda i: (i // tab_tiles, 0, i % tab_tiles))
        bpt = tm // MOBA_BLOCK
        attn_shape = (t_shape, t_shape, t_shape, jax.ShapeDtypeStruct((n, A_W), BF16),
                      jax.ShapeDtypeStruct((batch, tab_tiles * bpt, H_A, PV_ROWS, MOBA_BLOCK), BF16),
                      jax.ShapeDtypeStruct((n_tiles * bpt, 1, A_W), F32))
        attn_specs = (t_spec, t_spec, t_spec, row(A_W),
                      pl.BlockSpec((1, bpt, H_A, PV_ROWS, MOBA_BLOCK),
                                   lambda i: (i // tab_tiles, i % tab_tiles, 0, 0, 0)),
                      pl.BlockSpec((bpt, 1, A_W), lambda i: (i, 0, 0)))
    else:
        attn_shape = (f32(A_W), f32(A_W), f32(A_W))
        attn_specs = (row(A_W), row(A_W), row(A_W))
    qk = jax.ShapeDtypeStruct((n, M_W), BF16 if feature_major else F32)
    gate = jax.ShapeDtypeStruct((n, D_MODEL), BF16)
    out_shape = attn_shape + (qk, qk, f32(M_W), f32(M_W), gate, gate, f32(LANES))
    out_specs = attn_specs + (row(M_W), row(M_W), row(M_W), row(M_W), row(D_MODEL), row(D_MODEL), row(LANES))
    g_row = g_pre_mix.reshape(1, D_MODEL)
    return pl.pallas_call(
        functools.partial(_project_kernel, feature_major=feature_major),
        grid=(n_tiles,),
        in_specs=[row(D_MODEL), full(g_row),
                  pl.BlockSpec((D_MODEL, HEAD_COLS), lambda i: (0, 0), pipeline_mode=pl.Buffered(1)),
                  pl.BlockSpec(w_branch.shape, lambda i: (0, 0), pipeline_mode=pl.Buffered(1)),
                  full(w_gate), full(b_if_row), tab, tab, tab],
        out_specs=out_specs,
        out_shape=out_shape,
        compiler_params=_compiler_params(("parallel",)),
        name="project",
    )(x2d, g_row, w_in_bf, w_branch, w_gate, b_if_row, rc, rs1, rs2)


def _select_topk(scores, block_idx, n_valid, axis):
    width = scores.shape[axis]
    sc = jnp.where(block_idx < n_valid, scores, NEG_INF)
    sel = jnp.zeros(scores.shape, jnp.bool_)
    for _ in range(MOBA_TOPK):
        mx = jnp.max(sc, axis=axis, keepdims=True)
        idx = jnp.min(jnp.where(sc == mx, block_idx, width), axis=axis, keepdims=True)
        hit = block_idx == idx
        sel = sel | hit
        sc = jnp.where(hit, NEG_INF, sc)
    return jnp.where(sel & (block_idx < n_valid), 1.0, 0.0)


FOLD_LAG = 1

def _moba_prompt_kernel(qt_ref, k_ref, vt_ref, kmean_ref, o_ref,
                        qtb_ref, sel_ref, s0_ref, s1_ref, m_ref, acc_ref, ot_ref, *, n_blocks):
    i = pl.program_id(1)
    tq = MOBA_BLOCK
    scale = DH_A ** -0.5
    key = lax.broadcasted_iota(jnp.int32, (tq, tq), 0)
    qry = lax.broadcasted_iota(jnp.int32, (tq, tq), 1)
    causal = key <= qry
    feat = lax.broadcasted_iota(jnp.int32, (LANES, tq), 0)
    blk = lax.broadcasted_iota(jnp.int32, (n_blocks, tq), 0)
    n_pairs = A_W // LANES
    pair_feats = [slice(p * LANES, (p + 1) * LANES) for p in range(n_pairs)]

    heads = [(p, p * HEADS_PER_LANE_GROUP + hh) for p in range(n_pairs) for hh in range(HEADS_PER_LANE_GROUP)]

    def score(h, p, block):
        start = block * tq if isinstance(block, int) else pl.multiple_of(block * tq, tq)
        k_blk = k_ref[0, pl.ds(start, tq), pair_feats[p]]
        return jnp.dot(k_blk, qtb_ref[h], preferred_element_type=F32)

    def fold(h, p, block, s, own):
        m_old = m_ref[h]
        if own:
            s = jnp.where(causal, s, NEG_INF)
            m_new = jnp.maximum(m_old, jnp.max(s, axis=0, keepdims=True))
            m_shift = m_exp = m_new
        else:
            picked = sel_ref[h * n_blocks + block] > 0.5
            m_new = jnp.maximum(m_old, jnp.where(picked, jnp.max(s, axis=0, keepdims=True), NEG_INF))
            m_shift = jnp.where(m_new == NEG_INF, 0.0, m_new)
            m_exp = jnp.where(picked, m_shift, jnp.inf)
        alpha = jnp.exp2(m_old - m_shift)
        pe = jnp.exp2((s - m_exp).astype(BF16))
        pv = jnp.dot(vt_ref[0, block, h], pe, preferred_element_type=F32)
        m_ref[h] = m_new
        acc_ref[h] = alpha * acc_ref[h] + pv

    for p in range(n_pairs):
        qt_pair = qt_ref[0, pair_feats[p], :]
        kmean_pair = kmean_ref[0, :, pair_feats[p]]
        for hh in range(HEADS_PER_LANE_GROUP):
            h = p * HEADS_PER_LANE_GROUP + hh
            qt_head = jnp.where((feat // DH_A) == hh, qt_pair, 0.0)
            scores = jnp.dot(kmean_pair, qt_head, precision=lax.Precision.HIGHEST, preferred_element_type=F32)
            sel = _select_topk(scores, blk, i, axis=0)
            for j in range(n_blocks):
                sel_ref[h * n_blocks + j] = sel[j:j + 1, :]
            qtb_ref[h] = (qt_head * (scale * LOG2_E)).astype(BF16)
            m_ref[h] = jnp.full((1, tq), NEG_INF, F32)
            acc_ref[h] = jnp.zeros((PV_ROWS, tq), F32)

    bufs = (s0_ref, s1_ref)

    def stage(block, cur, own, score_next=True):
        for idx, (p, h) in enumerate(heads):
            if score_next:
                bufs[1 - cur][h] = score(h, p, block + 1)
            if idx >= FOLD_LAG:
                p_f, h_f = heads[idx - FOLD_LAG]
                fold(h_f, p_f, block, bufs[cur][h_f], own)
        for p_f, h_f in heads[len(heads) - FOLD_LAG:]:
            fold(h_f, p_f, block, bufs[cur][h_f], own)

    for p, h in heads:
        s0_ref[h] = score(h, p, 0)

    def body(jj, carry):
        stage(2 * jj, 0, own=False)
        stage(2 * jj + 1, 1, own=False)
        return carry

    lax.fori_loop(0, i // 2, body, 0)

    @pl.when(i % 2 == 0)
    def _():
        stage(i, 0, own=True, score_next=False)

    @pl.when(i % 2 == 1)
    def _():
        stage(i - 1, 0, own=False)
        stage(i, 1, own=True, score_next=False)

    for _, h in heads:
        ot_ref[h * DH_A:(h + 1) * DH_A, :] = acc_ref[h, :DH_A, :] / acc_ref[h, DH_A:DH_A + 1, :]
    o_ref[...] = ot_ref[...].T.astype(o_ref.dtype)


def _moba_prompt(qt, k_bf, vt_bf, kmean, batch, seq):
    n_blocks = seq // MOBA_BLOCK
    tq = MOBA_BLOCK
    k3 = k_bf.reshape(batch, seq, A_W)
    km3 = kmean.reshape(batch, n_blocks, A_W)
    return pl.pallas_call(
        functools.partial(_moba_prompt_kernel, n_blocks=n_blocks),
        grid=(batch, n_blocks),
        in_specs=[pl.BlockSpec((1, A_W, tq), lambda b, i: (b, 0, i)),
                  pl.BlockSpec((1, seq, A_W), lambda b, i: (b, 0, 0)),
                  pl.BlockSpec((1, n_blocks, H_A, PV_ROWS, tq), lambda b, i: (b, 0, 0, 0, 0)),
                  pl.BlockSpec((1, n_blocks, A_W), lambda b, i: (b, 0, 0))],
        out_specs=pl.BlockSpec((tq, A_W), lambda b, i: (b * n_blocks + i, 0)),
        out_shape=jax.ShapeDtypeStruct((batch * seq, A_W), BF16),
        scratch_shapes=[pltpu.VMEM((H_A, LANES, tq), BF16),
                        pltpu.VMEM((H_A * n_blocks, 1, tq), F32),
                        pltpu.VMEM((H_A, tq, tq), F32),
                        pltpu.VMEM((H_A, tq, tq), F32),
                        pltpu.VMEM((H_A, 1, tq), F32),
                        pltpu.VMEM((H_A, PV_ROWS, tq), F32),
                        pltpu.VMEM((A_W, tq), F32)],
        compiler_params=_compiler_params(("parallel", "arbitrary")),
        name="moba_prompt",
    )(qt, k3, vt_bf, km3)


SAMPLE_RING_PAGES = 64
SAMPLE_BLOCKS_PER_ITER = 8


def _moba_sample_kernel(pt_ref, q_ref, kn_ref, vn_ref, ck_ref, cv_ref, o_ref,
                        ring_ref, ring_sem, qbd_ref, qbdt_ref, s_ref, score_ref, bmax_ref, acc_ref,
                        *, n_pages, n_blocks, page_rows, dec_seq):
    b = pl.program_id(0)
    n_seq = pl.num_programs(0)
    n_ring = SAMPLE_RING_PAGES
    stream_len = 2 * n_pages
    ppb = MOBA_BLOCK // page_rows
    n_q = H_A * dec_seq

    def ring_copy(pool_ref, page, slot):
        return pltpu.make_async_copy(pool_ref.at[page], ring_ref.at[slot], ring_sem.at[slot])

    def start_fetch(seq, pos, keys_only=False):
        slot = pos % n_ring
        if keys_only:
            ring_copy(ck_ref, pt_ref[seq, pos], slot).start()
            return

        @pl.when(pos < n_pages)
        def _():
            ring_copy(ck_ref, pt_ref[seq, pos], slot).start()

        @pl.when(pos >= n_pages)
        def _():
            ring_copy(cv_ref, pt_ref[seq, pos - n_pages], slot).start()

    def take(pos):
        slot = pos % n_ring
        ring_copy(ck_ref, 0, slot).wait()
        return slot

    def refill(pos):
        nxt = pos + n_ring

        @pl.when(nxt < stream_len)
        def _():
            start_fetch(b, nxt)

        @pl.when((nxt >= stream_len) & (b + 1 < n_seq))
        def _():
            start_fetch(b + 1, nxt - stream_len, keys_only=True)

    @pl.when(b == 0)
    def _():
        for pos in range(n_ring):
            start_fetch(0, pos, keys_only=True)

    scale = DH_A ** -0.5
    row = lax.broadcasted_iota(jnp.int32, (n_q, A_W), 0)
    lane_w = lax.broadcasted_iota(jnp.int32, (n_q, A_W), 1)
    head_diag = (row // dec_seq) == (lane_w // DH_A)
    lane = lax.broadcasted_iota(jnp.int32, (n_q, LANES), 1)
    tok = lax.broadcasted_iota(jnp.int32, (n_q, 1), 0) % dec_seq
    score_blk = lax.broadcasted_iota(jnp.int32, (LANES, LANES), 0)

    def page_cols(t):
        return slice(t * page_rows, (t + 1) * page_rows)

    q_rep = jnp.concatenate([q_ref[...]] * H_A, axis=0)
    qbd = jnp.where(head_diag, q_rep, 0.0)
    qbd_ref[...] = qbd
    qbdt_ref[...] = jnp.concatenate([qbd, jnp.zeros((LANES - n_q, A_W), F32)], axis=0).T
    score_ref[...] = jnp.full(score_ref.shape, NEG_INF, F32)
    bmax_ref[...] = jnp.full(bmax_ref.shape, NEG_INF, F32)

    unroll = SAMPLE_BLOCKS_PER_ITER
    pages_per_iter = unroll * ppb

    def k_blocks(it, carry):
        qb = (qbd_ref[...] * scale).astype(BF16)
        qbdt = qbdt_ref[...]
        pos0 = it * pages_per_iter
        slots = [take(pos0 + t) for t in range(pages_per_iter)]
        score, bmax = score_ref[...], bmax_ref[...]
        for u in range(unroll):
            jb = it * unroll + u
            kts = [ring_ref[slots[u * ppb + t]] for t in range(ppb)]
            kmean = jnp.sum(functools.reduce(lambda x, y: x + y, kts), axis=1, keepdims=True) * (1.0 / MOBA_BLOCK)
            sc = jnp.sum(qbdt * kmean, axis=0, keepdims=True)
            score = jnp.where(score_blk == jb, sc, score)
            s_pages = [jnp.dot(qb, kt.astype(BF16), preferred_element_type=F32) for kt in kts]
            for t in range(ppb):
                s_ref[jb, :, page_cols(t)] = s_pages[t]
            s_max = jnp.max(functools.reduce(jnp.maximum, s_pages), axis=1, keepdims=True)
            bmax = jnp.where(lane == jb, s_max, bmax)
        score_ref[...], bmax_ref[...] = score, bmax
        for t in range(pages_per_iter):
            refill(pos0 + t)
        return carry

    lax.fori_loop(0, n_blocks // unroll, k_blocks, 0)

    sel = _select_topk(score_ref[...].T[:n_q], lane, n_blocks, axis=1)
    qs = qbd * scale
    kn = kn_ref[...]
    own = []
    for t in range(dec_seq):
        so = jnp.sum(qs * kn[t:t + 1, :], axis=1, keepdims=True)
        own.append(jnp.where(t <= tok, so, NEG_INF))
    m = jnp.max(jnp.where(sel > 0.5, bmax_ref[...], NEG_INF), axis=1, keepdims=True)
    m = functools.reduce(jnp.maximum, own, m)
    p_own = jnp.zeros((n_q, LANES), F32)
    for t in range(dec_seq):
        p_own = jnp.where(lane == t, jnp.exp(own[t] - m), p_own)
    acc_ref[...] = jnp.zeros(acc_ref.shape, F32)

    def v_blocks(it, lsum):
        pos0 = n_pages + it * pages_per_iter
        slots = [take(pos0 + t) for t in range(pages_per_iter)]
        weights = []
        for u in range(unroll):
            jb = it * unroll + u
            picked = jnp.sum(jnp.where(lane == jb, sel, 0.0), axis=1, keepdims=True) > 0.5
            pj = jnp.exp(jnp.where(picked, s_ref[jb] - m, NEG_INF))
            lsum = lsum + pj
            weights.append(pj.astype(BF16))
        acc = acc_ref[...]
        for u in range(unroll):
            for t in range(ppb):
                vt = ring_ref[slots[u * ppb + t]].astype(BF16)
                acc = acc + lax.dot_general(weights[u][:, page_cols(t)], vt, NT_DIMS,
                                            preferred_element_type=F32)
        acc_ref[...] = acc
        for t in range(pages_per_iter):
            refill(pos0 + t)
        return lsum

    lsum = lax.fori_loop(0, n_blocks // unroll, v_blocks, jnp.zeros((n_q, MOBA_BLOCK), F32))

    acc = acc_ref[...]
    vn = vn_ref[...]
    for t in range(dec_seq):
        acc = acc + p_own[:, t:t + 1] * vn[t:t + 1, :]
    l = jnp.sum(lsum, axis=1, keepdims=True) + jnp.sum(p_own, axis=1, keepdims=True)
    out = jnp.where(head_diag, acc / l, 0.0)
    o_ref[...] = functools.reduce(
        lambda x, y: x + y, [out[h * dec_seq:(h + 1) * dec_seq, :] for h in range(H_A)])


def _moba_sample(qa, ka, va, cache_k, cache_v, page_table, dec_batch, dec_seq):
    n_pool, page_rows = cache_k.shape[0], cache_k.shape[1]
    n_pages = page_table.shape[1]
    past = n_pages * page_rows
    assert past % MOBA_BLOCK == 0 and MOBA_BLOCK % page_rows == 0, "cached rows must fill whole MoBA blocks"
    n_blocks = past // MOBA_BLOCK
    assert MOBA_TOPK <= n_blocks <= LANES
    n_ring = SAMPLE_RING_PAGES
    assert page_rows % LANES == 0 and n_ring <= n_pages and (2 * n_pages) % n_ring == 0
    assert n_blocks % SAMPLE_BLOCKS_PER_ITER == 0
    n_q = H_A * dec_seq
    assert n_q <= LANES
    ck = cache_k.transpose(0, 2, 3, 1).reshape(n_pool, A_W, page_rows)
    cv = cache_v.transpose(0, 2, 3, 1).reshape(n_pool, A_W, page_rows)

    tok_spec = pl.BlockSpec((dec_seq, A_W), lambda b, pt: (b, 0))
    pool_spec = pl.BlockSpec(memory_space=pl.ANY)
    grid_spec = pltpu.PrefetchScalarGridSpec(
        num_scalar_prefetch=1,
        grid=(dec_batch,),
        in_specs=[tok_spec, tok_spec, tok_spec, pool_spec, pool_spec],
        out_specs=tok_spec,
        scratch_shapes=[pltpu.VMEM((n_ring, A_W, page_rows), F32),
                        pltpu.SemaphoreType.DMA((n_ring,)),
                        pltpu.VMEM((n_q, A_W), F32),
                        pltpu.VMEM((A_W, LANES), F32),
                        pltpu.VMEM((n_blocks, n_q, MOBA_BLOCK), F32),
                        pltpu.VMEM((LANES, LANES), F32),
                        pltpu.VMEM((n_q, LANES), F32),
                        pltpu.VMEM((n_q, A_W), F32)],
    )
    return pl.pallas_call(
        functools.partial(_moba_sample_kernel, n_pages=n_pages, n_blocks=n_blocks,
                          page_rows=page_rows, dec_seq=dec_seq),
        grid_spec=grid_spec,
        out_shape=jax.ShapeDtypeStruct((dec_batch * dec_seq, A_W), F32),
        compiler_params=_compiler_params(("arbitrary",)),
        name="moba_sample",
    )(page_table, qa, ka, va, ck, cv)


MLSTM_SEQS_PER_STEP = 4


def _mlstm_kernel(q_ref, k_ref, v_ref, o_ref, gate_ref, c0_ref, n0_ref, m0_ref,
                  h_ref, c_ref, n_ref, m_ref, *, chunk):
    step = pl.program_id(1)
    lp = MLSTM_CHUNK
    n_seq = q_ref.shape[0]

    @pl.when(step == 0)
    def _():
        c_ref[...] = c0_ref[...]
        n_ref[...] = n0_ref[...]
        m_ref[...] = m0_ref[...]

    def pad_rows(a):
        if chunk == lp:
            return a
        return jnp.concatenate([a, jnp.zeros((lp - chunk, a.shape[1]), a.dtype)], axis=0)

    src = lax.broadcasted_iota(jnp.int32, (lp, lp), 0)
    tgt = lax.broadcasted_iota(jnp.int32, (lp, lp), 1)
    lane8 = lax.broadcasted_iota(jnp.int32, (N_GATE, lp), 1)
    chains = [(b, h) for b in range(n_seq) for h in range(H_M)]
    lanes = [slice(h * DH_M, (h + 1) * DH_M) for h in range(H_M)]
    last = slice(chunk - 1, chunk)

    q_all = [pad_rows(q_ref[b]) for b in range(n_seq)]
    k_all = [pad_rows(k_ref[b]) for b in range(n_seq)]
    v_all = [pad_rows(v_ref[b]) for b in range(n_seq)]
    qb = {(b, h): q_all[b][:, lanes[h]].astype(BF16) for b, h in chains}
    kb = {(b, h): k_all[b][:, lanes[h]].astype(BF16) for b, h in chains}

    s_kq = {ch: lax.dot_general(kb[ch], qb[ch], NT_DIMS, preferred_element_type=F32) for ch in chains}
    c_q = {(b, h): lax.dot_general(c_ref[b, h].astype(BF16), qb[b, h], NT_DIMS, preferred_element_type=F32)
           for b, h in chains}
    n_q = {(b, h): lax.dot_general(jnp.broadcast_to(n_ref[b, h], (N_GATE, DH_M)).astype(BF16), qb[b, h], NT_DIMS,
                                   preferred_element_type=F32)[0:1] for b, h in chains}

    g_row, b_row_all = [], []
    for b in range(n_seq):
        g = pad_rows(gate_ref[b])
        csum = g
        shift = 1
        while shift < lp:
            csum = csum + jnp.where(src >= shift, pltpu.roll(csum, shift, 0), 0.0)
            shift *= 2
        g_row.append(g.T[:N_GATE])
        b_row_all.append(csum.T[:N_GATE])

    v_t = {(b, h): v_all[b][:, lanes[h]].T for b, h in chains}
    w_d, w_prev, m_row, decay, w_s = {}, {}, {}, {}, {}
    for b, h in chains:
        i_row, b_row = g_row[b][h:h + 1], b_row_all[b][H_M + h:H_M + h + 1]
        m_prev = m_ref[b, h]
        u_src = jnp.broadcast_to(b_row - i_row, (lp, lp)).T
        log_d = jnp.where(src <= tgt, b_row - u_src, NEG_INF)
        m_row[b, h] = jnp.maximum(m_prev + b_row, jnp.max(log_d, axis=0, keepdims=True))
        w_d[b, h] = jnp.exp(log_d - m_row[b, h])
        w_prev[b, h] = jnp.exp(m_prev + b_row - m_row[b, h])
        m_new = m_row[b, h][:, last]
        b_last = b_row[:, last]
        decay[b, h] = jnp.exp(m_prev + b_last - m_new)
        w_s[b, h] = jnp.where(lane8[0:1] < chunk, jnp.exp(b_last - b_row + i_row - m_new), 0.0)
        m_ref[b, h] = m_new

    a = {ch: s_kq[ch] * w_d[ch] for ch in chains}
    v_a = {ch: jnp.dot(v_t[ch].astype(BF16), a[ch].astype(BF16), preferred_element_type=F32) for ch in chains}
    c_upd = {ch: jnp.dot((v_t[ch] * w_s[ch]).astype(BF16), kb[ch], preferred_element_type=F32) for ch in chains}
    n_upd = {ch: jnp.dot(jnp.broadcast_to(w_s[ch], (N_GATE, lp)).astype(BF16), kb[ch],
                         preferred_element_type=F32)[0:1] for ch in chains}

    for b, h in chains:
        num = v_a[b, h] + w_prev[b, h] * c_q[b, h]
        den = jnp.sum(a[b, h], axis=0, keepdims=True) + w_prev[b, h] * n_q[b, h]
        hc = (num / jnp.maximum(jnp.abs(den), jnp.exp(-m_row[b, h]))).T
        out = jax.nn.sigmoid(pad_rows(o_ref[b])[:, lanes[h]]) * hc
        h_ref[b, :, lanes[h]] = out[:chunk].astype(h_ref.dtype)
        c_ref[b, h] = decay[b, h] * c_ref[b, h] + c_upd[b, h]
        n_ref[b, h] = decay[b, h] * n_ref[b, h] + n_upd[b, h]


def _mlstm(qm, km, vm, om, gates, c0, n0, m0, batch, seq):
    chunk = math.gcd(seq, MLSTM_CHUNK)
    n_chunks = seq // chunk
    nb = MLSTM_SEQS_PER_STEP
    assert batch % nb == 0
    h_dtype = BF16 if chunk % BF16_SUBLANES == 0 else F32
    n0_4 = n0.reshape(batch, H_M, 1, DH_M)
    m0_4 = m0.reshape(batch, H_M, 1, 1)
    per_seq = lambda a: a.reshape(batch, seq, a.shape[-1])
    row = lambda w: pl.BlockSpec((nb, chunk, w), lambda g, s: (g, s, 0))
    state = lambda a: pl.BlockSpec((nb,) + a.shape[1:], lambda g, s: (g, 0, 0, 0))
    h, c, n, m = pl.pallas_call(
        functools.partial(_mlstm_kernel, chunk=chunk),
        grid=(batch // nb, n_chunks),
        in_specs=[row(M_W), row(M_W), row(M_W), row(M_W), row(LANES), state(c0), state(n0_4), state(m0_4)],
        out_specs=(row(M_W), state(c0), state(n0_4), state(m0_4)),
        out_shape=(jax.ShapeDtypeStruct((batch, seq, M_W), h_dtype),
                   jax.ShapeDtypeStruct(c0.shape, F32),
                   jax.ShapeDtypeStruct(n0_4.shape, F32),
                   jax.ShapeDtypeStruct(m0_4.shape, F32)),
        compiler_params=_compiler_params(("parallel", "arbitrary")),
        name="mlstm",
    )(per_seq(qm), per_seq(km), per_seq(vm), per_seq(om), per_seq(gates), c0, n0_4, m0_4)
    return h.reshape(batch * seq, M_W), c, n.reshape(batch, H_M, DH_M), m.reshape(batch, H_M)


def _merge_kernel(x_ref, att_ref, ml_ref, gate_a_ref, gate_m_ref, wa_ref, wm_ref, wo_ref, g_ref, o_ref):
    ya = jnp.dot(att_ref[...].astype(BF16), wa_ref[...], preferred_element_type=F32)
    ym = jnp.dot(ml_ref[...].astype(BF16), wm_ref[...], preferred_element_type=F32)
    u = gate_a_ref[...].astype(F32) * ya + gate_m_ref[...].astype(F32) * ym
    r = jnp.dot(u.astype(BF16), wo_ref[...], preferred_element_type=F32)
    o_ref[...] = x_ref[...] + _rms(r, g_ref[...])


def _merge(x2d, att, ml, ga, gm, wa, wm, wo, g_post_mix):
    n = x2d.shape[0]
    tm = min(512, n)
    row = lambda w: pl.BlockSpec((tm, w), lambda i: (i, 0))
    full = lambda a: pl.BlockSpec(a.shape, lambda i: (0,) * a.ndim)
    g_row = g_post_mix.reshape(1, D_MODEL)
    return pl.pallas_call(
        _merge_kernel,
        grid=(n // tm,),
        in_specs=[row(D_MODEL), row(A_W), row(M_W), row(D_MODEL), row(D_MODEL),
                  full(wa), full(wm), full(wo), full(g_row)],
        out_specs=row(D_MODEL),
        out_shape=jax.ShapeDtypeStruct((n, D_MODEL), F32),
        compiler_params=_compiler_params(("parallel",)),
        name="merge",
    )(x2d, att, ml, ga, gm, wa, wm, wo, g_row)


MLP_FF_TILE = 1024


def _mlp_kernel(x_ref, g1_ref, wu_ref, wd_ref, g2_ref, o_ref):
    x = x_ref[...]
    h = _rms(x, g1_ref[...]).astype(BF16)
    n_chunks = D_FF // MLP_FF_TILE
    cols = lambda c: slice(c * MLP_FF_TILE, (c + 1) * MLP_FF_TILE)
    up = lambda c: jnp.dot(h, wu_ref[:, cols(c)], preferred_element_type=F32)
    pending = up(0)
    acc = None
    for c in range(n_chunks):
        act = jnp.square(jnp.maximum(pending, 0.0)).astype(BF16)
        if c + 1 < n_chunks:
            pending = up(c + 1)
        part = jnp.dot(act, wd_ref[cols(c), :], preferred_element_type=F32)
        acc = part if acc is None else acc + part
    o_ref[...] = x + _rms(acc, g2_ref[...])


def _mlp(x2d, g_pre_mlp, w_up, w_down, g_post_mlp):
    n = x2d.shape[0]
    tm = min(512, n)
    g1 = g_pre_mlp.reshape(1, D_MODEL)
    g2 = g_post_mlp.reshape(1, D_MODEL)
    gspec = pl.BlockSpec((1, D_MODEL), lambda i: (0, 0))
    resident = lambda a: pl.BlockSpec(a.shape, lambda i: (0, 0), pipeline_mode=pl.Buffered(1))
    return pl.pallas_call(
        _mlp_kernel,
        grid=(n // tm,),
        in_specs=[pl.BlockSpec((tm, D_MODEL), lambda i: (i, 0)), gspec, resident(w_up), resident(w_down), gspec],
        out_specs=pl.BlockSpec((tm, D_MODEL), lambda i: (i, 0)),
        out_shape=jax.ShapeDtypeStruct((n, D_MODEL), F32),
        compiler_params=_compiler_params(("parallel",)),
        name="mlp",
    )(x2d, g1, w_up, w_down, g2)


def kernel(x_prompt, x_sample, cache_k, cache_v, state_C, state_n, state_m, page_table, g_pre_mix, w_in, b_if,
           w_attn_br, w_mlstm_br, w_out, g_post_mix, g_pre_mlp, w_up, w_down, g_post_mlp):
    bp, sp, _ = x_prompt.shape
    db, ds, _ = x_sample.shape
    past = page_table.shape[1] * cache_k.shape[1]

    w_in_bf = w_in.astype(BF16)
    w_branch = w_in_bf[:, HEAD_COLS + N_GATE:]
    w_gate = jnp.pad(w_in_bf[:, HEAD_COLS:HEAD_COLS + N_GATE], ((0, 0), (0, LANES - N_GATE)))
    b_if_row = jnp.pad(b_if.astype(F32), (0, LANES - N_GATE)).reshape(1, LANES)
    wa, wm, wo = w_attn_br.astype(BF16), w_mlstm_br.astype(BF16), w_out.astype(BF16)
    wu, wd = w_up.astype(BF16), w_down.astype(BF16)

    def tail(x2d, att, ml, ga, gm):
        x1 = _merge(x2d, att, ml, ga, gm, wa, wm, wo, g_post_mix)
        return _mlp(x1, g_pre_mlp, wu, wd, g_post_mlp)

    xp = x_prompt.reshape(bp * sp, D_MODEL)
    (qt, kt, vt, k_bf, vt_bf, kmean, qm, km, vm, om, ga, gm, gates) = _project(
        xp, jnp.arange(sp, dtype=F32), g_pre_mix, w_in_bf, w_branch, w_gate, b_if_row, feature_major=True)
    att_p = _moba_prompt(qt, k_bf, vt_bf, kmean, bp, sp)
    ml_p, c_p, n_p, m_p = _mlstm(qm, km, vm, om, gates,
                                 jnp.zeros((bp, H_M, DH_M, DH_M), F32), jnp.zeros((bp, H_M, DH_M), F32),
                                 jnp.zeros((bp, H_M), F32), bp, sp)
    y_prompt = tail(xp, att_p, ml_p, ga, gm).reshape(bp, sp, D_MODEL)
    k_prompt = kt.reshape(bp, H_A, DH_A, sp).transpose(0, 3, 1, 2)
    v_prompt = vt.reshape(bp, H_A, DH_A, sp).transpose(0, 3, 1, 2)

    xs = x_sample.reshape(db * ds, D_MODEL)
    (qa, ka, va, qm, km, vm, om, ga, gm, gates) = _project(
        xs, past + jnp.arange(ds, dtype=F32), g_pre_mix, w_in_bf, w_branch, w_gate, b_if_row, feature_major=False)
    att_s = _moba_sample(qa, ka, va, cache_k, cache_v, page_table, db, ds)
    ml_s, c_s, n_s, m_s = _mlstm(qm, km, vm, om, gates, state_C.astype(F32), state_n.astype(F32),
                                 state_m.astype(F32), db, ds)
    y_sample = tail(xs, att_s, ml_s, ga, gm).reshape(db, ds, D_MODEL)
    k_sample = ka.reshape(db, ds, H_A, DH_A)
    v_sample = va.reshape(db, ds, H_A, DH_A)

    return (y_prompt, y_sample, k_prompt, v_prompt, c_p, n_p, m_p, k_sample, v_sample, c_s, n_s, m_s)
```

```python
import functools
import math

import jax
import jax.numpy as jnp
from jax import lax
from jax.experimental import pallas as pl
from jax.experimental.pallas import tpu as pltpu

F32 = jnp.float32
BF16 = jnp.bfloat16
NEG_INF = float("-inf")
LOG2_E = math.log2(math.e)

D_MODEL = 1024
H_A = 8
DH_A = 64
A_W = H_A * DH_A
MOBA_BLOCK = 256
MOBA_TOPK = 3
ROT_DIMS = DH_A // 4
ROPE_THETA = 500000.0
H_M = 4
DH_M = 128
M_W = H_M * DH_M
MLSTM_CHUNK = 128
D_FF = 4 * D_MODEL
RMS_EPS = 1e-6
N_GATE = 2 * H_M

LANES = 128
BF16_SUBLANES = 16
HEADS_PER_LANE_GROUP = LANES // DH_A
VMEM_LIMIT_BYTES = 52 * 1024 * 1024

NT_DIMS = (((1,), (1,)), ((), ()))
HEAD_COLS = 3 * A_W + 4 * M_W


def _compiler_params(semantics):
    return pltpu.CompilerParams(dimension_semantics=semantics, vmem_limit_bytes=VMEM_LIMIT_BYTES)


def _rms(x, g):
    return x * lax.rsqrt(jnp.mean(x * x, axis=-1, keepdims=True) + RMS_EPS) * g


def _log_sigmoid(x):
    return jnp.minimum(x, 0.0) - jnp.log1p(jnp.exp(-jnp.abs(x)))


PROJ_ROWS = 2 * MOBA_BLOCK
PV_ROWS = DH_A + BF16_SUBLANES


def _project_kernel(x_ref, g_ref, wt_ref, bif_ref, rc_ref, rs1_ref, rs2_ref, *out_refs, feature_major):
    if feature_major:
        (qt_ref, kt_ref, vt_ref, kb_ref, vtb_ref, kmean_ref,
         qm_ref, km_ref, vm_ref, om_ref, ga_ref, gm_ref, gates_ref) = out_refs
    else:
        qa_ref, ka_ref, va_ref, qm_ref, km_ref, vm_ref, om_ref, ga_ref, gm_ref, gates_ref = out_refs
    hb = _rms(x_ref[...], g_ref[...]).astype(BF16)

    def mm(c0, n):
        return lax.dot_general(hb, wt_ref[c0:c0 + n, :].astype(BF16), NT_DIMS, preferred_element_type=F32)

    rc, rs1, rs2 = rc_ref[...], rs1_ref[...], rs2_ref[...]

    def rot(z):
        half = ROT_DIMS // 2
        outs = []
        for c in range(z.shape[1] // LANES):
            zc = z[:, c * LANES:(c + 1) * LANES]
            outs.append(zc * rc + pltpu.roll(zc, LANES - half, 1) * rs1 + pltpu.roll(zc, half, 1) * rs2)
        return jnp.concatenate(outs, axis=1)

    qa = rot(mm(0, A_W))
    ka = rot(mm(A_W, A_W))
    va = mm(2 * A_W, A_W)
    if feature_major:
        qt_ref[0] = qa.T
        kt_ref[0] = ka.T
        kb_ref[...] = ka.astype(BF16)
        vt = va.T
        vt_ref[0] = vt
        ones_row = (lax.broadcasted_iota(jnp.int32, (PV_ROWS - DH_A, MOBA_BLOCK), 0) == 0).astype(BF16)
        for j in range(ka.shape[0] // MOBA_BLOCK):
            rows = slice(j * MOBA_BLOCK, (j + 1) * MOBA_BLOCK)
            kmean_ref[j] = jnp.mean(ka[rows], axis=0, keepdims=True)
            for h in range(H_A):
                vtb_ref[0, j, h] = jnp.concatenate([vt[h * DH_A:(h + 1) * DH_A, rows].astype(BF16), ones_row], axis=0)
    else:
        qa_ref[...] = qa
        ka_ref[...] = ka
        va_ref[...] = va
    c0 = 3 * A_W
    qm_ref[...] = mm(c0, M_W).astype(qm_ref.dtype)
    km_ref[...] = (mm(c0 + M_W, M_W) * (DH_M ** -0.5)).astype(km_ref.dtype)
    vm_ref[...] = mm(c0 + 2 * M_W, M_W)
    om_ref[...] = mm(c0 + 3 * M_W, M_W)
    branch0 = HEAD_COLS + N_GATE
    ga_ref[...] = jax.nn.sigmoid(mm(branch0, D_MODEL)).astype(ga_ref.dtype)
    gm_ref[...] = jax.nn.sigmoid(mm(branch0 + D_MODEL, D_MODEL)).astype(gm_ref.dtype)
    zg = mm(HEAD_COLS, LANES) + bif_ref[...]
    lane = lax.broadcasted_iota(jnp.int32, zg.shape, 1)
    is_forget = (lane >= H_M) & (lane < N_GATE)
    gates_ref[...] = jnp.where(is_forget, _log_sigmoid(zg), zg)


def _rotary_tables(pos):
    half = ROT_DIMS // 2
    inv = ROPE_THETA ** (-jnp.arange(half, dtype=F32) * 2.0 / ROT_DIMS)
    ang = pos[:, None] * inv[None, :]
    cos, sin = jnp.cos(ang), jnp.sin(ang)
    n = pos.shape[0]
    pad = jnp.zeros((n, DH_A - ROT_DIMS), F32)
    c_head = jnp.concatenate([cos, cos, pad + 1.0], axis=1)
    s1_head = jnp.concatenate([-sin, jnp.zeros_like(sin), pad], axis=1)
    s2_head = jnp.concatenate([jnp.zeros_like(sin), sin, pad], axis=1)
    rep = lambda t: jnp.tile(t, (1, HEADS_PER_LANE_GROUP))
    return rep(c_head), rep(s1_head), rep(s2_head)


def _project(x2d, pos, g_pre_mix, w_in_t, b_if_row, feature_major):
    n = x2d.shape[0]
    tm = min(PROJ_ROWS, n)
    n_tiles = n // tm
    rc, rs1, rs2 = _rotary_tables(pos)
    if pos.shape[0] < tm:
        reps = tm // pos.shape[0]
        rc, rs1, rs2 = (jnp.tile(t, (reps, 1)) for t in (rc, rs1, rs2))
    tab_tiles = rc.shape[0] // tm
    row = lambda w: pl.BlockSpec((tm, w), lambda i: (i, 0))
    full = lambda a: pl.BlockSpec(a.shape, lambda i: (0,) * a.ndim)
    tab = pl.BlockSpec((tm, LANES), lambda i: (i % tab_tiles, 0))
    f32 = lambda w: jax.ShapeDtypeStruct((n, w), F32)
    if feature_major:
        seq = pos.shape[0]
        batch = n // seq
        t_shape = jax.ShapeDtypeStruct((batch, A_W, seq), F32)
        t_spec = pl.BlockSpec((1, A_W, tm), lambda i: (i // tab_tiles, 0, i % tab_tiles))
        bpt = tm // MOBA_BLOCK
        attn_shape = (t_shape, t_shape, t_shape, jax.ShapeDtypeStruct((n, A_W), BF16),
                      jax.ShapeDtypeStruct((batch, tab_tiles * bpt, H_A, PV_ROWS, MOBA_BLOCK), BF16),
                      jax.ShapeDtypeStruct((n_tiles * bpt, 1, A_W), F32))
        attn_specs = (t_spec, t_spec, t_spec, row(A_W),
                      pl.BlockSpec((1, bpt, H_A, PV_ROWS, MOBA_BLOCK),
                                   lambda i: (i // tab_tiles, i % tab_tiles, 0, 0, 0)),
                      pl.BlockSpec((bpt, 1, A_W), lambda i: (i, 0, 0)))
    else:
        attn_shape = (f32(A_W), f32(A_W), f32(A_W))
        attn_specs = (row(A_W), row(A_W), row(A_W))
    qk = jax.ShapeDtypeStruct((n, M_W), BF16 if feature_major else F32)
    gate = jax.ShapeDtypeStruct((n, D_MODEL), BF16)
    out_shape = attn_shape + (qk, qk, f32(M_W), f32(M_W), gate, gate, f32(LANES))
    out_specs = attn_specs + (row(M_W), row(M_W), row(M_W), row(M_W), row(D_MODEL), row(D_MODEL), row(LANES))
    g_row = g_pre_mix.reshape(1, D_MODEL)
    return pl.pallas_call(
        functools.partial(_project_kernel, feature_major=feature_major),
        grid=(n_tiles,),
        in_specs=[row(D_MODEL), full(g_row),
                  pl.BlockSpec(w_in_t.shape, lambda i: (0, 0), pipeline_mode=pl.Buffered(1)),
                  full(b_if_row), tab, tab, tab],
        out_specs=out_specs,
        out_shape=out_shape,
        compiler_params=_compiler_params(("parallel",)),
        name="project",
    )(x2d, g_row, w_in_t, b_if_row, rc, rs1, rs2)


def _select_topk(scores, block_idx, n_valid, axis):
    width = scores.shape[axis]
    sc = jnp.where(block_idx < n_valid, scores, NEG_INF)
    sel = jnp.zeros(scores.shape, jnp.bool_)
    for _ in range(MOBA_TOPK):
        mx = jnp.max(sc, axis=axis, keepdims=True)
        idx = jnp.min(jnp.where(sc == mx, block_idx, width), axis=axis, keepdims=True)
        hit = block_idx == idx
        sel = sel | hit
        sc = jnp.where(hit, NEG_INF, sc)
    return jnp.where(sel & (block_idx < n_valid), 1.0, 0.0)


FOLD_LAG = 1

def _moba_prompt_kernel(qt_ref, k_ref, vt_ref, kmean_ref, o_ref,
                        qtb_ref, sel_ref, s0_ref, s1_ref, m_ref, acc_ref, ot_ref, *, n_blocks):
    i = pl.program_id(1)
    tq = MOBA_BLOCK
    scale = DH_A ** -0.5
    key = lax.broadcasted_iota(jnp.int32, (tq, tq), 0)
    qry = lax.broadcasted_iota(jnp.int32, (tq, tq), 1)
    causal = key <= qry
    feat = lax.broadcasted_iota(jnp.int32, (LANES, tq), 0)
    blk = lax.broadcasted_iota(jnp.int32, (n_blocks, tq), 0)
    n_pairs = A_W // LANES
    pair_feats = [slice(p * LANES, (p + 1) * LANES) for p in range(n_pairs)]

    heads = [(p, p * HEADS_PER_LANE_GROUP + hh) for p in range(n_pairs) for hh in range(HEADS_PER_LANE_GROUP)]

    def score(h, p, block):
        start = block * tq if isinstance(block, int) else pl.multiple_of(block * tq, tq)
        k_blk = k_ref[0, pl.ds(start, tq), pair_feats[p]]
        return jnp.dot(k_blk, qtb_ref[h], preferred_element_type=F32)

    def fold(h, p, block, s, own):
        m_old = m_ref[h]
        if own:
            s = jnp.where(causal, s, NEG_INF)
            m_new = jnp.maximum(m_old, jnp.max(s, axis=0, keepdims=True))
            m_shift = m_exp = m_new
        else:
            picked = sel_ref[h * n_blocks + block] > 0.5
            m_new = jnp.maximum(m_old, jnp.where(picked, jnp.max(s, axis=0, keepdims=True), NEG_INF))
            m_shift = jnp.where(m_new == NEG_INF, 0.0, m_new)
            m_exp = jnp.where(picked, m_shift, jnp.inf)
        alpha = jnp.exp2(m_old - m_shift)
        pe = jnp.exp2((s - m_exp).astype(BF16))
        pv = jnp.dot(vt_ref[0, block, h], pe, preferred_element_type=F32)
        m_ref[h] = m_new
        acc_ref[h] = alpha * acc_ref[h] + pv

    for p in range(n_pairs):
        qt_pair = qt_ref[0, pair_feats[p], :]
        kmean_pair = kmean_ref[0, :, pair_feats[p]]
        for hh in range(HEADS_PER_LANE_GROUP):
            h = p * HEADS_PER_LANE_GROUP + hh
            qt_head = jnp.where((feat // DH_A) == hh, qt_pair, 0.0)
            scores = jnp.dot(kmean_pair, qt_head, precision=lax.Precision.HIGHEST, preferred_element_type=F32)
            sel = _select_topk(scores, blk, i, axis=0)
            for j in range(n_blocks):
                sel_ref[h * n_blocks + j] = sel[j:j + 1, :]
            qtb_ref[h] = (qt_head * (scale * LOG2_E)).astype(BF16)
            m_ref[h] = jnp.full((1, tq), NEG_INF, F32)
            acc_ref[h] = jnp.zeros((PV_ROWS, tq), F32)

    bufs = (s0_ref, s1_ref)

    def stage(block, cur, own, score_next=True):
        for idx, (p, h) in enumerate(heads):
            if score_next:
                bufs[1 - cur][h] = score(h, p, block + 1)
            if idx >= FOLD_LAG:
                p_f, h_f = heads[idx - FOLD_LAG]
                fold(h_f, p_f, block, bufs[cur][h_f], own)
        for p_f, h_f in heads[len(heads) - FOLD_LAG:]:
            fold(h_f, p_f, block, bufs[cur][h_f], own)

    for p, h in heads:
        s0_ref[h] = score(h, p, 0)

    def body(jj, carry):
        stage(2 * jj, 0, own=False)
        stage(2 * jj + 1, 1, own=False)
        return carry

    lax.fori_loop(0, i // 2, body, 0)

    @pl.when(i % 2 == 0)
    def _():
        stage(i, 0, own=True, score_next=False)

    @pl.when(i % 2 == 1)
    def _():
        stage(i - 1, 0, own=False)
        stage(i, 1, own=True, score_next=False)

    for _, h in heads:
        ot_ref[h * DH_A:(h + 1) * DH_A, :] = acc_ref[h, :DH_A, :] / acc_ref[h, DH_A:DH_A + 1, :]
    o_ref[...] = ot_ref[...].T.astype(o_ref.dtype)


def _moba_prompt(qt, k_bf, vt_bf, kmean, batch, seq):
    n_blocks = seq // MOBA_BLOCK
    tq = MOBA_BLOCK
    k3 = k_bf.reshape(batch, seq, A_W)
    km3 = kmean.reshape(batch, n_blocks, A_W)
    return pl.pallas_call(
        functools.partial(_moba_prompt_kernel, n_blocks=n_blocks),
        grid=(batch, n_blocks),
        in_specs=[pl.BlockSpec((1, A_W, tq), lambda b, i: (b, 0, i)),
                  pl.BlockSpec((1, seq, A_W), lambda b, i: (b, 0, 0)),
                  pl.BlockSpec((1, n_blocks, H_A, PV_ROWS, tq), lambda b, i: (b, 0, 0, 0, 0)),
                  pl.BlockSpec((1, n_blocks, A_W), lambda b, i: (b, 0, 0))],
        out_specs=pl.BlockSpec((tq, A_W), lambda b, i: (b * n_blocks + i, 0)),
        out_shape=jax.ShapeDtypeStruct((batch * seq, A_W), BF16),
        scratch_shapes=[pltpu.VMEM((H_A, LANES, tq), BF16),
                        pltpu.VMEM((H_A * n_blocks, 1, tq), F32),
                        pltpu.VMEM((H_A, tq, tq), F32),
                        pltpu.VMEM((H_A, tq, tq), F32),
                        pltpu.VMEM((H_A, 1, tq), F32),
                        pltpu.VMEM((H_A, PV_ROWS, tq), F32),
                        pltpu.VMEM((A_W, tq), F32)],
        compiler_params=_compiler_params(("parallel", "arbitrary")),
        name="moba_prompt",
    )(qt, k3, vt_bf, km3)


SAMPLE_RING_PAGES = 64
SAMPLE_BLOCKS_PER_ITER = 8


def _moba_sample_kernel(pt_ref, q_ref, kn_ref, vn_ref, ck_ref, cv_ref, o_ref,
                        ring_ref, ring_sem, qbd_ref, qbdt_ref, s_ref, score_ref, bmax_ref, acc_ref,
                        *, n_pages, n_blocks, page_rows, dec_seq):
    b = pl.program_id(0)
    n_seq = pl.num_programs(0)
    n_ring = SAMPLE_RING_PAGES
    stream_len = 2 * n_pages
    ppb = MOBA_BLOCK // page_rows
    n_q = H_A * dec_seq

    def ring_copy(pool_ref, page, slot):
        return pltpu.make_async_copy(pool_ref.at[page], ring_ref.at[slot], ring_sem.at[slot])

    def start_fetch(seq, pos, keys_only=False):
        slot = pos % n_ring
        if keys_only:
            ring_copy(ck_ref, pt_ref[seq, pos], slot).start()
            return

        @pl.when(pos < n_pages)
        def _():
            ring_copy(ck_ref, pt_ref[seq, pos], slot).start()

        @pl.when(pos >= n_pages)
        def _():
            ring_copy(cv_ref, pt_ref[seq, pos - n_pages], slot).start()

    def take(pos):
        slot = pos % n_ring
        ring_copy(ck_ref, 0, slot).wait()
        return slot

    def refill(pos):
        nxt = pos + n_ring

        @pl.when(nxt < stream_len)
        def _():
            start_fetch(b, nxt)

        @pl.when((nxt >= stream_len) & (b + 1 < n_seq))
        def _():
            start_fetch(b + 1, nxt - stream_len, keys_only=True)

    @pl.when(b == 0)
    def _():
        for pos in range(n_ring):
            start_fetch(0, pos, keys_only=True)

    scale = DH_A ** -0.5
    row = lax.broadcasted_iota(jnp.int32, (n_q, A_W), 0)
    lane_w = lax.broadcasted_iota(jnp.int32, (n_q, A_W), 1)
    head_diag = (row // dec_seq) == (lane_w // DH_A)
    lane = lax.broadcasted_iota(jnp.int32, (n_q, LANES), 1)
    tok = lax.broadcasted_iota(jnp.int32, (n_q, 1), 0) % dec_seq
    score_blk = lax.broadcasted_iota(jnp.int32, (LANES, LANES), 0)

    def page_cols(t):
        return slice(t * page_rows, (t + 1) * page_rows)

    q_rep = jnp.concatenate([q_ref[...]] * H_A, axis=0)
    qbd = jnp.where(head_diag, q_rep, 0.0)
    qbd_ref[...] = qbd
    qbdt_ref[...] = jnp.concatenate([qbd, jnp.zeros((LANES - n_q, A_W), F32)], axis=0).T
    score_ref[...] = jnp.full(score_ref.shape, NEG_INF, F32)
    bmax_ref[...] = jnp.full(bmax_ref.shape, NEG_INF, F32)

    unroll = SAMPLE_BLOCKS_PER_ITER
    pages_per_iter = unroll * ppb

    def k_blocks(it, carry):
        qb = (qbd_ref[...] * scale).astype(BF16)
        qbdt = qbdt_ref[...]
        pos0 = it * pages_per_iter
        slots = [take(pos0 + t) for t in range(pages_per_iter)]
        score, bmax = score_ref[...], bmax_ref[...]
        for u in range(unroll):
            jb = it * unroll + u
            kts = [ring_ref[slots[u * ppb + t]] for t in range(ppb)]
            kmean = jnp.sum(functools.reduce(lambda x, y: x + y, kts), axis=1, keepdims=True) * (1.0 / MOBA_BLOCK)
            sc = jnp.sum(qbdt * kmean, axis=0, keepdims=True)
            score = jnp.where(score_blk == jb, sc, score)
            s_pages = [jnp.dot(qb, kt.astype(BF16), preferred_element_type=F32) for kt in kts]
            for t in range(ppb):
                s_ref[jb, :, page_cols(t)] = s_pages[t]
            s_max = jnp.max(functools.reduce(jnp.maximum, s_pages), axis=1, keepdims=True)
            bmax = jnp.where(lane == jb, s_max, bmax)
        score_ref[...], bmax_ref[...] = score, bmax
        for t in range(pages_per_iter):
            refill(pos0 + t)
        return carry

    lax.fori_loop(0, n_blocks // unroll, k_blocks, 0)

    sel = _select_topk(score_ref[...].T[:n_q], lane, n_blocks, axis=1)
    qs = qbd * scale
    kn = kn_ref[...]
    own = []
    for t in range(dec_seq):
        so = jnp.sum(qs * kn[t:t + 1, :], axis=1, keepdims=True)
        own.append(jnp.where(t <= tok, so, NEG_INF))
    m = jnp.max(jnp.where(sel > 0.5, bmax_ref[...], NEG_INF), axis=1, keepdims=True)
    m = functools.reduce(jnp.maximum, own, m)
    p_own = jnp.zeros((n_q, LANES), F32)
    for t in range(dec_seq):
        p_own = jnp.where(lane == t, jnp.exp(own[t] - m), p_own)
    acc_ref[...] = jnp.zeros(acc_ref.shape, F32)

    def v_blocks(it, lsum):
        pos0 = n_pages + it * pages_per_iter
        slots = [take(pos0 + t) for t in range(pages_per_iter)]
        weights = []
        for u in range(unroll):
            jb = it * unroll + u
            picked = jnp.sum(jnp.where(lane == jb, sel, 0.0), axis=1, keepdims=True) > 0.5
            pj = jnp.exp(jnp.where(picked, s_ref[jb] - m, NEG_INF))
            lsum = lsum + pj
            weights.append(pj.astype(BF16))
        acc = acc_ref[...]
        for u in range(unroll):
            for t in range(ppb):
                vt = ring_ref[slots[u * ppb + t]].astype(BF16)
                acc = acc + lax.dot_general(weights[u][:, page_cols(t)], vt, NT_DIMS,
                                            preferred_element_type=F32)
        acc_ref[...] = acc
        for t in range(pages_per_iter):
            refill(pos0 + t)
        return lsum

    lsum = lax.fori_loop(0, n_blocks // unroll, v_blocks, jnp.zeros((n_q, MOBA_BLOCK), F32))

    acc = acc_ref[...]
    vn = vn_ref[...]
    for t in range(dec_seq):
        acc = acc + p_own[:, t:t + 1] * vn[t:t + 1, :]
    l = jnp.sum(lsum, axis=1, keepdims=True) + jnp.sum(p_own, axis=1, keepdims=True)
    out = jnp.where(head_diag, acc / l, 0.0)
    o_ref[...] = functools.reduce(
        lambda x, y: x + y, [out[h * dec_seq:(h + 1) * dec_seq, :] for h in range(H_A)])


def _moba_sample(qa, ka, va, cache_k, cache_v, page_table, dec_batch, dec_seq):
    n_pool, page_rows = cache_k.shape[0], cache_k.shape[1]
    n_pages = page_table.shape[1]
    past = n_pages * page_rows
    assert past % MOBA_BLOCK == 0 and MOBA_BLOCK % page_rows == 0, "cached rows must fill whole MoBA blocks"
    n_blocks = past // MOBA_BLOCK
    assert MOBA_TOPK <= n_blocks <= LANES
    n_ring = SAMPLE_RING_PAGES
    assert page_rows % LANES == 0 and n_ring <= n_pages and (2 * n_pages) % n_ring == 0
    assert n_blocks % SAMPLE_BLOCKS_PER_ITER == 0
    n_q = H_A * dec_seq
    assert n_q <= LANES
    ck = cache_k.transpose(0, 2, 3, 1).reshape(n_pool, A_W, page_rows)
    cv = cache_v.transpose(0, 2, 3, 1).reshape(n_pool, A_W, page_rows)

    tok_spec = pl.BlockSpec((dec_seq, A_W), lambda b, pt: (b, 0))
    pool_spec = pl.BlockSpec(memory_space=pl.ANY)
    grid_spec = pltpu.PrefetchScalarGridSpec(
        num_scalar_prefetch=1,
        grid=(dec_batch,),
        in_specs=[tok_spec, tok_spec, tok_spec, pool_spec, pool_spec],
        out_specs=tok_spec,
        scratch_shapes=[pltpu.VMEM((n_ring, A_W, page_rows), F32),
                        pltpu.SemaphoreType.DMA((n_ring,)),
                        pltpu.VMEM((n_q, A_W), F32),
                        pltpu.VMEM((A_W, LANES), F32),
                        pltpu.VMEM((n_blocks, n_q, MOBA_BLOCK), F32),
                        pltpu.VMEM((LANES, LANES), F32),
                        pltpu.VMEM((n_q, LANES), F32),
                        pltpu.VMEM((n_q, A_W), F32)],
    )
    return pl.pallas_call(
        functools.partial(_moba_sample_kernel, n_pages=n_pages, n_blocks=n_blocks,
                          page_rows=page_rows, dec_seq=dec_seq),
        grid_spec=grid_spec,
        out_shape=jax.ShapeDtypeStruct((dec_batch * dec_seq, A_W), F32),
        compiler_params=_compiler_params(("arbitrary",)),
        name="moba_sample",
    )(page_table, qa, ka, va, ck, cv)


MLSTM_SEQS_PER_STEP = 4


def _mlstm_kernel(q_ref, k_ref, v_ref, o_ref, gate_ref, c0_ref, n0_ref, m0_ref,
                  h_ref, c_ref, n_ref, m_ref, *, chunk):
    step = pl.program_id(1)
    lp = MLSTM_CHUNK
    n_seq = q_ref.shape[0]

    @pl.when(step == 0)
    def _():
        c_ref[...] = c0_ref[...]
        n_ref[...] = n0_ref[...]
        m_ref[...] = m0_ref[...]

    def pad_rows(a):
        if chunk == lp:
            return a
        return jnp.concatenate([a, jnp.zeros((lp - chunk, a.shape[1]), a.dtype)], axis=0)

    src = lax.broadcasted_iota(jnp.int32, (lp, lp), 0)
    tgt = lax.broadcasted_iota(jnp.int32, (lp, lp), 1)
    lane8 = lax.broadcasted_iota(jnp.int32, (N_GATE, lp), 1)
    chains = [(b, h) for b in range(n_seq) for h in range(H_M)]
    lanes = [slice(h * DH_M, (h + 1) * DH_M) for h in range(H_M)]
    last = slice(chunk - 1, chunk)

    q_all = [pad_rows(q_ref[b]) for b in range(n_seq)]
    k_all = [pad_rows(k_ref[b]) for b in range(n_seq)]
    v_all = [pad_rows(v_ref[b]) for b in range(n_seq)]
    qb = {(b, h): q_all[b][:, lanes[h]].astype(BF16) for b, h in chains}
    kb = {(b, h): k_all[b][:, lanes[h]].astype(BF16) for b, h in chains}

    s_kq = {ch: lax.dot_general(kb[ch], qb[ch], NT_DIMS, preferred_element_type=F32) for ch in chains}
    c_q = {(b, h): lax.dot_general(c_ref[b, h].astype(BF16), qb[b, h], NT_DIMS, preferred_element_type=F32)
           for b, h in chains}
    n_q = {(b, h): lax.dot_general(jnp.broadcast_to(n_ref[b, h], (N_GATE, DH_M)).astype(BF16), qb[b, h], NT_DIMS,
                                   preferred_element_type=F32)[0:1] for b, h in chains}

    g_row, b_row_all = [], []
    for b in range(n_seq):
        g = pad_rows(gate_ref[b])
        csum = g
        shift = 1
        while shift < lp:
            csum = csum + jnp.where(src >= shift, pltpu.roll(csum, shift, 0), 0.0)
            shift *= 2
        g_row.append(g.T[:N_GATE])
        b_row_all.append(csum.T[:N_GATE])

    v_t = {(b, h): v_all[b][:, lanes[h]].T for b, h in chains}
    w_d, w_prev, m_row, decay, w_s = {}, {}, {}, {}, {}
    for b, h in chains:
        i_row, b_row = g_row[b][h:h + 1], b_row_all[b][H_M + h:H_M + h + 1]
        m_prev = m_ref[b, h]
        u_src = jnp.broadcast_to(b_row - i_row, (lp, lp)).T
        log_d = jnp.where(src <= tgt, b_row - u_src, NEG_INF)
        m_row[b, h] = jnp.maximum(m_prev + b_row, jnp.max(log_d, axis=0, keepdims=True))
        w_d[b, h] = jnp.exp(log_d - m_row[b, h])
        w_prev[b, h] = jnp.exp(m_prev + b_row - m_row[b, h])
        m_new = m_row[b, h][:, last]
        b_last = b_row[:, last]
        decay[b, h] = jnp.exp(m_prev + b_last - m_new)
        w_s[b, h] = jnp.where(lane8[0:1] < chunk, jnp.exp(b_last - b_row + i_row - m_new), 0.0)
        m_ref[b, h] = m_new

    a = {ch: s_kq[ch] * w_d[ch] for ch in chains}
    v_a = {ch: jnp.dot(v_t[ch].astype(BF16), a[ch].astype(BF16), preferred_element_type=F32) for ch in chains}
    c_upd = {ch: jnp.dot((v_t[ch] * w_s[ch]).astype(BF16), kb[ch], preferred_element_type=F32) for ch in chains}
    n_upd = {ch: jnp.dot(jnp.broadcast_to(w_s[ch], (N_GATE, lp)).astype(BF16), kb[ch],
                         preferred_element_type=F32)[0:1] for ch in chains}

    for b, h in chains:
        num = v_a[b, h] + w_prev[b, h] * c_q[b, h]
        den = jnp.sum(a[b, h], axis=0, keepdims=True) + w_prev[b, h] * n_q[b, h]
        hc = (num / jnp.maximum(jnp.abs(den), jnp.exp(-m_row[b, h]))).T
        out = jax.nn.sigmoid(pad_rows(o_ref[b])[:, lanes[h]]) * hc
        h_ref[b, :, lanes[h]] = out[:chunk].astype(h_ref.dtype)
        c_ref[b, h] = decay[b, h] * c_ref[b, h] + c_upd[b, h]
        n_ref[b, h] = decay[b, h] * n_ref[b, h] + n_upd[b, h]


def _mlstm(qm, km, vm, om, gates, c0, n0, m0, batch, seq):
    chunk = math.gcd(seq, MLSTM_CHUNK)
    n_chunks = seq // chunk
    nb = MLSTM_SEQS_PER_STEP
    assert batch % nb == 0
    h_dtype = BF16 if chunk % BF16_SUBLANES == 0 else F32
    n0_4 = n0.reshape(batch, H_M, 1, DH_M)
    m0_4 = m0.reshape(batch, H_M, 1, 1)
    per_seq = lambda a: a.reshape(batch, seq, a.shape[-1])
    row = lambda w: pl.BlockSpec((nb, chunk, w), lambda g, s: (g, s, 0))
    state = lambda a: pl.BlockSpec((nb,) + a.shape[1:], lambda g, s: (g, 0, 0, 0))
    h, c, n, m = pl.pallas_call(
        functools.partial(_mlstm_kernel, chunk=chunk),
        grid=(batch // nb, n_chunks),
        in_specs=[row(M_W), row(M_W), row(M_W), row(M_W), row(LANES), state(c0), state(n0_4), state(m0_4)],
        out_specs=(row(M_W), state(c0), state(n0_4), state(m0_4)),
        out_shape=(jax.ShapeDtypeStruct((batch, seq, M_W), h_dtype),
                   jax.ShapeDtypeStruct(c0.shape, F32),
                   jax.ShapeDtypeStruct(n0_4.shape, F32),
                   jax.ShapeDtypeStruct(m0_4.shape, F32)),
        compiler_params=_compiler_params(("parallel", "arbitrary")),
        name="mlstm",
    )(per_seq(qm), per_seq(km), per_seq(vm), per_seq(om), per_seq(gates), c0, n0_4, m0_4)
    return h.reshape(batch * seq, M_W), c, n.reshape(batch, H_M, DH_M), m.reshape(batch, H_M)


def _merge_kernel(x_ref, att_ref, ml_ref, gate_a_ref, gate_m_ref, wa_ref, wm_ref, wo_ref, g_ref, o_ref):
    ya = jnp.dot(att_ref[...].astype(BF16), wa_ref[...], preferred_element_type=F32)
    ym = jnp.dot(ml_ref[...].astype(BF16), wm_ref[...], preferred_element_type=F32)
    u = gate_a_ref[...].astype(F32) * ya + gate_m_ref[...].astype(F32) * ym
    r = jnp.dot(u.astype(BF16), wo_ref[...], preferred_element_type=F32)
    o_ref[...] = x_ref[...] + _rms(r, g_ref[...])


def _merge(x2d, att, ml, ga, gm, wa, wm, wo, g_post_mix):
    n = x2d.shape[0]
    tm = min(512, n)
    row = lambda w: pl.BlockSpec((tm, w), lambda i: (i, 0))
    full = lambda a: pl.BlockSpec(a.shape, lambda i: (0,) * a.ndim)
    g_row = g_post_mix.reshape(1, D_MODEL)
    return pl.pallas_call(
        _merge_kernel,
        grid=(n // tm,),
        in_specs=[row(D_MODEL), row(A_W), row(M_W), row(D_MODEL), row(D_MODEL),
                  full(wa), full(wm), full(wo), full(g_row)],
        out_specs=row(D_MODEL),
        out_shape=jax.ShapeDtypeStruct((n, D_MODEL), F32),
        compiler_params=_compiler_params(("parallel",)),
        name="merge",
    )(x2d, att, ml, ga, gm, wa, wm, wo, g_row)


MLP_FF_TILE = 1024


def _mlp_kernel(x_ref, g1_ref, wu_ref, wd_ref, g2_ref, o_ref):
    x = x_ref[...]
    h = _rms(x, g1_ref[...]).astype(BF16)
    n_chunks = D_FF // MLP_FF_TILE
    cols = lambda c: slice(c * MLP_FF_TILE, (c + 1) * MLP_FF_TILE)
    up = lambda c: jnp.dot(h, wu_ref[:, cols(c)], preferred_element_type=F32)
    pending = up(0)
    acc = None
    for c in range(n_chunks):
        act = jnp.square(jnp.maximum(pending, 0.0)).astype(BF16)
        if c + 1 < n_chunks:
            pending = up(c + 1)
        part = jnp.dot(act, wd_ref[cols(c), :], preferred_element_type=F32)
        acc = part if acc is None else acc + part
    o_ref[...] = x + _rms(acc, g2_ref[...])


def _mlp(x2d, g_pre_mlp, w_up, w_down, g_post_mlp):
    n = x2d.shape[0]
    tm = min(512, n)
    g1 = g_pre_mlp.reshape(1, D_MODEL)
    g2 = g_post_mlp.reshape(1, D_MODEL)
    gspec = pl.BlockSpec((1, D_MODEL), lambda i: (0, 0))
    resident = lambda a: pl.BlockSpec(a.shape, lambda i: (0, 0), pipeline_mode=pl.Buffered(1))
    return pl.pallas_call(
        _mlp_kernel,
        grid=(n // tm,),
        in_specs=[pl.BlockSpec((tm, D_MODEL), lambda i: (i, 0)), gspec, resident(w_up), resident(w_down), gspec],
        out_specs=pl.BlockSpec((tm, D_MODEL), lambda i: (i, 0)),
        out_shape=jax.ShapeDtypeStruct((n, D_MODEL), F32),
        compiler_params=_compiler_params(("parallel",)),
        name="mlp",
    )(x2d, g1, w_up, w_down, g2)


def kernel(x_prompt, x_sample, cache_k, cache_v, state_C, state_n, state_m, page_table, g_pre_mix, w_in, b_if,
           w_attn_br, w_mlstm_br, w_out, g_post_mix, g_pre_mlp, w_up, w_down, g_post_mlp):
    bp, sp, _ = x_prompt.shape
    db, ds, _ = x_sample.shape
    past = page_table.shape[1] * cache_k.shape[1]

    w_in_t = w_in.T
    b_if_row = jnp.pad(b_if.astype(F32), (0, LANES - N_GATE)).reshape(1, LANES)
    wa, wm, wo = w_attn_br.astype(BF16), w_mlstm_br.astype(BF16), w_out.astype(BF16)
    wu, wd = w_up.astype(BF16), w_down.astype(BF16)

    def tail(x2d, att, ml, ga, gm):
        x1 = _merge(x2d, att, ml, ga, gm, wa, wm, wo, g_post_mix)
        return _mlp(x1, g_pre_mlp, wu, wd, g_post_mlp)

    xp = x_prompt.reshape(bp * sp, D_MODEL)
    (qt, kt, vt, k_bf, vt_bf, kmean, qm, km, vm, om, ga, gm, gates) = _project(
        xp, jnp.arange(sp, dtype=F32), g_pre_mix, w_in_t, b_if_row, feature_major=True)
    att_p = _moba_prompt(qt, k_bf, vt_bf, kmean, bp, sp)
    ml_p, c_p, n_p, m_p = _mlstm(qm, km, vm, om, gates,
                                 jnp.zeros((bp, H_M, DH_M, DH_M), F32), jnp.zeros((bp, H_M, DH_M), F32),
                                 jnp.zeros((bp, H_M), F32), bp, sp)
    y_prompt = tail(xp, att_p, ml_p, ga, gm).reshape(bp, sp, D_MODEL)
    k_prompt = kt.reshape(bp, H_A, DH_A, sp).transpose(0, 3, 1, 2)
    v_prompt = vt.reshape(bp, H_A, DH_A, sp).transpose(0, 3, 1, 2)

    xs = x_sample.reshape(db * ds, D_MODEL)
    (qa, ka, va, qm, km, vm, om, ga, gm, gates) = _project(
        xs, past + jnp.arange(ds, dtype=F32), g_pre_mix, w_in_t, b_if_row, feature_major=False)
    att_s = _moba_sample(qa, ka, va, cache_k, cache_v, page_table, db, ds)
    ml_s, c_s, n_s, m_s = _mlstm(qm, km, vm, om, gates, state_C.astype(F32), state_n.astype(F32),
                                 state_m.astype(F32), db, ds)
    y_sample = tail(xs, att_s, ml_s, ga, gm).reshape(db, ds, D_MODEL)
    k_sample = ka.reshape(db, ds, H_A, DH_A)
    v_sample = va.reshape(db, ds, H_A, DH_A)

    return (y_prompt, y_sample, k_prompt, v_prompt, c_p, n_p, m_p, k_sample, v_sample, c_s, n_s, m_s)
```

```python
import functools
import math

import jax
import jax.numpy as jnp
from jax import lax
from jax.experimental import pallas as pl
from jax.experimental.pallas import tpu as pltpu

F32 = jnp.float32
BF16 = jnp.bfloat16
NEG_INF = float("-inf")
LOG2_E = math.log2(math.e)

D_MODEL = 1024
H_A = 8
DH_A = 64
A_W = H_A * DH_A
MOBA_BLOCK = 256
MOBA_TOPK = 3
ROT_DIMS = DH_A // 4
ROPE_THETA = 500000.0
H_M = 4
DH_M = 128
M_W = H_M * DH_M
MLSTM_CHUNK = 128
D_FF = 4 * D_MODEL
RMS_EPS = 1e-6
N_GATE = 2 * H_M

LANES = 128
BF16_SUBLANES = 16
HEADS_PER_LANE_GROUP = LANES // DH_A
VMEM_LIMIT_BYTES = 52 * 1024 * 1024

NT_DIMS = (((1,), (1,)), ((), ()))
HEAD_COLS = 3 * A_W + 4 * M_W


def _compiler_params(semantics):
    return pltpu.CompilerParams(dimension_semantics=semantics, vmem_limit_bytes=VMEM_LIMIT_BYTES)


def _rms(x, g):
    return x * lax.rsqrt(jnp.mean(x * x, axis=-1, keepdims=True) + RMS_EPS) * g


def _log_sigmoid(x):
    return jnp.minimum(x, 0.0) - jnp.log1p(jnp.exp(-jnp.abs(x)))


PROJ_ROWS = 2 * MOBA_BLOCK
PV_ROWS = DH_A + BF16_SUBLANES


def _project_kernel(x_ref, g_ref, wt_ref, bif_ref, rc_ref, rs1_ref, rs2_ref, *out_refs, feature_major):
    if feature_major:
        (qt_ref, kt_ref, vt_ref, kb_ref, vtb_ref, kmean_ref,
         qm_ref, km_ref, vm_ref, om_ref, ga_ref, gm_ref, gates_ref) = out_refs
    else:
        qa_ref, ka_ref, va_ref, qm_ref, km_ref, vm_ref, om_ref, ga_ref, gm_ref, gates_ref = out_refs
    hb = _rms(x_ref[...], g_ref[...]).astype(BF16)

    def mm(c0, n):
        return lax.dot_general(hb, wt_ref[c0:c0 + n, :].astype(BF16), NT_DIMS, preferred_element_type=F32)

    rc, rs1, rs2 = rc_ref[...], rs1_ref[...], rs2_ref[...]

    def rot(z):
        half = ROT_DIMS // 2
        outs = []
        for c in range(z.shape[1] // LANES):
            zc = z[:, c * LANES:(c + 1) * LANES]
            outs.append(zc * rc + pltpu.roll(zc, LANES - half, 1) * rs1 + pltpu.roll(zc, half, 1) * rs2)
        return jnp.concatenate(outs, axis=1)

    qa = rot(mm(0, A_W))
    ka = rot(mm(A_W, A_W))
    va = mm(2 * A_W, A_W)
    if feature_major:
        qt_ref[0] = qa.T
        kt_ref[0] = ka.T
        kb_ref[...] = ka.astype(BF16)
        vt = va.T
        vt_ref[0] = vt
        ones_row = (lax.broadcasted_iota(jnp.int32, (PV_ROWS - DH_A, MOBA_BLOCK), 0) == 0).astype(BF16)
        for j in range(ka.shape[0] // MOBA_BLOCK):
            rows = slice(j * MOBA_BLOCK, (j + 1) * MOBA_BLOCK)
            kmean_ref[j] = jnp.mean(ka[rows], axis=0, keepdims=True)
            for h in range(H_A):
                vtb_ref[0, j, h] = jnp.concatenate([vt[h * DH_A:(h + 1) * DH_A, rows].astype(BF16), ones_row], axis=0)
    else:
        qa_ref[...] = qa
        ka_ref[...] = ka
        va_ref[...] = va
    c0 = 3 * A_W
    qm_ref[...] = mm(c0, M_W).astype(qm_ref.dtype)
    km_ref[...] = (mm(c0 + M_W, M_W) * (DH_M ** -0.5)).astype(km_ref.dtype)
    vm_ref[...] = mm(c0 + 2 * M_W, M_W)
    om_ref[...] = mm(c0 + 3 * M_W, M_W)
    branch0 = HEAD_COLS + N_GATE
    ga_ref[...] = jax.nn.sigmoid(mm(branch0, D_MODEL)).astype(ga_ref.dtype)
    gm_ref[...] = jax.nn.sigmoid(mm(branch0 + D_MODEL, D_MODEL)).astype(gm_ref.dtype)
    zg = mm(HEAD_COLS, LANES) + bif_ref[...]
    lane = lax.broadcasted_iota(jnp.int32, zg.shape, 1)
    is_forget = (lane >= H_M) & (lane < N_GATE)
    gates_ref[...] = jnp.where(is_forget, _log_sigmoid(zg), zg)


def _rotary_tables(pos):
    half = ROT_DIMS // 2
    inv = ROPE_THETA ** (-jnp.arange(half, dtype=F32) * 2.0 / ROT_DIMS)
    ang = pos[:, None] * inv[None, :]
    cos, sin = jnp.cos(ang), jnp.sin(ang)
    n = pos.shape[0]
    pad = jnp.zeros((n, DH_A - ROT_DIMS), F32)
    c_head = jnp.concatenate([cos, cos, pad + 1.0], axis=1)
    s1_head = jnp.concatenate([-sin, jnp.zeros_like(sin), pad], axis=1)
    s2_head = jnp.concatenate([jnp.zeros_like(sin), sin, pad], axis=1)
    rep = lambda t: jnp.tile(t, (1, HEADS_PER_LANE_GROUP))
    return rep(c_head), rep(s1_head), rep(s2_head)


def _project(x2d, pos, g_pre_mix, w_in_t, b_if_row, feature_major):
    n = x2d.shape[0]
    tm = min(PROJ_ROWS, n)
    n_tiles = n // tm
    rc, rs1, rs2 = _rotary_tables(pos)
    if pos.shape[0] < tm:
        reps = tm // pos.shape[0]
        rc, rs1, rs2 = (jnp.tile(t, (reps, 1)) for t in (rc, rs1, rs2))
    tab_tiles = rc.shape[0] // tm
    row = lambda w: pl.BlockSpec((tm, w), lambda i: (i, 0))
    full = lambda a: pl.BlockSpec(a.shape, lambda i: (0,) * a.ndim)
    tab = pl.BlockSpec((tm, LANES), lambda i: (i % tab_tiles, 0))
    f32 = lambda w: jax.ShapeDtypeStruct((n, w), F32)
    if feature_major:
        seq = pos.shape[0]
        batch = n // seq
        t_shape = jax.ShapeDtypeStruct((batch, A_W, seq), F32)
        t_spec = pl.BlockSpec((1, A_W, tm), lambda i: (i // tab_tiles, 0, i % tab_tiles))
        bpt = tm // MOBA_BLOCK
        attn_shape = (t_shape, t_shape, t_shape, jax.ShapeDtypeStruct((n, A_W), BF16),
                      jax.ShapeDtypeStruct((batch, tab_tiles * bpt, H_A, PV_ROWS, MOBA_BLOCK), BF16),
                      jax.ShapeDtypeStruct((n_tiles * bpt, 1, A_W), F32))
        attn_specs = (t_spec, t_spec, t_spec, row(A_W),
                      pl.BlockSpec((1, bpt, H_A, PV_ROWS, MOBA_BLOCK),
                                   lambda i: (i // tab_tiles, i % tab_tiles, 0, 0, 0)),
                      pl.BlockSpec((bpt, 1, A_W), lambda i: (i, 0, 0)))
    else:
        attn_shape = (f32(A_W), f32(A_W), f32(A_W))
        attn_specs = (row(A_W), row(A_W), row(A_W))
    qk = jax.ShapeDtypeStruct((n, M_W), BF16 if feature_major else F32)
    gate = jax.ShapeDtypeStruct((n, D_MODEL), BF16)
    out_shape = attn_shape + (qk, qk, f32(M_W), f32(M_W), gate, gate, f32(LANES))
    out_specs = attn_specs + (row(M_W), row(M_W), row(M_W), row(M_W), row(D_MODEL), row(D_MODEL), row(LANES))
    g_row = g_pre_mix.reshape(1, D_MODEL)
    return pl.pallas_call(
        functools.partial(_project_kernel, feature_major=feature_major),
        grid=(n_tiles,),
        in_specs=[row(D_MODEL), full(g_row),
                  pl.BlockSpec(w_in_t.shape, lambda i: (0, 0), pipeline_mode=pl.Buffered(1)),
                  full(b_if_row), tab, tab, tab],
        out_specs=out_specs,
        out_shape=out_shape,
        compiler_params=_compiler_params(("parallel",)),
        name="project",
    )(x2d, g_row, w_in_t, b_if_row, rc, rs1, rs2)


def _select_topk(scores, block_idx, n_valid, axis):
    width = scores.shape[axis]
    sc = jnp.where(block_idx < n_valid, scores, NEG_INF)
    sel = jnp.zeros(scores.shape, jnp.bool_)
    for _ in range(MOBA_TOPK):
        mx = jnp.max(sc, axis=axis, keepdims=True)
        idx = jnp.min(jnp.where(sc == mx, block_idx, width), axis=axis, keepdims=True)
        hit = block_idx == idx
        sel = sel | hit
        sc = jnp.where(hit, NEG_INF, sc)
    return jnp.where(sel & (block_idx < n_valid), 1.0, 0.0)


FOLD_LAG = 1

def _moba_prompt_kernel(qt_ref, k_ref, vt_ref, kmean_ref, o_ref,
                        qtb_ref, sel_ref, s0_ref, s1_ref, m_ref, acc_ref, ot_ref, *, n_blocks):
    i = pl.program_id(1)
    tq = MOBA_BLOCK
    scale = DH_A ** -0.5
    key = lax.broadcasted_iota(jnp.int32, (tq, tq), 0)
    qry = lax.broadcasted_iota(jnp.int32, (tq, tq), 1)
    causal = key <= qry
    feat = lax.broadcasted_iota(jnp.int32, (LANES, tq), 0)
    blk = lax.broadcasted_iota(jnp.int32, (n_blocks, tq), 0)
    n_pairs = A_W // LANES
    pair_feats = [slice(p * LANES, (p + 1) * LANES) for p in range(n_pairs)]

    heads = [(p, p * HEADS_PER_LANE_GROUP + hh) for p in range(n_pairs) for hh in range(HEADS_PER_LANE_GROUP)]

    def score(h, p, block):
        start = block * tq if isinstance(block, int) else pl.multiple_of(block * tq, tq)
        k_blk = k_ref[0, pl.ds(start, tq), pair_feats[p]]
        return jnp.dot(k_blk, qtb_ref[h], preferred_element_type=F32)

    def fold(h, p, block, s, own):
        m_old = m_ref[h]
        if own:
            s = jnp.where(causal, s, NEG_INF)
            m_new = jnp.maximum(m_old, jnp.max(s, axis=0, keepdims=True))
            m_shift = m_exp = m_new
        else:
            picked = sel_ref[h * n_blocks + block] > 0.5
            m_new = jnp.maximum(m_old, jnp.where(picked, jnp.max(s, axis=0, keepdims=True), NEG_INF))
            m_shift = jnp.where(m_new == NEG_INF, 0.0, m_new)
            m_exp = jnp.where(picked, m_shift, jnp.inf)
        alpha = jnp.exp2(m_old - m_shift)
        pe = jnp.exp2((s - m_exp).astype(BF16))
        pv = jnp.dot(vt_ref[0, block, h], pe, preferred_element_type=F32)
        m_ref[h] = m_new
        acc_ref[h] = alpha * acc_ref[h] + pv

    for p in range(n_pairs):
        qt_pair = qt_ref[0, pair_feats[p], :]
        kmean_pair = kmean_ref[0, :, pair_feats[p]]
        for hh in range(HEADS_PER_LANE_GROUP):
            h = p * HEADS_PER_LANE_GROUP + hh
            qt_head = jnp.where((feat // DH_A) == hh, qt_pair, 0.0)
            scores = jnp.dot(kmean_pair, qt_head, precision=lax.Precision.HIGHEST, preferred_element_type=F32)
            sel = _select_topk(scores, blk, i, axis=0)
            for j in range(n_blocks):
                sel_ref[h * n_blocks + j] = sel[j:j + 1, :]
            qtb_ref[h] = (qt_head * (scale * LOG2_E)).astype(BF16)
            m_ref[h] = jnp.full((1, tq), NEG_INF, F32)
            acc_ref[h] = jnp.zeros((PV_ROWS, tq), F32)

    bufs = (s0_ref, s1_ref)

    def stage(block, cur, own, score_next=True):
        for idx, (p, h) in enumerate(heads):
            if score_next:
                bufs[1 - cur][h] = score(h, p, block + 1)
            if idx >= FOLD_LAG:
                p_f, h_f = heads[idx - FOLD_LAG]
                fold(h_f, p_f, block, bufs[cur][h_f], own)
        for p_f, h_f in heads[len(heads) - FOLD_LAG:]:
            fold(h_f, p_f, block, bufs[cur][h_f], own)

    for p, h in heads:
        s0_ref[h] = score(h, p, 0)

    def body(jj, carry):
        stage(2 * jj, 0, own=False)
        stage(2 * jj + 1, 1, own=False)
        return carry

    lax.fori_loop(0, i // 2, body, 0)

    @pl.when(i % 2 == 0)
    def _():
        stage(i, 0, own=True, score_next=False)

    @pl.when(i % 2 == 1)
    def _():
        stage(i - 1, 0, own=False)
        stage(i, 1, own=True, score_next=False)

    for _, h in heads:
        ot_ref[h * DH_A:(h + 1) * DH_A, :] = acc_ref[h, :DH_A, :] / acc_ref[h, DH_A:DH_A + 1, :]
    o_ref[...] = ot_ref[...].T.astype(o_ref.dtype)


def _moba_prompt(qt, k_bf, vt_bf, kmean, batch, seq):
    n_blocks = seq // MOBA_BLOCK
    tq = MOBA_BLOCK
    k3 = k_bf.reshape(batch, seq, A_W)
    km3 = kmean.reshape(batch, n_blocks, A_W)
    return pl.pallas_call(
        functools.partial(_moba_prompt_kernel, n_blocks=n_blocks),
        grid=(batch, n_blocks),
        in_specs=[pl.BlockSpec((1, A_W, tq), lambda b, i: (b, 0, i)),
                  pl.BlockSpec((1, seq, A_W), lambda b, i: (b, 0, 0)),
                  pl.BlockSpec((1, n_blocks, H_A, PV_ROWS, tq), lambda b, i: (b, 0, 0, 0, 0)),
                  pl.BlockSpec((1, n_blocks, A_W), lambda b, i: (b, 0, 0))],
        out_specs=pl.BlockSpec((tq, A_W), lambda b, i: (b * n_blocks + i, 0)),
        out_shape=jax.ShapeDtypeStruct((batch * seq, A_W), BF16),
        scratch_shapes=[pltpu.VMEM((H_A, LANES, tq), BF16),
                        pltpu.VMEM((H_A * n_blocks, 1, tq), F32),
                        pltpu.VMEM((H_A, tq, tq), F32),
                        pltpu.VMEM((H_A, tq, tq), F32),
                        pltpu.VMEM((H_A, 1, tq), F32),
                        pltpu.VMEM((H_A, PV_ROWS, tq), F32),
                        pltpu.VMEM((A_W, tq), F32)],
        compiler_params=_compiler_params(("parallel", "arbitrary")),
        name="moba_prompt",
    )(qt, k3, vt_bf, km3)


SAMPLE_RING_PAGES = 64
SAMPLE_BLOCKS_PER_ITER = 8


def _moba_sample_kernel(pt_ref, q_ref, kn_ref, vn_ref, ck_ref, cv_ref, o_ref,
                        ring_ref, ring_sem, qbd_ref, qbdt_ref, s_ref, score_ref, bmax_ref, acc_ref,
                        *, n_pages, n_blocks, page_rows, dec_seq):
    b = pl.program_id(0)
    n_seq = pl.num_programs(0)
    n_ring = SAMPLE_RING_PAGES
    stream_len = 2 * n_pages
    ppb = MOBA_BLOCK // page_rows
    n_q = H_A * dec_seq

    def ring_copy(pool_ref, page, slot):
        return pltpu.make_async_copy(pool_ref.at[page], ring_ref.at[slot], ring_sem.at[slot])

    def start_fetch(seq, pos, keys_only=False):
        slot = pos % n_ring
        if keys_only:
            ring_copy(ck_ref, pt_ref[seq, pos], slot).start()
            return

        @pl.when(pos < n_pages)
        def _():
            ring_copy(ck_ref, pt_ref[seq, pos], slot).start()

        @pl.when(pos >= n_pages)
        def _():
            ring_copy(cv_ref, pt_ref[seq, pos - n_pages], slot).start()

    def take(pos):
        slot = pos % n_ring
        ring_copy(ck_ref, 0, slot).wait()
        return slot

    def refill(pos):
        nxt = pos + n_ring

        @pl.when(nxt < stream_len)
        def _():
            start_fetch(b, nxt)

        @pl.when((nxt >= stream_len) & (b + 1 < n_seq))
        def _():
            start_fetch(b + 1, nxt - stream_len, keys_only=True)

    @pl.when(b == 0)
    def _():
        for pos in range(n_ring):
            start_fetch(0, pos, keys_only=True)

    scale = DH_A ** -0.5
    row = lax.broadcasted_iota(jnp.int32, (n_q, A_W), 0)
    lane_w = lax.broadcasted_iota(jnp.int32, (n_q, A_W), 1)
    head_diag = (row // dec_seq) == (lane_w // DH_A)
    lane = lax.broadcasted_iota(jnp.int32, (n_q, LANES), 1)
    tok = lax.broadcasted_iota(jnp.int32, (n_q, 1), 0) % dec_seq
    score_blk = lax.broadcasted_iota(jnp.int32, (LANES, LANES), 0)

    def page_cols(t):
        return slice(t * page_rows, (t + 1) * page_rows)

    q_rep = jnp.concatenate([q_ref[...]] * H_A, axis=0)
    qbd = jnp.where(head_diag, q_rep, 0.0)
    qbd_ref[...] = qbd
    qbdt_ref[...] = jnp.concatenate([qbd, jnp.zeros((LANES - n_q, A_W), F32)], axis=0).T
    score_ref[...] = jnp.full(score_ref.shape, NEG_INF, F32)
    bmax_ref[...] = jnp.full(bmax_ref.shape, NEG_INF, F32)

    unroll = SAMPLE_BLOCKS_PER_ITER
    pages_per_iter = unroll * ppb

    def k_blocks(it, carry):
        qb = (qbd_ref[...] * scale).astype(BF16)
        qbdt = qbdt_ref[...]
        pos0 = it * pages_per_iter
        slots = [take(pos0 + t) for t in range(pages_per_iter)]
        score, bmax = score_ref[...], bmax_ref[...]
        for u in range(unroll):
            jb = it * unroll + u
            kts = [ring_ref[slots[u * ppb + t]] for t in range(ppb)]
            kmean = jnp.sum(functools.reduce(lambda x, y: x + y, kts), axis=1, keepdims=True) * (1.0 / MOBA_BLOCK)
            sc = jnp.sum(qbdt * kmean, axis=0, keepdims=True)
            score = jnp.where(score_blk == jb, sc, score)
            s_pages = [jnp.dot(qb, kt.astype(BF16), preferred_element_type=F32) for kt in kts]
            for t in range(ppb):
                s_ref[jb, :, page_cols(t)] = s_pages[t]
            s_max = jnp.max(functools.reduce(jnp.maximum, s_pages), axis=1, keepdims=True)
            bmax = jnp.where(lane == jb, s_max, bmax)
        score_ref[...], bmax_ref[...] = score, bmax
        for t in range(pages_per_iter):
            refill(pos0 + t)
        return carry

    lax.fori_loop(0, n_blocks // unroll, k_blocks, 0)

    sel = _select_topk(score_ref[...].T[:n_q], lane, n_blocks, axis=1)
    qs = qbd * scale
    kn = kn_ref[...]
    own = []
    for t in range(dec_seq):
        so = jnp.sum(qs * kn[t:t + 1, :], axis=1, keepdims=True)
        own.append(jnp.where(t <= tok, so, NEG_INF))
    m = jnp.max(jnp.where(sel > 0.5, bmax_ref[...], NEG_INF), axis=1, keepdims=True)
    m = functools.reduce(jnp.maximum, own, m)
    p_own = jnp.zeros((n_q, LANES), F32)
    for t in range(dec_seq):
        p_own = jnp.where(lane == t, jnp.exp(own[t] - m), p_own)
    acc_ref[...] = jnp.zeros(acc_ref.shape, F32)

    def v_blocks(it, lsum):
        pos0 = n_pages + it * pages_per_iter
        slots = [take(pos0 + t) for t in range(pages_per_iter)]
        weights = []
        for u in range(unroll):
            jb = it * unroll + u
            picked = jnp.sum(jnp.where(lane == jb, sel, 0.0), axis=1, keepdims=True) > 0.5
            pj = jnp.exp(jnp.where(picked, s_ref[jb] - m, NEG_INF))
            lsum = lsum + pj
            weights.append(pj.astype(BF16))
        acc = acc_ref[...]
        for u in range(unroll):
            for t in range(ppb):
                vt = ring_ref[slots[u * ppb + t]].astype(BF16)
                acc = acc + lax.dot_general(weights[u][:, page_cols(t)], vt, NT_DIMS,
                                            preferred_element_type=F32)
        acc_ref[...] = acc
        for t in range(pages_per_iter):
            refill(pos0 + t)
        return lsum

    lsum = lax.fori_loop(0, n_blocks // unroll, v_blocks, jnp.zeros((n_q, MOBA_BLOCK), F32))

    acc = acc_ref[...]
    vn = vn_ref[...]
    for t in range(dec_seq):
        acc = acc + p_own[:, t:t + 1] * vn[t:t + 1, :]
    l = jnp.sum(lsum, axis=1, keepdims=True) + jnp.sum(p_own, axis=1, keepdims=True)
    out = jnp.where(head_diag, acc / l, 0.0)
    o_ref[...] = functools.reduce(
        lambda x, y: x + y, [out[h * dec_seq:(h + 1) * dec_seq, :] for h in range(H_A)])


def _moba_sample(qa, ka, va, cache_k, cache_v, page_table, dec_batch, dec_seq):
    n_pool, page_rows = cache_k.shape[0], cache_k.shape[1]
    n_pages = page_table.shape[1]
    past = n_pages * page_rows
    assert past % MOBA_BLOCK == 0 and MOBA_BLOCK % page_rows == 0, "cached rows must fill whole MoBA blocks"
    n_blocks = past // MOBA_BLOCK
    assert MOBA_TOPK <= n_blocks <= LANES
    n_ring = SAMPLE_RING_PAGES
    assert page_rows % LANES == 0 and n_ring <= n_pages and (2 * n_pages) % n_ring == 0
    assert n_blocks % SAMPLE_BLOCKS_PER_ITER == 0
    n_q = H_A * dec_seq
    assert n_q <= LANES
    ck = cache_k.transpose(0, 2, 3, 1).reshape(n_pool, A_W, page_rows)
    cv = cache_v.transpose(0, 2, 3, 1).reshape(n_pool, A_W, page_rows)

    tok_spec = pl.BlockSpec((dec_seq, A_W), lambda b, pt: (b, 0))
    pool_spec = pl.BlockSpec(memory_space=pl.ANY)
    grid_spec = pltpu.PrefetchScalarGridSpec(
        num_scalar_prefetch=1,
        grid=(dec_batch,),
        in_specs=[tok_spec, tok_spec, tok_spec, pool_spec, pool_spec],
        out_specs=tok_spec,
        scratch_shapes=[pltpu.VMEM((n_ring, A_W, page_rows), F32),
                        pltpu.SemaphoreType.DMA((n_ring,)),
                        pltpu.VMEM((n_q, A_W), F32),
                        pltpu.VMEM((A_W, LANES), F32),
                        pltpu.VMEM((n_blocks, n_q, MOBA_BLOCK), F32),
                        pltpu.VMEM((LANES, LANES), F32),
                        pltpu.VMEM((n_q, LANES), F32),
                        pltpu.VMEM((n_q, A_W), F32)],
    )
    return pl.pallas_call(
        functools.partial(_moba_sample_kernel, n_pages=n_pages, n_blocks=n_blocks,
                          page_rows=page_rows, dec_seq=dec_seq),
        grid_spec=grid_spec,
        out_shape=jax.ShapeDtypeStruct((dec_batch * dec_seq, A_W), F32),
        compiler_params=_compiler_params(("arbitrary",)),
        name="moba_sample",
    )(page_table, qa, ka, va, ck, cv)


MLSTM_SEQS_PER_STEP = 4


def _mlstm_kernel(q_ref, k_ref, v_ref, o_ref, gate_ref, c0_ref, n0_ref, m0_ref,
                  h_ref, c_ref, n_ref, m_ref, *, chunk):
    step = pl.program_id(1)
    lp = MLSTM_CHUNK
    n_seq = q_ref.shape[0]

    @pl.when(step == 0)
    def _():
        c_ref[...] = c0_ref[...]
        n_ref[...] = n0_ref[...]
        m_ref[...] = m0_ref[...]

    def pad_rows(a):
        if chunk == lp:
            return a
        return jnp.concatenate([a, jnp.zeros((lp - chunk, a.shape[1]), a.dtype)], axis=0)

    src = lax.broadcasted_iota(jnp.int32, (lp, lp), 0)
    tgt = lax.broadcasted_iota(jnp.int32, (lp, lp), 1)
    lane8 = lax.broadcasted_iota(jnp.int32, (N_GATE, lp), 1)
    chains = [(b, h) for b in range(n_seq) for h in range(H_M)]
    lanes = [slice(h * DH_M, (h + 1) * DH_M) for h in range(H_M)]
    last = slice(chunk - 1, chunk)

    q_all = [pad_rows(q_ref[b]) for b in range(n_seq)]
    k_all = [pad_rows(k_ref[b]) for b in range(n_seq)]
    v_all = [pad_rows(v_ref[b]) for b in range(n_seq)]
    qb = {(b, h): q_all[b][:, lanes[h]].astype(BF16) for b, h in chains}
    kb = {(b, h): k_all[b][:, lanes[h]].astype(BF16) for b, h in chains}

    s_kq = {ch: lax.dot_general(kb[ch], qb[ch], NT_DIMS, preferred_element_type=F32) for ch in chains}
    c_q = {(b, h): lax.dot_general(c_ref[b, h].astype(BF16), qb[b, h], NT_DIMS, preferred_element_type=F32)
           for b, h in chains}
    n_q = {(b, h): lax.dot_general(jnp.broadcast_to(n_ref[b, h], (N_GATE, DH_M)).astype(BF16), qb[b, h], NT_DIMS,
                                   preferred_element_type=F32)[0:1] for b, h in chains}

    g_row, b_row_all = [], []
    for b in range(n_seq):
        g = pad_rows(gate_ref[b])
        csum = g
        shift = 1
        while shift < lp:
            csum = csum + jnp.where(src >= shift, pltpu.roll(csum, shift, 0), 0.0)
            shift *= 2
        g_row.append(g.T[:N_GATE])
        b_row_all.append(csum.T[:N_GATE])

    v_t = {(b, h): v_all[b][:, lanes[h]].T for b, h in chains}
    w_d, w_prev, m_row, decay, w_s = {}, {}, {}, {}, {}
    for b, h in chains:
        i_row, b_row = g_row[b][h:h + 1], b_row_all[b][H_M + h:H_M + h + 1]
        m_prev = m_ref[b, h]
        u_src = jnp.broadcast_to(b_row - i_row, (lp, lp)).T
        log_d = jnp.where(src <= tgt, b_row - u_src, NEG_INF)
        m_row[b, h] = jnp.maximum(m_prev + b_row, jnp.max(log_d, axis=0, keepdims=True))
        w_d[b, h] = jnp.exp(log_d - m_row[b, h])
        w_prev[b, h] = jnp.exp(m_prev + b_row - m_row[b, h])
        m_new = m_row[b, h][:, last]
        b_last = b_row[:, last]
        decay[b, h] = jnp.exp(m_prev + b_last - m_new)
        w_s[b, h] = jnp.where(lane8[0:1] < chunk, jnp.exp(b_last - b_row + i_row - m_new), 0.0)
        m_ref[b, h] = m_new

    a = {ch: s_kq[ch] * w_d[ch] for ch in chains}
    v_a = {ch: jnp.dot(v_t[ch].astype(BF16), a[ch].astype(BF16), preferred_element_type=F32) for ch in chains}
    c_upd = {ch: jnp.dot((v_t[ch] * w_s[ch]).astype(BF16), kb[ch], preferred_element_type=F32) for ch in chains}
    n_upd = {ch: jnp.dot(jnp.broadcast_to(w_s[ch], (N_GATE, lp)).astype(BF16), kb[ch],
                         preferred_element_type=F32)[0:1] for ch in chains}

    for b, h in chains:
        num = v_a[b, h] + w_prev[b, h] * c_q[b, h]
        den = jnp.sum(a[b, h], axis=0, keepdims=True) + w_prev[b, h] * n_q[b, h]
        hc = (num / jnp.maximum(jnp.abs(den), jnp.exp(-m_row[b, h]))).T
        out = jax.nn.sigmoid(pad_rows(o_ref[b])[:, lanes[h]]) * hc
        h_ref[b, :, lanes[h]] = out[:chunk].astype(h_ref.dtype)
        c_ref[b, h] = decay[b, h] * c_ref[b, h] + c_upd[b, h]
        n_ref[b, h] = decay[b, h] * n_ref[b, h] + n_upd[b, h]


def _mlstm(qm, km, vm, om, gates, c0, n0, m0, batch, seq):
    chunk = math.gcd(seq, MLSTM_CHUNK)
    n_chunks = seq // chunk
    nb = MLSTM_SEQS_PER_STEP
    assert batch % nb == 0
    h_dtype = BF16 if chunk % BF16_SUBLANES == 0 else F32
    n0_4 = n0.reshape(batch, H_M, 1, DH_M)
    m0_4 = m0.reshape(batch, H_M, 1, 1)
    per_seq = lambda a: a.reshape(batch, seq, a.shape[-1])
    row = lambda w: pl.BlockSpec((nb, chunk, w), lambda g, s: (g, s, 0))
    state = lambda a: pl.BlockSpec((nb,) + a.shape[1:], lambda g, s: (g, 0, 0, 0))
    h, c, n, m = pl.pallas_call(
        functools.partial(_mlstm_kernel, chunk=chunk),
        grid=(batch // nb, n_chunks),
        in_specs=[row(M_W), row(M_W), row(M_W), row(M_W), row(LANES), state(c0), state(n0_4), state(m0_4)],
        out_specs=(row(M_W), state(c0), state(n0_4), state(m0_4)),
        out_shape=(jax.ShapeDtypeStruct((batch, seq, M_W), h_dtype),
                   jax.ShapeDtypeStruct(c0.shape, F32),
                   jax.ShapeDtypeStruct(n0_4.shape, F32),
                   jax.ShapeDtypeStruct(m0_4.shape, F32)),
        compiler_params=_compiler_params(("parallel", "arbitrary")),
        name="mlstm",
    )(per_seq(qm), per_seq(km), per_seq(vm), per_seq(om), per_seq(gates), c0, n0_4, m0_4)
    return h.reshape(batch * seq, M_W), c, n.reshape(batch, H_M, DH_M), m.reshape(batch, H_M)


def _merge_kernel(x_ref, att_ref, ml_ref, gate_a_ref, gate_m_ref, wa_ref, wm_ref, wo_ref, g_ref, o_ref):
    ya = jnp.dot(att_ref[...].astype(BF16), wa_ref[...], preferred_element_type=F32)
    ym = jnp.dot(ml_ref[...].astype(BF16), wm_ref[...], preferred_element_type=F32)
    u = gate_a_ref[...].astype(F32) * ya + gate_m_ref[...].astype(F32) * ym
    r = jnp.dot(u.astype(BF16), wo_ref[...], preferred_element_type=F32)
    o_ref[...] = x_ref[...] + _rms(r, g_ref[...])


def _merge(x2d, att, ml, ga, gm, wa, wm, wo, g_post_mix):
    n = x2d.shape[0]
    tm = min(512, n)
    row = lambda w: pl.BlockSpec((tm, w), lambda i: (i, 0))
    full = lambda a: pl.BlockSpec(a.shape, lambda i: (0,) * a.ndim)
    g_row = g_post_mix.reshape(1, D_MODEL)
    return pl.pallas_call(
        _merge_kernel,
        grid=(n // tm,),
        in_specs=[row(D_MODEL), row(A_W), row(M_W), row(D_MODEL), row(D_MODEL),
                  full(wa), full(wm), full(wo), full(g_row)],
        out_specs=row(D_MODEL),
        out_shape=jax.ShapeDtypeStruct((n, D_MODEL), F32),
        compiler_params=_compiler_params(("parallel",)),
        name="merge",
    )(x2d, att, ml, ga, gm, wa, wm, wo, g_row)


MLP_FF_TILE = 1024


def _mlp_kernel(x_ref, g1_ref, wu_ref, wd_ref, g2_ref, o_ref):
    x = x_ref[...]
    h = _rms(x, g1_ref[...]).astype(BF16)
    n_chunks = D_FF // MLP_FF_TILE
    cols = lambda c: slice(c * MLP_FF_TILE, (c + 1) * MLP_FF_TILE)
    up = lambda c: jnp.dot(h, wu_ref[:, cols(c)].astype(BF16), preferred_element_type=F32)
    pending = up(0)
    acc = None
    for c in range(n_chunks):
        act = jnp.square(jnp.maximum(pending, 0.0)).astype(BF16)
        if c + 1 < n_chunks:
            pending = up(c + 1)
        part = jnp.dot(act, wd_ref[cols(c), :].astype(BF16), preferred_element_type=F32)
        acc = part if acc is None else acc + part
    o_ref[...] = x + _rms(acc, g2_ref[...])


def _mlp(x2d, g_pre_mlp, w_up, w_down, g_post_mlp):
    n = x2d.shape[0]
    tm = min(512, n)
    g1 = g_pre_mlp.reshape(1, D_MODEL)
    g2 = g_post_mlp.reshape(1, D_MODEL)
    gspec = pl.BlockSpec((1, D_MODEL), lambda i: (0, 0))
    resident = lambda a: pl.BlockSpec(a.shape, lambda i: (0, 0), pipeline_mode=pl.Buffered(1))
    return pl.pallas_call(
        _mlp_kernel,
        grid=(n // tm,),
        in_specs=[pl.BlockSpec((tm, D_MODEL), lambda i: (i, 0)), gspec, resident(w_up), resident(w_down), gspec],
        out_specs=pl.BlockSpec((tm, D_MODEL), lambda i: (i, 0)),
        out_shape=jax.ShapeDtypeStruct((n, D_MODEL), F32),
        compiler_params=_compiler_params(("parallel",)),
        name="mlp",
    )(x2d, g1, w_up, w_down, g2)


def kernel(x_prompt, x_sample, cache_k, cache_v, state_C, state_n, state_m, page_table, g_pre_mix, w_in, b_if,
           w_attn_br, w_mlstm_br, w_out, g_post_mix, g_pre_mlp, w_up, w_down, g_post_mlp):
    bp, sp, _ = x_prompt.shape
    db, ds, _ = x_sample.shape
    past = page_table.shape[1] * cache_k.shape[1]

    w_in_t = w_in.T
    b_if_row = jnp.pad(b_if.astype(F32), (0, LANES - N_GATE)).reshape(1, LANES)
    wa, wm, wo = w_attn_br.astype(BF16), w_mlstm_br.astype(BF16), w_out.astype(BF16)
    wu, wd = w_up, w_down

    def tail(x2d, att, ml, ga, gm):
        x1 = _merge(x2d, att, ml, ga, gm, wa, wm, wo, g_post_mix)
        return _mlp(x1, g_pre_mlp, wu, wd, g_post_mlp)

    xp = x_prompt.reshape(bp * sp, D_MODEL)
    (qt, kt, vt, k_bf, vt_bf, kmean, qm, km, vm, om, ga, gm, gates) = _project(
        xp, jnp.arange(sp, dtype=F32), g_pre_mix, w_in_t, b_if_row, feature_major=True)
    att_p = _moba_prompt(qt, k_bf, vt_bf, kmean, bp, sp)
    ml_p, c_p, n_p, m_p = _mlstm(qm, km, vm, om, gates,
                                 jnp.zeros((bp, H_M, DH_M, DH_M), F32), jnp.zeros((bp, H_M, DH_M), F32),
                                 jnp.zeros((bp, H_M), F32), bp, sp)
    y_prompt = tail(xp, att_p, ml_p, ga, gm).reshape(bp, sp, D_MODEL)
    k_prompt = kt.reshape(bp, H_A, DH_A, sp).transpose(0, 3, 1, 2)
    v_prompt = vt.reshape(bp, H_A, DH_A, sp).transpose(0, 3, 1, 2)

    xs = x_sample.reshape(db * ds, D_MODEL)
    (qa, ka, va, qm, km, vm, om, ga, gm, gates) = _project(
        xs, past + jnp.arange(ds, dtype=F32), g_pre_mix, w_in_t, b_if_row, feature_major=False)
    att_s = _moba_sample(qa, ka, va, cache_k, cache_v, page_table, db, ds)
    ml_s, c_s, n_s, m_s = _mlstm(qm, km, vm, om, gates, state_C.astype(F32), state_n.astype(F32),
                                 state_m.astype(F32), db, ds)
    y_sample = tail(xs, att_s, ml_s, ga, gm).reshape(db, ds, D_MODEL)
    k_sample = ka.reshape(db, ds, H_A, DH_A)
    v_sample = va.reshape(db, ds, H_A, DH_A)

    return (y_prompt, y_sample, k_prompt, v_prompt, c_p, n_p, m_p, k_sample, v_sample, c_s, n_s, m_s)
```

```python
import functools
import math

import jax
import jax.numpy as jnp
from jax import lax
from jax.experimental import pallas as pl
from jax.experimental.pallas import tpu as pltpu

F32 = jnp.float32
BF16 = jnp.bfloat16
NEG_INF = float("-inf")
LOG2_E = math.log2(math.e)

D_MODEL = 1024
H_A = 8
DH_A = 64
A_W = H_A * DH_A
MOBA_BLOCK = 256
MOBA_TOPK = 3
ROT_DIMS = DH_A // 4
ROPE_THETA = 500000.0
H_M = 4
DH_M = 128
M_W = H_M * DH_M
MLSTM_CHUNK = 128
D_FF = 4 * D_MODEL
RMS_EPS = 1e-6
N_GATE = 2 * H_M

LANES = 128
BF16_SUBLANES = 16
HEADS_PER_LANE_GROUP = LANES // DH_A
VMEM_LIMIT_BYTES = 52 * 1024 * 1024

NT_DIMS = (((1,), (1,)), ((), ()))
HEAD_COLS = 3 * A_W + 4 * M_W


def _compiler_params(semantics):
    return pltpu.CompilerParams(dimension_semantics=semantics, vmem_limit_bytes=VMEM_LIMIT_BYTES)


def _rms(x, g):
    return x * lax.rsqrt(jnp.mean(x * x, axis=-1, keepdims=True) + RMS_EPS) * g


def _log_sigmoid(x):
    return jnp.minimum(x, 0.0) - jnp.log1p(jnp.exp(-jnp.abs(x)))


PROJ_ROWS = 2 * MOBA_BLOCK
PV_ROWS = DH_A + BF16_SUBLANES


def _project_kernel(x_ref, g_ref, wt_ref, bif_ref, rc_ref, rs1_ref, rs2_ref, *out_refs, feature_major):
    if feature_major:
        (qt_ref, kt_ref, vt_ref, kb_ref, vtb_ref, kmean_ref,
         qm_ref, km_ref, vm_ref, om_ref, ga_ref, gm_ref, gates_ref) = out_refs
    else:
        qa_ref, ka_ref, va_ref, qm_ref, km_ref, vm_ref, om_ref, ga_ref, gm_ref, gates_ref = out_refs
    hb = _rms(x_ref[...], g_ref[...]).astype(BF16)

    def mm(c0, n):
        return lax.dot_general(hb, wt_ref[c0:c0 + n, :].astype(BF16), NT_DIMS, preferred_element_type=F32)

    rc, rs1, rs2 = rc_ref[...], rs1_ref[...], rs2_ref[...]

    def rot(z):
        half = ROT_DIMS // 2
        outs = []
        for c in range(z.shape[1] // LANES):
            zc = z[:, c * LANES:(c + 1) * LANES]
            outs.append(zc * rc + pltpu.roll(zc, LANES - half, 1) * rs1 + pltpu.roll(zc, half, 1) * rs2)
        return jnp.concatenate(outs, axis=1)

    qa = rot(mm(0, A_W))
    ka = rot(mm(A_W, A_W))
    va = mm(2 * A_W, A_W)
    if feature_major:
        qt_ref[0] = qa.T
        kt_ref[0] = ka.T
        kb_ref[...] = ka.astype(BF16)
        vt = va.T
        vt_ref[0] = vt
        ones_row = (lax.broadcasted_iota(jnp.int32, (PV_ROWS - DH_A, MOBA_BLOCK), 0) == 0).astype(BF16)
        for j in range(ka.shape[0] // MOBA_BLOCK):
            rows = slice(j * MOBA_BLOCK, (j + 1) * MOBA_BLOCK)
            kmean_ref[j] = jnp.mean(ka[rows], axis=0, keepdims=True)
            for h in range(H_A):
                vtb_ref[0, j, h] = jnp.concatenate([vt[h * DH_A:(h + 1) * DH_A, rows].astype(BF16), ones_row], axis=0)
    else:
        qa_ref[...] = qa
        ka_ref[...] = ka
        va_ref[...] = va
    c0 = 3 * A_W
    qm_ref[...] = mm(c0, M_W).astype(qm_ref.dtype)
    km_ref[...] = (mm(c0 + M_W, M_W) * (DH_M ** -0.5)).astype(km_ref.dtype)
    vm_ref[...] = mm(c0 + 2 * M_W, M_W)
    om_ref[...] = mm(c0 + 3 * M_W, M_W)
    branch0 = HEAD_COLS + N_GATE
    ga_ref[...] = jax.nn.sigmoid(mm(branch0, D_MODEL)).astype(ga_ref.dtype)
    gm_ref[...] = jax.nn.sigmoid(mm(branch0 + D_MODEL, D_MODEL)).astype(gm_ref.dtype)
    zg = mm(HEAD_COLS, LANES) + bif_ref[...]
    lane = lax.broadcasted_iota(jnp.int32, zg.shape, 1)
    is_forget = (lane >= H_M) & (lane < N_GATE)
    gates_ref[...] = jnp.where(is_forget, _log_sigmoid(zg), zg)


def _rotary_tables(pos):
    half = ROT_DIMS // 2
    inv = ROPE_THETA ** (-jnp.arange(half, dtype=F32) * 2.0 / ROT_DIMS)
    ang = pos[:, None] * inv[None, :]
    cos, sin = jnp.cos(ang), jnp.sin(ang)
    n = pos.shape[0]
    pad = jnp.zeros((n, DH_A - ROT_DIMS), F32)
    c_head = jnp.concatenate([cos, cos, pad + 1.0], axis=1)
    s1_head = jnp.concatenate([-sin, jnp.zeros_like(sin), pad], axis=1)
    s2_head = jnp.concatenate([jnp.zeros_like(sin), sin, pad], axis=1)
    rep = lambda t: jnp.tile(t, (1, HEADS_PER_LANE_GROUP))
    return rep(c_head), rep(s1_head), rep(s2_head)


def _project(x2d, pos, g_pre_mix, w_in_t, b_if_row, feature_major):
    n = x2d.shape[0]
    tm = min(PROJ_ROWS, n)
    n_tiles = n // tm
    rc, rs1, rs2 = _rotary_tables(pos)
    if pos.shape[0] < tm:
        reps = tm // pos.shape[0]
        rc, rs1, rs2 = (jnp.tile(t, (reps, 1)) for t in (rc, rs1, rs2))
    tab_tiles = rc.shape[0] // tm
    row = lambda w: pl.BlockSpec((tm, w), lambda i: (i, 0))
    full = lambda a: pl.BlockSpec(a.shape, lambda i: (0,) * a.ndim)
    tab = pl.BlockSpec((tm, LANES), lambda i: (i % tab_tiles, 0))
    f32 = lambda w: jax.ShapeDtypeStruct((n, w), F32)
    if feature_major:
        seq = pos.shape[0]
        batch = n // seq
        t_shape = jax.ShapeDtypeStruct((batch, A_W, seq), F32)
        t_spec = pl.BlockSpec((1, A_W, tm), lambda i: (i // tab_tiles, 0, i % tab_tiles))
        bpt = tm // MOBA_BLOCK
        attn_shape = (t_shape, t_shape, t_shape, jax.ShapeDtypeStruct((n, A_W), BF16),
                      jax.ShapeDtypeStruct((batch, tab_tiles * bpt, H_A, PV_ROWS, MOBA_BLOCK), BF16),
                      jax.ShapeDtypeStruct((n_tiles * bpt, 1, A_W), F32))
        attn_specs = (t_spec, t_spec, t_spec, row(A_W),
                      pl.BlockSpec((1, bpt, H_A, PV_ROWS, MOBA_BLOCK),
                                   lambda i: (i // tab_tiles, i % tab_tiles, 0, 0, 0)),
                      pl.BlockSpec((bpt, 1, A_W), lambda i: (i, 0, 0)))
    else:
        attn_shape = (f32(A_W), f32(A_W), f32(A_W))
        attn_specs = (row(A_W), row(A_W), row(A_W))
    qk = jax.ShapeDtypeStruct((n, M_W), BF16 if feature_major else F32)
    gate = jax.ShapeDtypeStruct((n, D_MODEL), BF16)
    out_shape = attn_shape + (qk, qk, f32(M_W), f32(M_W), gate, gate, f32(LANES))
    out_specs = attn_specs + (row(M_W), row(M_W), row(M_W), row(M_W), row(D_MODEL), row(D_MODEL), row(LANES))
    g_row = g_pre_mix.reshape(1, D_MODEL)
    return pl.pallas_call(
        functools.partial(_project_kernel, feature_major=feature_major),
        grid=(n_tiles,),
        in_specs=[row(D_MODEL), full(g_row),
                  pl.BlockSpec(w_in_t.shape, lambda i: (0, 0), pipeline_mode=pl.Buffered(1)),
                  full(b_if_row), tab, tab, tab],
        out_specs=out_specs,
        out_shape=out_shape,
        compiler_params=_compiler_params(("parallel",)),
        name="project",
    )(x2d, g_row, w_in_t, b_if_row, rc, rs1, rs2)


def _select_topk(scores, block_idx, n_valid, axis):
    width = scores.shape[axis]
    sc = jnp.where(block_idx < n_valid, scores, NEG_INF)
    sel = jnp.zeros(scores.shape, jnp.bool_)
    for _ in range(MOBA_TOPK):
        mx = jnp.max(sc, axis=axis, keepdims=True)
        idx = jnp.min(jnp.where(sc == mx, block_idx, width), axis=axis, keepdims=True)
        hit = block_idx == idx
        sel = sel | hit
        sc = jnp.where(hit, NEG_INF, sc)
    return jnp.where(sel & (block_idx < n_valid), 1.0, 0.0)


FOLD_LAG = 1

def _moba_prompt_kernel(qt_ref, k_ref, vt_ref, kmean_ref, o_ref,
                        qtb_ref, sel_ref, s0_ref, s1_ref, m_ref, acc_ref, ot_ref, *, n_blocks):
    i = pl.program_id(1)
    tq = MOBA_BLOCK
    scale = DH_A ** -0.5
    key = lax.broadcasted_iota(jnp.int32, (tq, tq), 0)
    qry = lax.broadcasted_iota(jnp.int32, (tq, tq), 1)
    causal = key <= qry
    feat = lax.broadcasted_iota(jnp.int32, (LANES, tq), 0)
    blk = lax.broadcasted_iota(jnp.int32, (n_blocks, tq), 0)
    n_pairs = A_W // LANES
    pair_feats = [slice(p * LANES, (p + 1) * LANES) for p in range(n_pairs)]

    heads = [(p, p * HEADS_PER_LANE_GROUP + hh) for p in range(n_pairs) for hh in range(HEADS_PER_LANE_GROUP)]

    def score(h, p, block):
        start = block * tq if isinstance(block, int) else pl.multiple_of(block * tq, tq)
        k_blk = k_ref[0, pl.ds(start, tq), pair_feats[p]]
        return jnp.dot(k_blk, qtb_ref[h], preferred_element_type=F32)

    def fold(h, p, block, s, own):
        m_old = m_ref[h]
        if own:
            s = jnp.where(causal, s, NEG_INF)
            m_new = jnp.maximum(m_old, jnp.max(s, axis=0, keepdims=True))
            m_shift = m_exp = m_new
        else:
            picked = sel_ref[h * n_blocks + block] > 0.5
            m_new = jnp.maximum(m_old, jnp.where(picked, jnp.max(s, axis=0, keepdims=True), NEG_INF))
            m_shift = jnp.where(m_new == NEG_INF, 0.0, m_new)
            m_exp = jnp.where(picked, m_shift, jnp.inf)
        alpha = jnp.exp2(m_old - m_shift)
        pe = jnp.exp2((s - m_exp).astype(BF16))
        pv = jnp.dot(vt_ref[0, block, h], pe, preferred_element_type=F32)
        m_ref[h] = m_new
        acc_ref[h] = alpha * acc_ref[h] + pv

    for p in range(n_pairs):
        qt_pair = qt_ref[0, pair_feats[p], :]
        kmean_pair = kmean_ref[0, :, pair_feats[p]]
        for hh in range(HEADS_PER_LANE_GROUP):
            h = p * HEADS_PER_LANE_GROUP + hh
            qt_head = jnp.where((feat // DH_A) == hh, qt_pair, 0.0)
            scores = jnp.dot(kmean_pair, qt_head, precision=lax.Precision.HIGHEST, preferred_element_type=F32)
            sel = _select_topk(scores, blk, i, axis=0)
            for j in range(n_blocks):
                sel_ref[h * n_blocks + j] = sel[j:j + 1, :]
            qtb_ref[h] = (qt_head * (scale * LOG2_E)).astype(BF16)
            m_ref[h] = jnp.full((1, tq), NEG_INF, F32)
            acc_ref[h] = jnp.zeros((PV_ROWS, tq), F32)

    bufs = (s0_ref, s1_ref)

    def stage(block, cur, own, score_next=True):
        for idx, (p, h) in enumerate(heads):
            if score_next:
                bufs[1 - cur][h] = score(h, p, block + 1)
            if idx >= FOLD_LAG:
                p_f, h_f = heads[idx - FOLD_LAG]
                fold(h_f, p_f, block, bufs[cur][h_f], own)
        for p_f, h_f in heads[len(heads) - FOLD_LAG:]:
            fold(h_f, p_f, block, bufs[cur][h_f], own)

    for p, h in heads:
        s0_ref[h] = score(h, p, 0)

    def body(jj, carry):
        stage(2 * jj, 0, own=False)
        stage(2 * jj + 1, 1, own=False)
        return carry

    lax.fori_loop(0, i // 2, body, 0)

    @pl.when(i % 2 == 0)
    def _():
        stage(i, 0, own=True, score_next=False)

    @pl.when(i % 2 == 1)
    def _():
        stage(i - 1, 0, own=False)
        stage(i, 1, own=True, score_next=False)

    for _, h in heads:
        ot_ref[h * DH_A:(h + 1) * DH_A, :] = acc_ref[h, :DH_A, :] / acc_ref[h, DH_A:DH_A + 1, :]
    o_ref[...] = ot_ref[...].T.astype(o_ref.dtype)


def _moba_prompt(qt, k_bf, vt_bf, kmean, batch, seq):
    n_blocks = seq // MOBA_BLOCK
    tq = MOBA_BLOCK
    k3 = k_bf.reshape(batch, seq, A_W)
    km3 = kmean.reshape(batch, n_blocks, A_W)
    return pl.pallas_call(
        functools.partial(_moba_prompt_kernel, n_blocks=n_blocks),
        grid=(batch, n_blocks),
        in_specs=[pl.BlockSpec((1, A_W, tq), lambda b, i: (b, 0, i)),
                  pl.BlockSpec((1, seq, A_W), lambda b, i: (b, 0, 0)),
                  pl.BlockSpec((1, n_blocks, H_A, PV_ROWS, tq), lambda b, i: (b, 0, 0, 0, 0)),
                  pl.BlockSpec((1, n_blocks, A_W), lambda b, i: (b, 0, 0))],
        out_specs=pl.BlockSpec((tq, A_W), lambda b, i: (b * n_blocks + i, 0)),
        out_shape=jax.ShapeDtypeStruct((batch * seq, A_W), BF16),
        scratch_shapes=[pltpu.VMEM((H_A, LANES, tq), BF16),
                        pltpu.VMEM((H_A * n_blocks, 1, tq), F32),
                        pltpu.VMEM((H_A, tq, tq), F32),
                        pltpu.VMEM((H_A, tq, tq), F32),
                        pltpu.VMEM((H_A, 1, tq), F32),
                        pltpu.VMEM((H_A, PV_ROWS, tq), F32),
                        pltpu.VMEM((A_W, tq), F32)],
        compiler_params=_compiler_params(("parallel", "arbitrary")),
        name="moba_prompt",
    )(qt, k3, vt_bf, km3)


SAMPLE_RING_PAGES = 64
SAMPLE_BLOCKS_PER_ITER = 8


def _moba_sample_kernel(pt_ref, q_ref, kn_ref, vn_ref, ck_ref, cv_ref, o_ref,
                        ring_ref, ring_sem, qbd_ref, qbdt_ref, s_ref, score_ref, bmax_ref, acc_ref,
                        *, n_pages, n_blocks, page_rows, dec_seq):
    b = pl.program_id(0)
    n_seq = pl.num_programs(0)
    n_ring = SAMPLE_RING_PAGES
    stream_len = 2 * n_pages
    ppb = MOBA_BLOCK // page_rows
    n_q = H_A * dec_seq

    def ring_copy(pool_ref, page, slot):
        return pltpu.make_async_copy(pool_ref.at[page], ring_ref.at[slot], ring_sem.at[slot])

    def start_fetch(seq, pos, queue, keys_only=False):
        slot = pos % n_ring
        if keys_only:
            ring_copy(ck_ref, pt_ref[seq, pos], slot).start(priority=queue)
            return

        @pl.when(pos < n_pages)
        def _():
            ring_copy(ck_ref, pt_ref[seq, pos], slot).start(priority=queue)

        @pl.when(pos >= n_pages)
        def _():
            ring_copy(cv_ref, pt_ref[seq, pos - n_pages], slot).start(priority=queue)

    def take(pos):
        slot = pos % n_ring
        ring_copy(ck_ref, 0, slot).wait()
        return slot

    def refill(pos, queue):
        nxt = pos + n_ring

        @pl.when(nxt < stream_len)
        def _():
            start_fetch(b, nxt, queue)

        @pl.when((nxt >= stream_len) & (b + 1 < n_seq))
        def _():
            start_fetch(b + 1, nxt - stream_len, queue, keys_only=True)

    @pl.when(b == 0)
    def _():
        for pos in range(n_ring):
            start_fetch(0, pos, pos % 2, keys_only=True)

    scale = DH_A ** -0.5
    row = lax.broadcasted_iota(jnp.int32, (n_q, A_W), 0)
    lane_w = lax.broadcasted_iota(jnp.int32, (n_q, A_W), 1)
    head_diag = (row // dec_seq) == (lane_w // DH_A)
    lane = lax.broadcasted_iota(jnp.int32, (n_q, LANES), 1)
    tok = lax.broadcasted_iota(jnp.int32, (n_q, 1), 0) % dec_seq
    score_blk = lax.broadcasted_iota(jnp.int32, (LANES, LANES), 0)

    def page_cols(t):
        return slice(t * page_rows, (t + 1) * page_rows)

    q_rep = jnp.concatenate([q_ref[...]] * H_A, axis=0)
    qbd = jnp.where(head_diag, q_rep, 0.0)
    qbd_ref[...] = qbd
    qbdt_ref[...] = jnp.concatenate([qbd, jnp.zeros((LANES - n_q, A_W), F32)], axis=0).T
    score_ref[...] = jnp.full(score_ref.shape, NEG_INF, F32)
    bmax_ref[...] = jnp.full(bmax_ref.shape, NEG_INF, F32)

    unroll = SAMPLE_BLOCKS_PER_ITER
    pages_per_iter = unroll * ppb

    def k_blocks(it, carry):
        qb = (qbd_ref[...] * scale).astype(BF16)
        qbdt = qbdt_ref[...]
        pos0 = it * pages_per_iter
        slots = [take(pos0 + t) for t in range(pages_per_iter)]
        score, bmax = score_ref[...], bmax_ref[...]
        for u in range(unroll):
            jb = it * unroll + u
            kts = [ring_ref[slots[u * ppb + t]] for t in range(ppb)]
            kmean = jnp.sum(functools.reduce(lambda x, y: x + y, kts), axis=1, keepdims=True) * (1.0 / MOBA_BLOCK)
            sc = jnp.sum(qbdt * kmean, axis=0, keepdims=True)
            score = jnp.where(score_blk == jb, sc, score)
            s_pages = [jnp.dot(qb, kt.astype(BF16), preferred_element_type=F32) for kt in kts]
            for t in range(ppb):
                s_ref[jb, :, page_cols(t)] = s_pages[t]
            s_max = jnp.max(functools.reduce(jnp.maximum, s_pages), axis=1, keepdims=True)
            bmax = jnp.where(lane == jb, s_max, bmax)
        score_ref[...], bmax_ref[...] = score, bmax
        for t in range(pages_per_iter):
            refill(pos0 + t, t % 2)
        return carry

    lax.fori_loop(0, n_blocks // unroll, k_blocks, 0)

    sel = _select_topk(score_ref[...].T[:n_q], lane, n_blocks, axis=1)
    qs = qbd * scale
    kn = kn_ref[...]
    own = []
    for t in range(dec_seq):
        so = jnp.sum(qs * kn[t:t + 1, :], axis=1, keepdims=True)
        own.append(jnp.where(t <= tok, so, NEG_INF))
    m = jnp.max(jnp.where(sel > 0.5, bmax_ref[...], NEG_INF), axis=1, keepdims=True)
    m = functools.reduce(jnp.maximum, own, m)
    p_own = jnp.zeros((n_q, LANES), F32)
    for t in range(dec_seq):
        p_own = jnp.where(lane == t, jnp.exp(own[t] - m), p_own)
    acc_ref[...] = jnp.zeros(acc_ref.shape, F32)

    def v_blocks(it, lsum):
        pos0 = n_pages + it * pages_per_iter
        slots = [take(pos0 + t) for t in range(pages_per_iter)]
        weights = []
        for u in range(unroll):
            jb = it * unroll + u
            picked = jnp.sum(jnp.where(lane == jb, sel, 0.0), axis=1, keepdims=True) > 0.5
            pj = jnp.exp(jnp.where(picked, s_ref[jb] - m, NEG_INF))
            lsum = lsum + pj
            weights.append(pj.astype(BF16))
        acc = acc_ref[...]
        for u in range(unroll):
            for t in range(ppb):
                vt = ring_ref[slots[u * ppb + t]].astype(BF16)
                acc = acc + lax.dot_general(weights[u][:, page_cols(t)], vt, NT_DIMS,
                                            preferred_element_type=F32)
        acc_ref[...] = acc
        for t in range(pages_per_iter):
            refill(pos0 + t, t % 2)
        return lsum

    lsum = lax.fori_loop(0, n_blocks // unroll, v_blocks, jnp.zeros((n_q, MOBA_BLOCK), F32))

    acc = acc_ref[...]
    vn = vn_ref[...]
    for t in range(dec_seq):
        acc = acc + p_own[:, t:t + 1] * vn[t:t + 1, :]
    l = jnp.sum(lsum, axis=1, keepdims=True) + jnp.sum(p_own, axis=1, keepdims=True)
    out = jnp.where(head_diag, acc / l, 0.0)
    o_ref[...] = functools.reduce(
        lambda x, y: x + y, [out[h * dec_seq:(h + 1) * dec_seq, :] for h in range(H_A)])


def _moba_sample(qa, ka, va, cache_k, cache_v, page_table, dec_batch, dec_seq):
    n_pool, page_rows = cache_k.shape[0], cache_k.shape[1]
    n_pages = page_table.shape[1]
    past = n_pages * page_rows
    assert past % MOBA_BLOCK == 0 and MOBA_BLOCK % page_rows == 0, "cached rows must fill whole MoBA blocks"
    n_blocks = past // MOBA_BLOCK
    assert MOBA_TOPK <= n_blocks <= LANES
    n_ring = SAMPLE_RING_PAGES
    assert page_rows % LANES == 0 and n_ring <= n_pages and (2 * n_pages) % n_ring == 0
    assert n_blocks % SAMPLE_BLOCKS_PER_ITER == 0
    n_q = H_A * dec_seq
    assert n_q <= LANES
    ck = cache_k.transpose(0, 2, 3, 1).reshape(n_pool, A_W, page_rows)
    cv = cache_v.transpose(0, 2, 3, 1).reshape(n_pool, A_W, page_rows)

    tok_spec = pl.BlockSpec((dec_seq, A_W), lambda b, pt: (b, 0))
    pool_spec = pl.BlockSpec(memory_space=pl.ANY)
    grid_spec = pltpu.PrefetchScalarGridSpec(
        num_scalar_prefetch=1,
        grid=(dec_batch,),
        in_specs=[tok_spec, tok_spec, tok_spec, pool_spec, pool_spec],
        out_specs=tok_spec,
        scratch_shapes=[pltpu.VMEM((n_ring, A_W, page_rows), F32),
                        pltpu.SemaphoreType.DMA((n_ring,)),
                        pltpu.VMEM((n_q, A_W), F32),
                        pltpu.VMEM((A_W, LANES), F32),
                        pltpu.VMEM((n_blocks, n_q, MOBA_BLOCK), F32),
                        pltpu.VMEM((LANES, LANES), F32),
                        pltpu.VMEM((n_q, LANES), F32),
                        pltpu.VMEM((n_q, A_W), F32)],
    )
    return pl.pallas_call(
        functools.partial(_moba_sample_kernel, n_pages=n_pages, n_blocks=n_blocks,
                          page_rows=page_rows, dec_seq=dec_seq),
        grid_spec=grid_spec,
        out_shape=jax.ShapeDtypeStruct((dec_batch * dec_seq, A_W), F32),
        compiler_params=_compiler_params(("arbitrary",)),
        name="moba_sample",
    )(page_table, qa, ka, va, ck, cv)


MLSTM_SEQS_PER_STEP = 4


def _mlstm_kernel(q_ref, k_ref, v_ref, o_ref, gate_ref, c0_ref, n0_ref, m0_ref,
                  h_ref, c_ref, n_ref, m_ref, *, chunk):
    step = pl.program_id(1)
    lp = MLSTM_CHUNK
    n_seq = q_ref.shape[0]

    @pl.when(step == 0)
    def _():
        c_ref[...] = c0_ref[...]
        n_ref[...] = n0_ref[...]
        m_ref[...] = m0_ref[...]

    def pad_rows(a):
        if chunk == lp:
            return a
        return jnp.concatenate([a, jnp.zeros((lp - chunk, a.shape[1]), a.dtype)], axis=0)

    src = lax.broadcasted_iota(jnp.int32, (lp, lp), 0)
    tgt = lax.broadcasted_iota(jnp.int32, (lp, lp), 1)
    lane8 = lax.broadcasted_iota(jnp.int32, (N_GATE, lp), 1)
    chains = [(b, h) for b in range(n_seq) for h in range(H_M)]
    lanes = [slice(h * DH_M, (h + 1) * DH_M) for h in range(H_M)]
    last = slice(chunk - 1, chunk)

    q_all = [pad_rows(q_ref[b]) for b in range(n_seq)]
    k_all = [pad_rows(k_ref[b]) for b in range(n_seq)]
    v_all = [pad_rows(v_ref[b]) for b in range(n_seq)]
    qb = {(b, h): q_all[b][:, lanes[h]].astype(BF16) for b, h in chains}
    kb = {(b, h): k_all[b][:, lanes[h]].astype(BF16) for b, h in chains}

    s_kq = {ch: lax.dot_general(kb[ch], qb[ch], NT_DIMS, preferred_element_type=F32) for ch in chains}
    c_q = {(b, h): lax.dot_general(c_ref[b, h].astype(BF16), qb[b, h], NT_DIMS, preferred_element_type=F32)
           for b, h in chains}
    n_q = {(b, h): lax.dot_general(jnp.broadcast_to(n_ref[b, h], (N_GATE, DH_M)).astype(BF16), qb[b, h], NT_DIMS,
                                   preferred_element_type=F32)[0:1] for b, h in chains}

    g_row, b_row_all = [], []
    for b in range(n_seq):
        g = pad_rows(gate_ref[b])
        csum = g
        shift = 1
        while shift < lp:
            csum = csum + jnp.where(src >= shift, pltpu.roll(csum, shift, 0), 0.0)
            shift *= 2
        g_row.append(g.T[:N_GATE])
        b_row_all.append(csum.T[:N_GATE])

    v_t = {(b, h): v_all[b][:, lanes[h]].T for b, h in chains}
    w_d, w_prev, m_row, decay, w_s = {}, {}, {}, {}, {}
    for b, h in chains:
        i_row, b_row = g_row[b][h:h + 1], b_row_all[b][H_M + h:H_M + h + 1]
        m_prev = m_ref[b, h]
        u_src = jnp.broadcast_to(b_row - i_row, (lp, lp)).T
        log_d = jnp.where(src <= tgt, b_row - u_src, NEG_INF)
        m_row[b, h] = jnp.maximum(m_prev + b_row, jnp.max(log_d, axis=0, keepdims=True))
        w_d[b, h] = jnp.exp(log_d - m_row[b, h])
        w_prev[b, h] = jnp.exp(m_prev + b_row - m_row[b, h])
        m_new = m_row[b, h][:, last]
        b_last = b_row[:, last]
        decay[b, h] = jnp.exp(m_prev + b_last - m_new)
        w_s[b, h] = jnp.where(lane8[0:1] < chunk, jnp.exp(b_last - b_row + i_row - m_new), 0.0)
        m_ref[b, h] = m_new

    a = {ch: s_kq[ch] * w_d[ch] for ch in chains}
    v_a = {ch: jnp.dot(v_t[ch].astype(BF16), a[ch].astype(BF16), preferred_element_type=F32) for ch in chains}
    c_upd = {ch: jnp.dot((v_t[ch] * w_s[ch]).astype(BF16), kb[ch], preferred_element_type=F32) for ch in chains}
    n_upd = {ch: jnp.dot(jnp.broadcast_to(w_s[ch], (N_GATE, lp)).astype(BF16), kb[ch],
                         preferred_element_type=F32)[0:1] for ch in chains}

    for b, h in chains:
        num = v_a[b, h] + w_prev[b, h] * c_q[b, h]
        den = jnp.sum(a[b, h], axis=0, keepdims=True) + w_prev[b, h] * n_q[b, h]
        hc = (num / jnp.maximum(jnp.abs(den), jnp.exp(-m_row[b, h]))).T
        out = jax.nn.sigmoid(pad_rows(o_ref[b])[:, lanes[h]]) * hc
        h_ref[b, :, lanes[h]] = out[:chunk].astype(h_ref.dtype)
        c_ref[b, h] = decay[b, h] * c_ref[b, h] + c_upd[b, h]
        n_ref[b, h] = decay[b, h] * n_ref[b, h] + n_upd[b, h]


def _mlstm(qm, km, vm, om, gates, c0, n0, m0, batch, seq):
    chunk = math.gcd(seq, MLSTM_CHUNK)
    n_chunks = seq // chunk
    nb = MLSTM_SEQS_PER_STEP
    assert batch % nb == 0
    h_dtype = BF16 if chunk % BF16_SUBLANES == 0 else F32
    n0_4 = n0.reshape(batch, H_M, 1, DH_M)
    m0_4 = m0.reshape(batch, H_M, 1, 1)
    per_seq = lambda a: a.reshape(batch, seq, a.shape[-1])
    row = lambda w: pl.BlockSpec((nb, chunk, w), lambda g, s: (g, s, 0))
    state = lambda a: pl.BlockSpec((nb,) + a.shape[1:], lambda g, s: (g, 0, 0, 0))
    h, c, n, m = pl.pallas_call(
        functools.partial(_mlstm_kernel, chunk=chunk),
        grid=(batch // nb, n_chunks),
        in_specs=[row(M_W), row(M_W), row(M_W), row(M_W), row(LANES), state(c0), state(n0_4), state(m0_4)],
        out_specs=(row(M_W), state(c0), state(n0_4), state(m0_4)),
        out_shape=(jax.ShapeDtypeStruct((batch, seq, M_W), h_dtype),
                   jax.ShapeDtypeStruct(c0.shape, F32),
                   jax.ShapeDtypeStruct(n0_4.shape, F32),
                   jax.ShapeDtypeStruct(m0_4.shape, F32)),
        compiler_params=_compiler_params(("parallel", "arbitrary")),
        name="mlstm",
    )(per_seq(qm), per_seq(km), per_seq(vm), per_seq(om), per_seq(gates), c0, n0_4, m0_4)
    return h.reshape(batch * seq, M_W), c, n.reshape(batch, H_M, DH_M), m.reshape(batch, H_M)


def _merge_kernel(x_ref, att_ref, ml_ref, gate_a_ref, gate_m_ref, wa_ref, wm_ref, wo_ref, g_ref, o_ref):
    ya = jnp.dot(att_ref[...].astype(BF16), wa_ref[...], preferred_element_type=F32)
    ym = jnp.dot(ml_ref[...].astype(BF16), wm_ref[...], preferred_element_type=F32)
    u = gate_a_ref[...].astype(F32) * ya + gate_m_ref[...].astype(F32) * ym
    r = jnp.dot(u.astype(BF16), wo_ref[...], preferred_element_type=F32)
    o_ref[...] = x_ref[...] + _rms(r, g_ref[...])


def _merge(x2d, att, ml, ga, gm, wa, wm, wo, g_post_mix):
    n = x2d.shape[0]
    tm = min(512, n)
    row = lambda w: pl.BlockSpec((tm, w), lambda i: (i, 0))
    full = lambda a: pl.BlockSpec(a.shape, lambda i: (0,) * a.ndim)
    g_row = g_post_mix.reshape(1, D_MODEL)
    return pl.pallas_call(
        _merge_kernel,
        grid=(n // tm,),
        in_specs=[row(D_MODEL), row(A_W), row(M_W), row(D_MODEL), row(D_MODEL),
                  full(wa), full(wm), full(wo), full(g_row)],
        out_specs=row(D_MODEL),
        out_shape=jax.ShapeDtypeStruct((n, D_MODEL), F32),
        compiler_params=_compiler_params(("parallel",)),
        name="merge",
    )(x2d, att, ml, ga, gm, wa, wm, wo, g_row)


MLP_FF_TILE = 1024


def _mlp_kernel(x_ref, g1_ref, wu_ref, wd_ref, g2_ref, o_ref):
    x = x_ref[...]
    h = _rms(x, g1_ref[...]).astype(BF16)
    n_chunks = D_FF // MLP_FF_TILE
    cols = lambda c: slice(c * MLP_FF_TILE, (c + 1) * MLP_FF_TILE)
    up = lambda c: jnp.dot(h, wu_ref[:, cols(c)].astype(BF16), preferred_element_type=F32)
    pending = up(0)
    acc = None
    for c in range(n_chunks):
        act = jnp.square(jnp.maximum(pending, 0.0)).astype(BF16)
        if c + 1 < n_chunks:
            pending = up(c + 1)
        part = jnp.dot(act, wd_ref[cols(c), :].astype(BF16), preferred_element_type=F32)
        acc = part if acc is None else acc + part
    o_ref[...] = x + _rms(acc, g2_ref[...])


def _mlp(x2d, g_pre_mlp, w_up, w_down, g_post_mlp):
    n = x2d.shape[0]
    tm = min(512, n)
    g1 = g_pre_mlp.reshape(1, D_MODEL)
    g2 = g_post_mlp.reshape(1, D_MODEL)
    gspec = pl.BlockSpec((1, D_MODEL), lambda i: (0, 0))
    resident = lambda a: pl.BlockSpec(a.shape, lambda i: (0, 0), pipeline_mode=pl.Buffered(1))
    return pl.pallas_call(
        _mlp_kernel,
        grid=(n // tm,),
        in_specs=[pl.BlockSpec((tm, D_MODEL), lambda i: (i, 0)), gspec, resident(w_up), resident(w_down), gspec],
        out_specs=pl.BlockSpec((tm, D_MODEL), lambda i: (i, 0)),
        out_shape=jax.ShapeDtypeStruct((n, D_MODEL), F32),
        compiler_params=_compiler_params(("parallel",)),
        name="mlp",
    )(x2d, g1, w_up, w_down, g2)


def kernel(x_prompt, x_sample, cache_k, cache_v, state_C, state_n, state_m, page_table, g_pre_mix, w_in, b_if,
           w_attn_br, w_mlstm_br, w_out, g_post_mix, g_pre_mlp, w_up, w_down, g_post_mlp):
    bp, sp, _ = x_prompt.shape
    db, ds, _ = x_sample.shape
    past = page_table.shape[1] * cache_k.shape[1]

    w_in_t = w_in.T
    b_if_row = jnp.pad(b_if.astype(F32), (0, LANES - N_GATE)).reshape(1, LANES)
    wa, wm, wo = w_attn_br.astype(BF16), w_mlstm_br.astype(BF16), w_out.astype(BF16)
    wu, wd = w_up, w_down

    def tail(x2d, att, ml, ga, gm):
        x1 = _merge(x2d, att, ml, ga, gm, wa, wm, wo, g_post_mix)
        return _mlp(x1, g_pre_mlp, wu, wd, g_post_mlp)

    xp = x_prompt.reshape(bp * sp, D_MODEL)
    (qt, kt, vt, k_bf, vt_bf, kmean, qm, km, vm, om, ga, gm, gates) = _project(
        xp, jnp.arange(sp, dtype=F32), g_pre_mix, w_in_t, b_if_row, feature_major=True)
    att_p = _moba_prompt(qt, k_bf, vt_bf, kmean, bp, sp)
    ml_p, c_p, n_p, m_p = _mlstm(qm, km, vm, om, gates,
                                 jnp.zeros((bp, H_M, DH_M, DH_M), F32), jnp.zeros((bp, H_M, DH_M), F32),
                                 jnp.zeros((bp, H_M), F32), bp, sp)
    y_prompt = tail(xp, att_p, ml_p, ga, gm).reshape(bp, sp, D_MODEL)
    k_prompt = kt.reshape(bp, H_A, DH_A, sp).transpose(0, 3, 1, 2)
    v_prompt = vt.reshape(bp, H_A, DH_A, sp).transpose(0, 3, 1, 2)

    xs = x_sample.reshape(db * ds, D_MODEL)
    (qa, ka, va, qm, km, vm, om, ga, gm, gates) = _project(
        xs, past + jnp.arange(ds, dtype=F32), g_pre_mix, w_in_t, b_if_row, feature_major=False)
    att_s = _moba_sample(qa, ka, va, cache_k, cache_v, page_table, db, ds)
    ml_s, c_s, n_s, m_s = _mlstm(qm, km, vm, om, gates, state_C.astype(F32), state_n.astype(F32),
                                 state_m.astype(F32), db, ds)
    y_sample = tail(xs, att_s, ml_s, ga, gm).reshape(db, ds, D_MODEL)
    k_sample = ka.reshape(db, ds, H_A, DH_A)
    v_sample = va.reshape(db, ds, H_A, DH_A)

    return (y_prompt, y_sample, k_prompt, v_prompt, c_p, n_p, m_p, k_sample, v_sample, c_s, n_s, m_s)
```
